```python
import math
import jax, jax.numpy as jnp
from jax import lax
import numpy as np

D_MODEL = 1024
BATCH = 1
SEQ = 16384
DEPTH = 1

SSM_EXPAND = 2
SSM_D_INNER = SSM_EXPAND * D_MODEL
SSM_HEADDIM = 64
SSM_HEADS = SSM_D_INNER // SSM_HEADDIM
SSM_GROUPS = 4
SSM_HEADS_PER_GROUP = SSM_HEADS // SSM_GROUPS
SSM_STATE = 128
SSM_CONV = 4
SSM_CHUNK = 128
SSM_CONV_DIM = SSM_D_INNER + 2 * SSM_GROUPS * SSM_STATE
SSM_DT_MIN = 0.001
SSM_DT_MAX = 0.1

ATTN_HEADS = 16
ATTN_KV_HEADS = 2
ATTN_HEADDIM = 64
ATTN_GROUP = ATTN_HEADS // ATTN_KV_HEADS
WINDOW = 128
REL_BUCKETS = 32
REL_MAX_DIST = 128

D_FF = 2816
FFN_CONV = 3

DEEPNORM_ALPHA = (2.0 * DEPTH) ** 0.25
DEEPNORM_BETA = (8.0 * DEPTH) ** -0.25
LN_EPS = 1e-5
RMS_EPS = 1e-5

Z_COLS = SSM_D_INNER
XBC_COLS = SSM_CONV_DIM
DT_COLS = SSM_HEADS
Q_COLS = ATTN_HEADS * ATTN_HEADDIM
KV_COLS = ATTN_KV_HEADS * ATTN_HEADDIM
GATE_COLS = 2 * D_MODEL
IN_COLS = Z_COLS + XBC_COLS + DT_COLS + Q_COLS + 2 * KV_COLS + GATE_COLS
SPLIT_POINTS = [Z_COLS,
                Z_COLS + XBC_COLS,
                Z_COLS + XBC_COLS + DT_COLS,
                Z_COLS + XBC_COLS + DT_COLS + Q_COLS,
                Z_COLS + XBC_COLS + DT_COLS + Q_COLS + KV_COLS,
                Z_COLS + XBC_COLS + DT_COLS + Q_COLS + 2 * KV_COLS]

kernel_name = 'hybrid_ssd_swa_sink_convffn_deepnorm'


def layer_norm(x, g, b):
    xf = x.astype(jnp.float32)
    mu = jnp.mean(xf, axis=-1, keepdims=True)
    xc = xf - mu
    var = jnp.mean(xc * xc, axis=-1, keepdims=True)
    y = xc * lax.rsqrt(var + LN_EPS) * g.astype(jnp.float32) + b.astype(jnp.float32)
    return y.astype(x.dtype)


def causal_depthwise_conv(u, w, b):
    k, c = w.shape
    out = lax.conv_general_dilated(u, w[:, None, :].astype(u.dtype), window_strides=(1,),
                                   padding=[(k - 1, 0)],
                                   dimension_numbers=('NWC', 'WIO', 'NWC'),
                                   feature_group_count=c)
    return out + b.astype(u.dtype)


def ssd_chunked(xs, dt, a, bm, cm):
    b, s = xs.shape[:2]
    nc, lc = s // SSM_CHUNK, SSM_CHUNK
    G, E, P, N = SSM_GROUPS, SSM_HEADS_PER_GROUP, SSM_HEADDIM, SSM_STATE
    x = xs.reshape(b, nc, lc, G, E, P)
    dt_c = dt.reshape(b, nc, lc, G, E)
    bm = bm.reshape(b, nc, lc, G, N)
    cm = cm.reshape(b, nc, lc, G, N)
    a_dt = jnp.moveaxis(dt_c * a.reshape(G, E), 2, -1)
    a_cs = jnp.cumsum(a_dt, axis=-1)
    xdt = x * dt_c[..., None]
    seg = a_cs[..., :, None] - a_cs[..., None, :]
    causal = jnp.tril(jnp.ones((lc, lc), dtype=bool))
    decay = jnp.exp(jnp.where(causal, seg, -jnp.inf))
    cb = jnp.einsum('bclgn,bcsgn->bcgls', cm, bm)
    y_diag = jnp.einsum('bcgels,bcsgep->bclgep', cb[:, :, :, None] * decay, xdt)
    decay_states = jnp.moveaxis(jnp.exp(a_cs[..., -1:] - a_cs), -1, 2)
    states = jnp.einsum('bclgn,bclgep->bcgepn', bm, xdt * decay_states[..., None])
    chunk_decay = jnp.exp(a_cs[..., -1])

    def step(h, inp):
        st, dec = inp
        return h * dec[..., None, None] + st, h

    h0 = jnp.zeros((b, G, E, P, N), jnp.float32)
    _, prev = lax.scan(step, h0, (jnp.moveaxis(states, 1, 0), jnp.moveaxis(chunk_decay, 1, 0)))
    prev = jnp.moveaxis(prev, 0, 1)
    state_decay_out = jnp.moveaxis(jnp.exp(a_cs), -1, 2)
    y_off = jnp.einsum('bclgn,bcgepn->bclgep', cm, prev) * state_decay_out[..., None]
    return (y_diag + y_off).reshape(b, s, SSM_HEADS, P)


def mamba2_branch(z, xbc, dt_raw, conv_w, conv_b, dt_bias, a_log, d_skip, norm_w):
    b, s, _ = z.shape
    xbc = jax.nn.silu(causal_depthwise_conv(xbc, conv_w, conv_b))
    xs, bm, cm = jnp.split(xbc, [SSM_D_INNER, SSM_D_INNER + SSM_GROUPS * SSM_STATE], axis=-1)
    xs = xs.reshape(b, s, SSM_HEADS, SSM_HEADDIM).astype(jnp.float32)
    bm = bm.reshape(b, s, SSM_GROUPS, SSM_STATE).astype(jnp.float32)
    cm = cm.reshape(b, s, SSM_GROUPS, SSM_STATE).astype(jnp.float32)
    dt = jax.nn.softplus(dt_raw.astype(jnp.float32) + dt_bias.astype(jnp.float32))
    a = -jnp.exp(a_log.astype(jnp.float32))
    y = ssd_chunked(xs, dt, a, bm, cm) + xs * d_skip.astype(jnp.float32)[:, None]
    y = y.reshape(b, s, SSM_D_INNER) * jax.nn.silu(z.astype(jnp.float32))
    yg = y.reshape(b, s, SSM_GROUPS, SSM_D_INNER // SSM_GROUPS)
    yg = yg * lax.rsqrt(jnp.mean(yg * yg, axis=-1, keepdims=True) + RMS_EPS)
    return (yg.reshape(b, s, SSM_D_INNER) * norm_w.astype(jnp.float32)).astype(z.dtype)


def rel_bucket(rel):
    n = jnp.maximum(rel, 0)
    max_exact = REL_BUCKETS // 2
    nf = jnp.maximum(n, 1).astype(jnp.float32)
    large = max_exact + (jnp.log(nf / max_exact) / math.log(REL_MAX_DIST / max_exact)
                         * (REL_BUCKETS - max_exact)).astype(jnp.int32)
    large = jnp.minimum(large, REL_BUCKETS - 1)
    return jnp.where(n < max_exact, n, large)


def swa_sink_attention(q, k, v, sinks, rel_bias):
    b, s, _ = q.shape
    W = WINDOW
    nb = s // W
    KV, G, Dh = ATTN_KV_HEADS, ATTN_GROUP, ATTN_HEADDIM
    qb = q.reshape(b, nb, W, KV, G, Dh).astype(jnp.float32)
    kb = k.reshape(b, nb, W, KV, Dh).astype(jnp.float32)
    vb = v.reshape(b, nb, W, KV, Dh).astype(jnp.float32)
    pad = jnp.zeros_like(kb[:, :1])
    k_band = jnp.concatenate([jnp.concatenate([pad, kb[:, :-1]], axis=1), kb], axis=2)
    v_band = jnp.concatenate([jnp.concatenate([pad, vb[:, :-1]], axis=1), vb], axis=2)
    logits = jnp.einsum('bnqkgd,bnskd->bnkgqs', qb, k_band) * (Dh ** -0.5)
    qi = jnp.arange(W)[:, None] + W
    kj = jnp.arange(2 * W)[None, :]
    rel = qi - kj
    bias = rel_bias.astype(jnp.float32)[rel_bucket(rel)]
    bias = jnp.transpose(bias, (2, 0, 1)).reshape(KV, G, W, 2 * W)
    in_window = (rel >= 0) & (rel < W)
    block_idx = jnp.arange(nb)[:, None, None]
    valid = in_window[None] & ((block_idx > 0) | (kj >= W)[None])
    logits = jnp.where(valid[None, :, None, None], logits + bias, -jnp.inf)
    sink = sinks.astype(jnp.float32).reshape(1, 1, KV, G, 1, 1)
    m = jnp.maximum(jnp.max(logits, axis=-1, keepdims=True), sink)
    p = jnp.exp(logits - m)
    probs = p / (jnp.sum(p, axis=-1, keepdims=True) + jnp.exp(sink - m))
    out = jnp.einsum('bnkgqs,bnskd->bnqkgd', probs, v_band)
    return out.reshape(b, s, ATTN_HEADS * Dh).astype(q.dtype)


def conv_ffn(h, w_up, conv_w, conv_b, w_down):
    u = causal_depthwise_conv(jnp.einsum('bsd,df->bsf', h, w_up), conv_w, conv_b)
    gate, val = jnp.split(u, 2, axis=-1)
    return jnp.einsum('bsf,fd->bsd', jax.nn.silu(gate) * val, w_down)


def setup_inputs(seed: int = 0) -> dict:
    key = jax.random.key(seed)
    ks = jax.random.split(key, 24)
    L = DEPTH
    f32 = jnp.float32
    nrm = lambda k, shape: jax.random.normal(k, shape, f32)
    x = nrm(ks[0], (BATCH, SEQ, D_MODEL))
    rel_bias = 0.1 * nrm(ks[1], (REL_BUCKETS, ATTN_HEADS))
    w_in = nrm(ks[2], (L, D_MODEL, IN_COLS)) * D_MODEL ** -0.5
    b_gate = 0.01 * nrm(ks[3], (L, GATE_COLS))
    ssm_conv_w = 0.5 * nrm(ks[4], (L, SSM_CONV, SSM_CONV_DIM))
    ssm_conv_b = 0.01 * nrm(ks[5], (L, SSM_CONV_DIM))
    u = jax.random.uniform(ks[6], (L, SSM_HEADS), f32)
    dt0 = jnp.exp(u * (math.log(SSM_DT_MAX) - math.log(SSM_DT_MIN)) + math.log(SSM_DT_MIN))
    ssm_dt_bias = dt0 + jnp.log(-jnp.expm1(-dt0))
    ssm_a_log = jnp.log(jax.random.uniform(ks[7], (L, SSM_HEADS), f32, 1.0, 16.0))
    ssm_d = 1.0 + 0.01 * nrm(ks[8], (L, SSM_HEADS))
    ssm_norm_w = 1.0 + 0.01 * nrm(ks[9], (L, SSM_D_INNER))
    attn_sinks = 0.1 * nrm(ks[10], (L, ATTN_HEADS))
    w_branch_ssm = nrm(ks[11], (L, SSM_D_INNER, D_MODEL)) * SSM_D_INNER ** -0.5 * DEEPNORM_BETA
    w_branch_attn = nrm(ks[12], (L, Q_COLS, D_MODEL)) * Q_COLS ** -0.5 * DEEPNORM_BETA
    w_mix_out = nrm(ks[13], (L, D_MODEL, D_MODEL)) * D_MODEL ** -0.5 * DEEPNORM_BETA
    ln1_g = 1.0 + 0.01 * nrm(ks[14], (L, D_MODEL))
    ln1_b = 0.01 * nrm(ks[15], (L, D_MODEL))
    w_up = nrm(ks[16], (L, D_MODEL, 2 * D_FF)) * D_MODEL ** -0.5 * DEEPNORM_BETA
    ffn_conv_w = nrm(ks[17], (L, FFN_CONV, 2 * D_FF)) * FFN_CONV ** -0.5
    ffn_conv_b = 0.01 * nrm(ks[18], (L, 2 * D_FF))
    w_down = nrm(ks[19], (L, D_FF, D_MODEL)) * D_FF ** -0.5 * DEEPNORM_BETA
    ln2_g = 1.0 + 0.01 * nrm(ks[20], (L, D_MODEL))
    ln2_b = 0.01 * nrm(ks[21], (L, D_MODEL))
    return {'x': x, 'rel_bias': rel_bias, 'w_in': w_in, 'b_gate': b_gate,
            'ssm_conv_w': ssm_conv_w, 'ssm_conv_b': ssm_conv_b, 'ssm_dt_bias': ssm_dt_bias,
            'ssm_a_log': ssm_a_log, 'ssm_d': ssm_d, 'ssm_norm_w': ssm_norm_w,
            'attn_sinks': attn_sinks, 'w_branch_ssm': w_branch_ssm, 'w_branch_attn': w_branch_attn,
            'w_mix_out': w_mix_out, 'ln1_g': ln1_g, 'ln1_b': ln1_b, 'w_up': w_up,
            'ffn_conv_w': ffn_conv_w, 'ffn_conv_b': ffn_conv_b, 'w_down': w_down,
            'ln2_g': ln2_g, 'ln2_b': ln2_b}


def reference(x, rel_bias, w_in, b_gate, ssm_conv_w, ssm_conv_b, ssm_dt_bias, ssm_a_log, ssm_d,
              ssm_norm_w, attn_sinks, w_branch_ssm, w_branch_attn, w_mix_out, ln1_g, ln1_b,
              w_up, ffn_conv_w, ffn_conv_b, w_down, ln2_g, ln2_b):
    h = x
    for l in range(DEPTH):
        proj = jnp.einsum('bsd,dc->bsc', h, w_in[l])
        z, xbc, dt_raw, q, k, v, gates = jnp.split(proj, SPLIT_POINTS, axis=-1)
        y_ssm = mamba2_branch(z, xbc, dt_raw, ssm_conv_w[l], ssm_conv_b[l], ssm_dt_bias[l],
                              ssm_a_log[l], ssm_d[l], ssm_norm_w[l])
        y_attn = swa_sink_attention(q, k, v, attn_sinks[l], rel_bias)
        g_ssm, g_attn = jnp.split(jax.nn.sigmoid(gates + b_gate[l]), 2, axis=-1)
        merged = (g_ssm * jnp.einsum('bsi,id->bsd', y_ssm, w_branch_ssm[l])
                  + g_attn * jnp.einsum('bsi,id->bsd', y_attn, w_branch_attn[l]))
        mix_out = jnp.einsum('bsd,de->bse', merged, w_mix_out[l])
        h = layer_norm(DEEPNORM_ALPHA * h + mix_out, ln1_g[l], ln1_b[l])
        ffn_out = conv_ffn(h, w_up[l], ffn_conv_w[l], ffn_conv_b[l], w_down[l])
        h = layer_norm(DEEPNORM_ALPHA * h + ffn_out, ln2_g[l], ln2_b[l])
    return h
```

```python
import functools
import math

import numpy as np
import jax
import jax.numpy as jnp
from jax import lax
from jax.experimental import pallas as pl
from jax.experimental.pallas import tpu as pltpu

F32 = jnp.float32
BF16 = jnp.bfloat16

D_MODEL = 1024
SSM_D_INNER = 2048
SSM_HEADDIM = 64
SSM_HEADS = 32
SSM_GROUPS = 4
SSM_HEADS_PER_GROUP = 8
SSM_STATE = 128
SSM_CONV = 4
SSM_CHUNK = 128
SSM_CONV_DIM = SSM_D_INNER + 2 * SSM_GROUPS * SSM_STATE
GROUP_WIDTH = SSM_HEADS_PER_GROUP * SSM_HEADDIM
ATTN_HEADS = 16
ATTN_KV_HEADS = 2
ATTN_HEADDIM = 64
WINDOW = 128
REL_BUCKETS = 32
REL_MAX_DIST = 128
Q_COLS = ATTN_HEADS * ATTN_HEADDIM
KV_COLS = ATTN_KV_HEADS * ATTN_HEADDIM
QKV_COLS = Q_COLS + 2 * KV_COLS
GATE_COLS = 2 * D_MODEL
D_FF = 2816
FFN_CONV = 3
DEPTH = 1
DEEPNORM_ALPHA = (2.0 * DEPTH) ** 0.25
LN_EPS = 1e-5
RMS_EPS = 1e-5

LANES = 128
SUBLANES = 8
DT_PAD = LANES
HALO_ROWS = SUBLANES
VMEM_LIMIT = 56 * 1024 * 1024

PROJ_SEGMENTS = (SSM_D_INNER, SSM_CONV_DIM, QKV_COLS, GATE_COLS, DT_PAD)
PROJ_COLS = sum(PROJ_SEGMENTS)

NEG_BIG = -1e30


def _tiles():
    return dict(inproj=512, ssd=SSM_CHUNK, attn=WINDOW, merge=256, ffn=512)


def _dot(a, b):
    return jnp.dot(a, b, preferred_element_type=F32)


def _dot_nt(a, b):
    return lax.dot_general(a, b, (((1,), (1,)), ((), ())), preferred_element_type=F32)


def _sigmoid(x):
    return 1.0 / (1.0 + jnp.exp(-x))


def _silu(x):
    return x * _sigmoid(x)


def _layer_norm(r, g, b):
    mu = jnp.mean(r, axis=-1, keepdims=True)
    rc = r - mu
    var = jnp.mean(rc * rc, axis=-1, keepdims=True)
    return rc * lax.rsqrt(var + LN_EPS) * g + b


def _shift_rows(x, prev, j):
    row = lax.broadcasted_iota(jnp.int32, prev.shape, 0)
    sh = pltpu.roll(x, j, 0)
    top = jnp.where(row < j, pltpu.roll(prev, j, 0), sh[:HALO_ROWS])
    return jnp.concatenate([top, sh[HALO_ROWS:]], axis=0)


def _causal_conv(x, prev, w_ref, b_ref, cols, taps):
    acc = b_ref[:, cols] + w_ref[taps - 1:taps, cols] * x
    for j in range(1, taps):
        acc = acc + w_ref[taps - 1 - j:taps - j, cols] * _shift_rows(x, prev, j)
    return acc


def _inproj_kernel(x_ref, w_ref, z_ref, xbc_ref, qkv_ref, g_ref, dt_ref):
    xb = x_ref[...].astype(BF16)
    chunk = 512
    col = 0
    for ref, width in zip((z_ref, xbc_ref, qkv_ref, g_ref, dt_ref), PROJ_SEGMENTS):
        for c in range(0, width, chunk):
            w = min(chunk, width - c)
            ref[:, c:c + w] = _dot(xb, w_ref[:, col + c:col + c + w]).astype(ref.dtype)
        col += width


def _inproj(x2d, w_perm, tm):
    s = x2d.shape[0]
    row = lambda i: (i, 0)
    const = lambda i: (0, 0)
    outs = [jax.ShapeDtypeStruct((s, n), dt) for n, dt in
            zip(PROJ_SEGMENTS, (BF16, BF16, BF16, BF16, F32))]
    return pl.pallas_call(
        _inproj_kernel,
        grid=(s // tm,),
        in_specs=[pl.BlockSpec((tm, D_MODEL), row),
                  pl.BlockSpec((D_MODEL, PROJ_COLS), const, pipeline_mode=pl.Buffered(1))],
        out_specs=[pl.BlockSpec((tm, n), row) for n in PROJ_SEGMENTS],
        out_shape=outs,
        compiler_params=pltpu.CompilerParams(dimension_semantics=("arbitrary",),
                                             vmem_limit_bytes=VMEM_LIMIT),
        name="inproj",
    )(x2d, w_perm)


def _split3(x):
    hi = x.astype(BF16).astype(F32)
    r = x - hi
    mid = r.astype(BF16).astype(F32)
    lo = (r - mid).astype(BF16).astype(F32)
    return hi, mid, lo


def _pack3(x):
    hi, mid, lo = _split3(x)
    return (hi + pltpu.roll(mid, SSM_HEADS, 1) + pltpu.roll(lo, 2 * SSM_HEADS, 1)).astype(BF16)


def _ssd_kernel(xbc_ref, z_ref, dt_ref, cw_ref, cb_ref, dtb_ref, alog_ref, dexp_ref, nw_ref,
                e3_ref, ec3_ref, y_ref, halo_ref, state_ref, conv_ref):
    L = SSM_CHUNK

    @pl.when(pl.program_id(0) == 0)
    def _():
        halo_ref[...] = jnp.zeros_like(halo_ref)
        state_ref[...] = jnp.zeros_like(state_ref)

    for c in range(0, SSM_CONV_DIM, GROUP_WIDTH):
        cols = slice(c, c + GROUP_WIDTH)
        x_in = xbc_ref[:, cols].astype(F32)
        prev = halo_ref[:, cols]
        halo_ref[:, cols] = x_in[L - HALO_ROWS:, :]
        conv_ref[:, cols] = _silu(_causal_conv(x_in, prev, cw_ref, cb_ref, cols, SSM_CONV))

    lane_ok = lax.broadcasted_iota(jnp.int32, (L, DT_PAD), 1) < SSM_HEADS
    dtr = dt_ref[...] + dtb_ref[...]
    softplus = jnp.maximum(dtr, 0.0) + jnp.log(1.0 + jnp.exp(-jnp.abs(dtr)))
    dt = jnp.where(lane_ok, softplus, 0.0)
    a_dt = dt * (-jnp.exp(alog_ref[...]))

    ri = lax.broadcasted_iota(jnp.int32, (L, L), 0)
    ci = lax.broadcasted_iota(jnp.int32, (L, L), 1)
    causal = ri >= ci
    tril = jnp.where(causal, 1.0, 0.0).astype(BF16)
    hi, mid, lo = _split3(a_dt)
    a_cs = _dot(tril, hi.astype(BF16)) + _dot(tril, mid.astype(BF16)) + _dot(tril, lo.astype(BF16))

    a_last = a_cs[L - 1:L, :]
    decay_out = jnp.where(lane_ok, jnp.exp(a_cs), 0.0)
    w_state = dt * jnp.exp(a_last - a_cs)

    e3 = e3_ref[...]
    dt_e = _dot(_pack3(dt), e3)
    ws_e = _dot(_pack3(w_state), e3)
    do_e = _dot(_pack3(decay_out), e3)
    cd_e = do_e[L - 1:L, :]

    a_cs_t = a_cs.T
    acs3 = _pack3(a_cs)
    lane_lo = lax.broadcasted_iota(jnp.int32, (L, LANES), 1) < SSM_HEADDIM

    for g in range(SSM_GROUPS):
        gsl = slice(g * GROUP_WIDTH, (g + 1) * GROUP_WIDTH)
        xs_g = conv_ref[:, gsl]
        b_off = SSM_D_INNER + g * SSM_STATE
        c_off = SSM_D_INNER + SSM_GROUPS * SSM_STATE + g * SSM_STATE
        b_g = conv_ref[:, b_off:b_off + SSM_STATE]
        c_g = conv_ref[:, c_off:c_off + SSM_STATE].astype(BF16)
        cb = _dot_nt(c_g, b_g.astype(BF16))
        xdt_b = (xs_g * dt_e[:, gsl]).astype(BF16)
        xdtd_b = (xs_g * ws_e[:, gsl]).astype(BF16)

        st = state_ref[g]
        y_off = _dot(c_g, st.astype(BF16)) * do_e[:, gsl]
        state_ref[g] = st * cd_e[:, gsl] + _dot(b_g.T.astype(BF16), xdtd_b)

        pieces = []
        for pr in range(SSM_HEADS_PER_GROUP // 2):
            h1 = g * SSM_HEADS_PER_GROUP + 2 * pr
            acol = _dot(acs3, ec3_ref[:, h1 * L:(h1 + 2) * L])
            ms = []
            for k in range(2):
                seg = acol[:, k * L:(k + 1) * L] - a_cs_t[h1 + k:h1 + k + 1, :]
                ms.append((cb * jnp.exp(jnp.where(causal, seg, NEG_BIG))).astype(BF16))
            xp = xdt_b[:, pr * LANES:(pr + 1) * LANES]
            zero = jnp.zeros_like(xp)
            rhs = jnp.concatenate([jnp.where(lane_lo, xp, zero), jnp.where(lane_lo, zero, xp)], axis=0)
            pieces.append(_dot(jnp.concatenate(ms, axis=1), rhs))
        y_g = jnp.concatenate(pieces, axis=1) + y_off + xs_g * dexp_ref[:, gsl]

        zg = z_ref[:, gsl].astype(F32)
        yz = y_g * _silu(zg)
        ms_ = jnp.mean(yz * yz, axis=-1, keepdims=True)
        y_ref[:, gsl] = (yz * lax.rsqrt(ms_ + RMS_EPS) * nw_ref[:, gsl]).astype(BF16)


def _expansion_matrices():
    k = np.arange(LANES)[:, None]
    valid = k < 3 * SSM_HEADS
    head = k % SSM_HEADS
    e3 = (valid & (head == (np.arange(SSM_D_INNER)[None, :] // SSM_HEADDIM))).astype(np.float32)
    ec3 = (valid & (head == (np.arange(SSM_HEADS * SSM_CHUNK)[None, :] // SSM_CHUNK))).astype(np.float32)
    return jnp.asarray(e3, BF16), jnp.asarray(ec3, BF16)


def _ssd(xbc, z, dt, conv_w, conv_b, dt_bias, a_log, d_exp, norm_w, tm):
    s = xbc.shape[0]
    e3, ec3 = _expansion_matrices()
    row = lambda i: (i, 0)
    const = lambda i: (0, 0)
    full = lambda a: pl.BlockSpec(a.shape, const)
    return pl.pallas_call(
        _ssd_kernel,
        grid=(s // tm,),
        in_specs=[pl.BlockSpec((tm, SSM_CONV_DIM), row),
                  pl.BlockSpec((tm, SSM_D_INNER), row),
                  pl.BlockSpec((tm, DT_PAD), row),
                  full(conv_w), full(conv_b), full(dt_bias), full(a_log), full(d_exp), full(norm_w),
                  full(e3), full(ec3)],
        out_specs=pl.BlockSpec((tm, SSM_D_INNER), row),
        out_shape=jax.ShapeDtypeStruct((s, SSM_D_INNER), BF16),
        scratch_shapes=[pltpu.VMEM((HALO_ROWS, SSM_CONV_DIM), F32),
                        pltpu.VMEM((SSM_GROUPS, SSM_STATE, GROUP_WIDTH), F32),
                        pltpu.VMEM((tm, SSM_CONV_DIM), F32)],
        compiler_params=pltpu.CompilerParams(dimension_semantics=("arbitrary",),
                                             vmem_limit_bytes=VMEM_LIMIT),
        name="ssd",
    )(xbc, z, dt, conv_w, conv_b, dt_bias, a_log, d_exp, norm_w, e3, ec3)


def _attn_kernel(sink_ref, q_ref, kvc_ref, kvp_ref, bm_ref, o_ref):
    W = WINDOW
    lane_lo2 = lax.broadcasted_iota(jnp.int32, (2 * W, LANES), 1) < ATTN_HEADDIM
    lane_lo = lax.broadcasted_iota(jnp.int32, (W, LANES), 1) < ATTN_HEADDIM

    def band(lo_col):
        return jnp.concatenate([kvp_ref[:, lo_col:lo_col + KV_COLS].astype(F32),
                                kvc_ref[:, lo_col:lo_col + KV_COLS].astype(F32)], axis=0)

    kband = band(0) * (ATTN_HEADDIM ** -0.5)
    vband = band(KV_COLS)
    ksw = pltpu.roll(kband, ATTN_HEADDIM, 1)
    vsw = pltpu.roll(vband, ATTN_HEADDIM, 1)
    k_var = ((jnp.where(lane_lo2, kband, 0.0).astype(BF16), jnp.where(lane_lo2, 0.0, ksw).astype(BF16)),
             (jnp.where(lane_lo2, ksw, 0.0).astype(BF16), jnp.where(lane_lo2, 0.0, kband).astype(BF16)))
    v_bd = (jnp.concatenate([jnp.where(lane_lo2, vband, 0.0), jnp.where(lane_lo2, 0.0, vsw)], axis=0).astype(BF16),
            jnp.concatenate([jnp.where(lane_lo2, vsw, 0.0), jnp.where(lane_lo2, 0.0, vband)], axis=0).astype(BF16))

    for i in range(ATTN_HEADS // 2):
        c = (2 * i) // (ATTN_HEADS // ATTN_KV_HEADS)
        qp = q_ref[:, i * LANES:(i + 1) * LANES]
        ps, rs = [], []
        for j in range(2):
            h = 2 * i + j
            sink = sink_ref[h]
            s = _dot_nt(qp, k_var[c][j]) + bm_ref[0, h]
            m = jnp.maximum(jnp.max(s, axis=-1, keepdims=True), sink)
            p = jnp.exp(s - m)
            den = jnp.sum(p, axis=-1, keepdims=True) + jnp.exp(sink - m)
            ps.append(p.astype(BF16))
            rs.append(1.0 / den)
        pv = _dot(jnp.concatenate(ps, axis=1), v_bd[c])
        o_ref[:, i * LANES:(i + 1) * LANES] = (pv * jnp.where(lane_lo, rs[0], rs[1])).astype(BF16)


def _attn(qkv, bias_mask, sinks, tm):
    s = qkv.shape[0]
    kv_blk = Q_COLS // (2 * KV_COLS)
    return pl.pallas_call(
        _attn_kernel,
        grid=(s // tm,),
        in_specs=[pl.BlockSpec(memory_space=pltpu.SMEM),
                  pl.BlockSpec((tm, Q_COLS), lambda i: (i, 0)),
                  pl.BlockSpec((tm, 2 * KV_COLS), lambda i: (i, kv_blk)),
                  pl.BlockSpec((tm, 2 * KV_COLS), lambda i: (jnp.maximum(i - 1, 0), kv_blk)),
                  pl.BlockSpec((1, ATTN_HEADS, WINDOW, 2 * WINDOW),
                               lambda i: (jnp.minimum(i, 1), 0, 0, 0))],
        out_specs=pl.BlockSpec((tm, Q_COLS), lambda i: (i, 0)),
        out_shape=jax.ShapeDtypeStruct((s, Q_COLS), BF16),
        compiler_params=pltpu.CompilerParams(dimension_semantics=("arbitrary",),
                                             vmem_limit_bytes=VMEM_LIMIT),
        name="attn",
    )(sinks, qkv, qkv, qkv, bias_mask)


def _rel_bucket(rel):
    n = jnp.maximum(rel, 0)
    max_exact = REL_BUCKETS // 2
    nf = jnp.maximum(n, 1).astype(F32)
    large = max_exact + (jnp.log(nf / max_exact) / math.log(REL_MAX_DIST / max_exact)
                         * (REL_BUCKETS - max_exact)).astype(jnp.int32)
    large = jnp.minimum(large, REL_BUCKETS - 1)
    return jnp.where(n < max_exact, n, large)


def _bias_mask(rel_bias):
    W = WINDOW
    rel = (jnp.arange(W)[:, None] + W) - jnp.arange(2 * W)[None, :]
    bias = jnp.transpose(rel_bias.astype(F32)[_rel_bucket(rel)], (2, 0, 1))
    in_window = (rel >= 0) & (rel < W)
    first = in_window & (jnp.arange(2 * W)[None, :] >= W)
    return jnp.stack([jnp.where(first[None], bias, -jnp.inf), jnp.where(in_window[None], bias, -jnp.inf)])


def _merge_kernel(ys_ref, ya_ref, g_ref, x_ref, wbs_ref, wba_ref, wmix_ref, bg_ref, lg_ref, lb_ref, h_ref):
    a = _dot(ys_ref[...], wbs_ref[...])
    b = _dot(ya_ref[...], wba_ref[...])
    g = g_ref[...].astype(F32) + bg_ref[...]
    merged = _sigmoid(g[:, :D_MODEL]) * a + _sigmoid(g[:, D_MODEL:]) * b
    mix = _dot(merged.astype(BF16), wmix_ref[...])
    h_ref[...] = _layer_norm(DEEPNORM_ALPHA * x_ref[...] + mix, lg_ref[...], lb_ref[...])


def _merge(y_ssm, y_attn, gates, x2d, w_bs, w_ba, w_mix, b_gate, ln_g, ln_b, tm):
    s = x2d.shape[0]
    row = lambda i: (i, 0)
    const = lambda i: (0, 0)
    full = lambda a: pl.BlockSpec(a.shape, const)
    return pl.pallas_call(
        _merge_kernel,
        grid=(s // tm,),
        in_specs=[pl.BlockSpec((tm, SSM_D_INNER), row), pl.BlockSpec((tm, Q_COLS), row),
                  pl.BlockSpec((tm, GATE_COLS), row), pl.BlockSpec((tm, D_MODEL), row),
                  full(w_bs), full(w_ba), full(w_mix), full(b_gate), full(ln_g), full(ln_b)],
        out_specs=pl.BlockSpec((tm, D_MODEL), row),
        out_shape=jax.ShapeDtypeStruct((s, D_MODEL), F32),
        compiler_params=pltpu.CompilerParams(dimension_semantics=("arbitrary",),
                                             vmem_limit_bytes=VMEM_LIMIT),
        name="merge",
    )(y_ssm, y_attn, gates, x2d, w_bs, w_ba, w_mix, b_gate, ln_g, ln_b)


def _ffn_kernel(h_ref, wup_ref, cw_ref, cb_ref, wdn_ref, lg_ref, lb_ref, o_ref, halo_ref, act_ref):
    tm = h_ref.shape[0]

    @pl.when(pl.program_id(0) == 0)
    def _():
        halo_ref[...] = jnp.zeros_like(halo_ref)

    hb = h_ref[...].astype(BF16)
    chunk = 256

    def conv_cols(cols):
        u = _dot(hb, wup_ref[:, cols])
        prev = halo_ref[:, cols]
        halo_ref[:, cols] = u[tm - HALO_ROWS:, :]
        return _causal_conv(u, prev, cw_ref, cb_ref, cols, FFN_CONV)

    for c in range(0, D_FF, chunk):
        gate = conv_cols(slice(c, c + chunk))
        val = conv_cols(slice(D_FF + c, D_FF + c + chunk))
        act_ref[:, c:c + chunk] = (_silu(gate) * val).astype(BF16)

    out = _dot(act_ref[...], wdn_ref[...])
    o_ref[...] = _layer_norm(DEEPNORM_ALPHA * h_ref[...] + out, lg_ref[...], lb_ref[...])


def _ffn(h1, w_up, conv_w, conv_b, w_down, ln_g, ln_b, tm):
    s = h1.shape[0]
    row = lambda i: (i, 0)
    const = lambda i: (0, 0)
    full = lambda a: pl.BlockSpec(a.shape, const)
    return pl.pallas_call(
        _ffn_kernel,
        grid=(s // tm,),
        in_specs=[pl.BlockSpec((tm, D_MODEL), row),
                  pl.BlockSpec(w_up.shape, const, pipeline_mode=pl.Buffered(1)),
                  full(conv_w), full(conv_b),
                  pl.BlockSpec(w_down.shape, const, pipeline_mode=pl.Buffered(1)),
                  full(ln_g), full(ln_b)],
        out_specs=pl.BlockSpec((tm, D_MODEL), row),
        out_shape=jax.ShapeDtypeStruct((s, D_MODEL), F32),
        scratch_shapes=[pltpu.VMEM((HALO_ROWS, 2 * D_FF), F32),
                        pltpu.VMEM((tm, D_FF), BF16)],
        compiler_params=pltpu.CompilerParams(dimension_semantics=("arbitrary",),
                                             vmem_limit_bytes=VMEM_LIMIT),
        name="ffn",
    )(h1, w_up, conv_w, conv_b, w_down, ln_g, ln_b)


def _permute_w_in(w):
    dt_lo = SSM_D_INNER + SSM_CONV_DIM
    dt_hi = dt_lo + SSM_HEADS
    pad = jnp.zeros((w.shape[0], DT_PAD - SSM_HEADS), w.dtype)
    return jnp.concatenate([w[:, :dt_lo], w[:, dt_hi:], w[:, dt_lo:dt_hi], pad], axis=1).astype(BF16)


def _row(v, width=None):
    v = v.astype(F32).reshape(1, -1)
    if width is not None and v.shape[1] < width:
        v = jnp.pad(v, ((0, 0), (0, width - v.shape[1])))
    return v


def kernel(x, rel_bias, w_in, b_gate, ssm_conv_w, ssm_conv_b, ssm_dt_bias, ssm_a_log, ssm_d, ssm_norm_w,
           attn_sinks, w_branch_ssm, w_branch_attn, w_mix_out, ln1_g, ln1_b, w_up, ffn_conv_w, ffn_conv_b,
           w_down, ln2_g, ln2_b):
    b, s, d = x.shape
    assert (b, d) == (1, D_MODEL) and s % 512 == 0 and w_in.shape[0] == DEPTH
    t = _tiles()
    h = x.reshape(s, d)
    bias_mask = _bias_mask(rel_bias)
    for l in range(DEPTH):
        z, xbc, qkv, gates, dt = _inproj(h, _permute_w_in(w_in[l]), t["inproj"])
        y_ssm = _ssd(xbc, z, dt, ssm_conv_w[l].astype(F32), _row(ssm_conv_b[l]),
                     _row(ssm_dt_bias[l], DT_PAD), _row(ssm_a_log[l], DT_PAD),
                     _row(jnp.repeat(ssm_d[l], SSM_HEADDIM)), _row(ssm_norm_w[l]), t["ssd"])
        y_attn = _attn(qkv, bias_mask, attn_sinks[l].astype(F32), t["attn"])
        h1 = _merge(y_ssm, y_attn, gates, h, w_branch_ssm[l].astype(BF16), w_branch_attn[l].astype(BF16),
                    w_mix_out[l].astype(BF16), _row(b_gate[l]), _row(ln1_g[l]), _row(ln1_b[l]), t["merge"])
        h = _ffn(h1, w_up[l].astype(BF16), ffn_conv_w[l].astype(F32), _row(ffn_conv_b[l]),
                 w_down[l].astype(BF16), _row(ln2_g[l]), _row(ln2_b[l]), t["ffn"])
    return h.reshape(b, s, d)
```

```python
import functools
import math

import numpy as np
import jax
import jax.numpy as jnp
from jax import lax
from jax.experimental import pallas as pl
from jax.experimental.pallas import tpu as pltpu

F32 = jnp.float32
BF16 = jnp.bfloat16

D_MODEL = 1024
SSM_D_INNER = 2048
SSM_HEADDIM = 64
SSM_HEADS = 32
SSM_GROUPS = 4
SSM_HEADS_PER_GROUP = 8
SSM_STATE = 128
SSM_CONV = 4
SSM_CHUNK = 128
SSM_CONV_DIM = SSM_D_INNER + 2 * SSM_GROUPS * SSM_STATE
GROUP_WIDTH = SSM_HEADS_PER_GROUP * SSM_HEADDIM
ATTN_HEADS = 16
ATTN_KV_HEADS = 2
ATTN_HEADDIM = 64
WINDOW = 128
REL_BUCKETS = 32
REL_MAX_DIST = 128
Q_COLS = ATTN_HEADS * ATTN_HEADDIM
KV_COLS = ATTN_KV_HEADS * ATTN_HEADDIM
QKV_COLS = Q_COLS + 2 * KV_COLS
GATE_COLS = 2 * D_MODEL
D_FF = 2816
FFN_CONV = 3
DEPTH = 1
DEEPNORM_ALPHA = (2.0 * DEPTH) ** 0.25
LN_EPS = 1e-5
RMS_EPS = 1e-5

LANES = 128
SUBLANES = 8
DT_PAD = LANES
HALO_ROWS = SUBLANES
VMEM_LIMIT = 56 * 1024 * 1024

PROJ_SEGMENTS = (SSM_D_INNER, SSM_CONV_DIM, QKV_COLS, GATE_COLS, DT_PAD)
PROJ_COLS = sum(PROJ_SEGMENTS)

NEG_BIG = -1e30


def _tiles():
    return dict(inproj=256, ssd=SSM_CHUNK, attn=WINDOW, merge=256, ffn=512)


def _dot(a, b):
    return jnp.dot(a, b, preferred_element_type=F32)


def _dot_nt(a, b):
    return lax.dot_general(a, b, (((1,), (1,)), ((), ())), preferred_element_type=F32)


def _sigmoid(x):
    return 1.0 / (1.0 + jnp.exp(-x))


def _silu(x):
    return x * _sigmoid(x)


def _layer_norm(r, g, b):
    mu = jnp.mean(r, axis=-1, keepdims=True)
    rc = r - mu
    var = jnp.mean(rc * rc, axis=-1, keepdims=True)
    return rc * lax.rsqrt(var + LN_EPS) * g + b


def _shift_rows(x, prev, j):
    row = lax.broadcasted_iota(jnp.int32, prev.shape, 0)
    sh = pltpu.roll(x, j, 0)
    top = jnp.where(row < j, pltpu.roll(prev, j, 0), sh[:HALO_ROWS])
    return jnp.concatenate([top, sh[HALO_ROWS:]], axis=0)


def _causal_conv(x, prev, w_ref, b_ref, cols, taps):
    acc = b_ref[:, cols] + w_ref[taps - 1:taps, cols] * x
    for j in range(1, taps):
        acc = acc + w_ref[taps - 1 - j:taps - j, cols] * _shift_rows(x, prev, j)
    return acc


def _inproj_kernel(x_ref, w_ref, cw_ref, cb_ref, dtb_ref, bg_ref,
                   zs_ref, xc_ref, qkv_ref, g_ref, dt_ref, halo_ref):
    tm = x_ref.shape[0]

    @pl.when(pl.program_id(0) == 0)
    def _():
        halo_ref[...] = jnp.zeros_like(halo_ref)

    xb = x_ref[...].astype(BF16)
    chunk = 256
    offs = [sum(PROJ_SEGMENTS[:i]) for i in range(len(PROJ_SEGMENTS))]

    def proj(seg, c, w):
        return _dot(xb, w_ref[:, offs[seg] + c:offs[seg] + c + w])

    for c in range(0, SSM_D_INNER, chunk):
        zs_ref[:, c:c + chunk] = _silu(proj(0, c, chunk)).astype(BF16)
    for c in range(0, SSM_CONV_DIM, chunk):
        cols = slice(c, c + chunk)
        u = proj(1, c, chunk)
        prev = halo_ref[:, cols]
        halo_ref[:, cols] = u[tm - HALO_ROWS:, :]
        xc_ref[:, cols] = _silu(_causal_conv(u, prev, cw_ref, cb_ref, cols, SSM_CONV)).astype(BF16)
    for c in range(0, QKV_COLS, chunk):
        w = min(chunk, QKV_COLS - c)
        qkv_ref[:, c:c + w] = proj(2, c, w).astype(BF16)
    for c in range(0, GATE_COLS, chunk):
        g_ref[:, c:c + chunk] = _sigmoid(proj(3, c, chunk) + bg_ref[:, c:c + chunk]).astype(BF16)
    dtr = proj(4, 0, DT_PAD) + dtb_ref[...]
    softplus = jnp.maximum(dtr, 0.0) + jnp.log(1.0 + jnp.exp(-jnp.abs(dtr)))
    lane_ok = lax.broadcasted_iota(jnp.int32, dtr.shape, 1) < SSM_HEADS
    dt_ref[...] = jnp.where(lane_ok, softplus, 0.0)


def _inproj(x2d, w_perm, conv_w, conv_b, dt_bias, b_gate, tm):
    s = x2d.shape[0]
    row = lambda i: (i, 0)
    const = lambda i: (0, 0)
    full = lambda a: pl.BlockSpec(a.shape, const)
    outs = [jax.ShapeDtypeStruct((s, n), dt) for n, dt in
            zip(PROJ_SEGMENTS, (BF16, BF16, BF16, BF16, F32))]
    return pl.pallas_call(
        _inproj_kernel,
        grid=(s // tm,),
        in_specs=[pl.BlockSpec((tm, D_MODEL), row),
                  pl.BlockSpec((D_MODEL, PROJ_COLS), const, pipeline_mode=pl.Buffered(1)),
                  full(conv_w), full(conv_b), full(dt_bias), full(b_gate)],
        out_specs=[pl.BlockSpec((tm, n), row) for n in PROJ_SEGMENTS],
        out_shape=outs,
        scratch_shapes=[pltpu.VMEM((HALO_ROWS, SSM_CONV_DIM), F32)],
        compiler_params=pltpu.CompilerParams(dimension_semantics=("arbitrary",),
                                             vmem_limit_bytes=VMEM_LIMIT),
        name="inproj",
    )(x2d, w_perm, conv_w, conv_b, dt_bias, b_gate)


def _split3(x):
    hi = x.astype(BF16).astype(F32)
    r = x - hi
    mid = r.astype(BF16).astype(F32)
    lo = (r - mid).astype(BF16).astype(F32)
    return hi, mid, lo


def _pack3(x):
    hi, mid, lo = _split3(x)
    return (hi + pltpu.roll(mid, SSM_HEADS, 1) + pltpu.roll(lo, 2 * SSM_HEADS, 1)).astype(BF16)


def _ssd_kernel(conv_ref, zs_ref, dt_ref, alog_ref, dexp_ref, nw_ref, e3_ref, ec3_ref, y_ref, state_ref):
    L = SSM_CHUNK

    @pl.when(pl.program_id(0) == 0)
    def _():
        state_ref[...] = jnp.zeros_like(state_ref)

    lane_ok = lax.broadcasted_iota(jnp.int32, (L, DT_PAD), 1) < SSM_HEADS
    dt = dt_ref[...]
    a_dt = dt * (-jnp.exp(alog_ref[...]))

    ri = lax.broadcasted_iota(jnp.int32, (L, L), 0)
    ci = lax.broadcasted_iota(jnp.int32, (L, L), 1)
    causal = ri >= ci
    tril = jnp.where(causal, 1.0, 0.0).astype(BF16)
    hi, mid, lo = _split3(a_dt)
    a_cs = _dot(tril, hi.astype(BF16)) + _dot(tril, mid.astype(BF16)) + _dot(tril, lo.astype(BF16))

    a_last = a_cs[L - 1:L, :]
    decay_out = jnp.where(lane_ok, jnp.exp(a_cs), 0.0)
    w_state = dt * jnp.exp(a_last - a_cs)

    e3 = e3_ref[...]
    dt_e = _dot(_pack3(dt), e3)
    ws_e = _dot(_pack3(w_state), e3)
    do_e = _dot(_pack3(decay_out), e3)
    cd_e = do_e[L - 1:L, :]

    a_cs_t = a_cs.T
    acs3 = _pack3(a_cs)
    lane_lo = lax.broadcasted_iota(jnp.int32, (L, LANES), 1) < SSM_HEADDIM

    for g in range(SSM_GROUPS):
        gsl = slice(g * GROUP_WIDTH, (g + 1) * GROUP_WIDTH)
        xs_g = conv_ref[:, gsl].astype(F32)
        b_off = SSM_D_INNER + g * SSM_STATE
        c_off = SSM_D_INNER + SSM_GROUPS * SSM_STATE + g * SSM_STATE
        b_g = conv_ref[:, b_off:b_off + SSM_STATE]
        c_g = conv_ref[:, c_off:c_off + SSM_STATE]
        cb = _dot_nt(c_g, b_g)
        xdt_b = (xs_g * dt_e[:, gsl]).astype(BF16)
        xdtd_b = (xs_g * ws_e[:, gsl]).astype(BF16)

        st = state_ref[g]
        y_off = _dot(c_g, st.astype(BF16)) * do_e[:, gsl]
        state_ref[g] = st * cd_e[:, gsl] + _dot(b_g.astype(F32).T.astype(BF16), xdtd_b)

        pieces = []
        for pr in range(SSM_HEADS_PER_GROUP // 2):
            h1 = g * SSM_HEADS_PER_GROUP + 2 * pr
            acol = _dot(acs3, ec3_ref[:, h1 * L:(h1 + 2) * L])
            ms = []
            for k in range(2):
                seg = acol[:, k * L:(k + 1) * L] - a_cs_t[h1 + k:h1 + k + 1, :]
                ms.append((cb * jnp.exp(jnp.where(causal, seg, NEG_BIG))).astype(BF16))
            xp = xdt_b[:, pr * LANES:(pr + 1) * LANES]
            zero = jnp.zeros_like(xp)
            rhs = jnp.concatenate([jnp.where(lane_lo, xp, zero), jnp.where(lane_lo, zero, xp)], axis=0)
            pieces.append(_dot(jnp.concatenate(ms, axis=1), rhs))
        y_g = jnp.concatenate(pieces, axis=1) + y_off + xs_g * dexp_ref[:, gsl]

        yz = y_g * zs_ref[:, gsl].astype(F32)
        ms_ = jnp.mean(yz * yz, axis=-1, keepdims=True)
        y_ref[:, gsl] = (yz * lax.rsqrt(ms_ + RMS_EPS) * nw_ref[:, gsl]).astype(BF16)


def _expansion_matrices():
    k = np.arange(LANES)[:, None]
    valid = k < 3 * SSM_HEADS
    head = k % SSM_HEADS
    e3 = (valid & (head == (np.arange(SSM_D_INNER)[None, :] // SSM_HEADDIM))).astype(np.float32)
    ec3 = (valid & (head == (np.arange(SSM_HEADS * SSM_CHUNK)[None, :] // SSM_CHUNK))).astype(np.float32)
    return jnp.asarray(e3, BF16), jnp.asarray(ec3, BF16)


def _ssd(xc, zs, dt, a_log, d_exp, norm_w, tm):
    s = xc.shape[0]
    e3, ec3 = _expansion_matrices()
    row = lambda i: (i, 0)
    const = lambda i: (0, 0)
    full = lambda a: pl.BlockSpec(a.shape, const)
    return pl.pallas_call(
        _ssd_kernel,
        grid=(s // tm,),
        in_specs=[pl.BlockSpec((tm, SSM_CONV_DIM), row),
                  pl.BlockSpec((tm, SSM_D_INNER), row),
                  pl.BlockSpec((tm, DT_PAD), row),
                  full(a_log), full(d_exp), full(norm_w), full(e3), full(ec3)],
        out_specs=pl.BlockSpec((tm, SSM_D_INNER), row),
        out_shape=jax.ShapeDtypeStruct((s, SSM_D_INNER), BF16),
        scratch_shapes=[pltpu.VMEM((SSM_GROUPS, SSM_STATE, GROUP_WIDTH), F32)],
        compiler_params=pltpu.CompilerParams(dimension_semantics=("arbitrary",),
                                             vmem_limit_bytes=VMEM_LIMIT),
        name="ssd",
    )(xc, zs, dt, a_log, d_exp, norm_w, e3, ec3)


def _attn_kernel(sink_ref, q_ref, kvc_ref, kvp_ref, ur_ref, o_ref, bm_ref):
    W = WINDOW
    lane_lo2 = lax.broadcasted_iota(jnp.int32, (2 * W, LANES), 1) < ATTN_HEADDIM
    lane_lo = lax.broadcasted_iota(jnp.int32, (W, LANES), 1) < ATTN_HEADDIM
    blk = pl.program_id(0)

    @pl.when(blk == 0)
    def _():
        first = lax.broadcasted_iota(jnp.int32, (W, 2 * W), 1) >= W
        for h in range(ATTN_HEADS):
            band_h = pltpu.roll(jnp.broadcast_to(ur_ref[h:h + 1, :], (W, 2 * W)), W, 1,
                                stride=1, stride_axis=0)
            bm_ref[1, h] = band_h
            bm_ref[0, h] = jnp.where(first, band_h, -jnp.inf)

    slab = jnp.minimum(blk, 1)

    def band(lo_col):
        return jnp.concatenate([kvp_ref[:, lo_col:lo_col + KV_COLS].astype(F32),
                                kvc_ref[:, lo_col:lo_col + KV_COLS].astype(F32)], axis=0)

    kband = band(0) * (ATTN_HEADDIM ** -0.5)
    vband = band(KV_COLS)
    ksw = pltpu.roll(kband, ATTN_HEADDIM, 1)
    vsw = pltpu.roll(vband, ATTN_HEADDIM, 1)
    k_var = ((jnp.where(lane_lo2, kband, 0.0).astype(BF16), jnp.where(lane_lo2, 0.0, ksw).astype(BF16)),
             (jnp.where(lane_lo2, ksw, 0.0).astype(BF16), jnp.where(lane_lo2, 0.0, kband).astype(BF16)))
    v_bd = (jnp.concatenate([jnp.where(lane_lo2, vband, 0.0), jnp.where(lane_lo2, 0.0, vsw)], axis=0).astype(BF16),
            jnp.concatenate([jnp.where(lane_lo2, vsw, 0.0), jnp.where(lane_lo2, 0.0, vband)], axis=0).astype(BF16))

    for i in range(ATTN_HEADS // 2):
        c = (2 * i) // (ATTN_HEADS // ATTN_KV_HEADS)
        qp = q_ref[:, i * LANES:(i + 1) * LANES]
        ps, rs = [], []
        for j in range(2):
            h = 2 * i + j
            sink = sink_ref[h]
            s = _dot_nt(qp, k_var[c][j]) + bm_ref[slab, h]
            m = jnp.maximum(jnp.max(s, axis=-1, keepdims=True), sink)
            p = jnp.exp(s - m)
            den = jnp.sum(p, axis=-1, keepdims=True) + jnp.exp(sink - m)
            ps.append(p.astype(BF16))
            rs.append(1.0 / den)
        pv = _dot(jnp.concatenate(ps, axis=1), v_bd[c])
        o_ref[:, i * LANES:(i + 1) * LANES] = (pv * jnp.where(lane_lo, rs[0], rs[1])).astype(BF16)


def _attn(qkv, bias_rows, sinks, tm):
    s = qkv.shape[0]
    kv_blk = Q_COLS // (2 * KV_COLS)
    return pl.pallas_call(
        _attn_kernel,
        grid=(s // tm,),
        in_specs=[pl.BlockSpec(memory_space=pltpu.SMEM),
                  pl.BlockSpec((tm, Q_COLS), lambda i: (i, 0)),
                  pl.BlockSpec((tm, 2 * KV_COLS), lambda i: (i, kv_blk)),
                  pl.BlockSpec((tm, 2 * KV_COLS), lambda i: (jnp.maximum(i - 1, 0), kv_blk)),
                  pl.BlockSpec(bias_rows.shape, lambda i: (0, 0))],
        out_specs=pl.BlockSpec((tm, Q_COLS), lambda i: (i, 0)),
        out_shape=jax.ShapeDtypeStruct((s, Q_COLS), BF16),
        scratch_shapes=[pltpu.VMEM((2, ATTN_HEADS, WINDOW, 2 * WINDOW), F32)],
        compiler_params=pltpu.CompilerParams(dimension_semantics=("arbitrary",),
                                             vmem_limit_bytes=VMEM_LIMIT),
        name="attn",
    )(sinks, qkv, qkv, qkv, bias_rows)


def _rel_bucket_static(n):
    max_exact = REL_BUCKETS // 2
    nf = np.maximum(n, 1).astype(np.float32)
    large = max_exact + (np.log(nf / max_exact) / math.log(REL_MAX_DIST / max_exact)
                         * (REL_BUCKETS - max_exact)).astype(np.int32)
    return np.where(n < max_exact, n, np.minimum(large, REL_BUCKETS - 1))


def _bias_rows(rel_bias):
    rel = (-np.arange(2 * WINDOW)) % (2 * WINDOW)
    idx = np.where(rel < WINDOW, _rel_bucket_static(rel), REL_BUCKETS)
    table = jnp.concatenate([rel_bias.astype(F32), jnp.full((1, ATTN_HEADS), -jnp.inf, F32)], axis=0)
    return table[idx].T


def _merge_kernel(ys_ref, ya_ref, g_ref, x_ref, wbs_ref, wba_ref, wmix_ref, lg_ref, lb_ref, h_ref):
    a = _dot(ys_ref[...], wbs_ref[...])
    b = _dot(ya_ref[...], wba_ref[...])
    merged = g_ref[:, :D_MODEL].astype(F32) * a + g_ref[:, D_MODEL:].astype(F32) * b
    mix = _dot(merged.astype(BF16), wmix_ref[...])
    h_ref[...] = _layer_norm(DEEPNORM_ALPHA * x_ref[...] + mix, lg_ref[...], lb_ref[...])


def _merge(y_ssm, y_attn, gates, x2d, w_bs, w_ba, w_mix, ln_g, ln_b, tm):
    s = x2d.shape[0]
    row = lambda i: (i, 0)
    const = lambda i: (0, 0)
    full = lambda a: pl.BlockSpec(a.shape, const)
    return pl.pallas_call(
        _merge_kernel,
        grid=(s // tm,),
        in_specs=[pl.BlockSpec((tm, SSM_D_INNER), row), pl.BlockSpec((tm, Q_COLS), row),
                  pl.BlockSpec((tm, GATE_COLS), row), pl.BlockSpec((tm, D_MODEL), row),
                  full(w_bs), full(w_ba), full(w_mix), full(ln_g), full(ln_b)],
        out_specs=pl.BlockSpec((tm, D_MODEL), row),
        out_shape=jax.ShapeDtypeStruct((s, D_MODEL), F32),
        compiler_params=pltpu.CompilerParams(dimension_semantics=("arbitrary",),
                                             vmem_limit_bytes=VMEM_LIMIT),
        name="merge",
    )(y_ssm, y_attn, gates, x2d, w_bs, w_ba, w_mix, ln_g, ln_b)


def _ffn_kernel(h_ref, wup_ref, cw_ref, cb_ref, wdn_ref, lg_ref, lb_ref, o_ref, halo_ref, act_ref):
    tm = h_ref.shape[0]

    @pl.when(pl.program_id(0) == 0)
    def _():
        halo_ref[...] = jnp.zeros_like(halo_ref)

    hb = h_ref[...].astype(BF16)
    chunk = 256

    def conv_cols(cols):
        u = _dot(hb, wup_ref[:, cols])
        prev = halo_ref[:, cols]
        halo_ref[:, cols] = u[tm - HALO_ROWS:, :]
        return _causal_conv(u, prev, cw_ref, cb_ref, cols, FFN_CONV)

    for c in range(0, D_FF, chunk):
        gate = conv_cols(slice(c, c + chunk))
        val = conv_cols(slice(D_FF + c, D_FF + c + chunk))
        act_ref[:, c:c + chunk] = (_silu(gate) * val).astype(BF16)

    out = _dot(act_ref[...], wdn_ref[...])
    o_ref[...] = _layer_norm(DEEPNORM_ALPHA * h_ref[...] + out, lg_ref[...], lb_ref[...])


def _ffn(h1, w_up, conv_w, conv_b, w_down, ln_g, ln_b, tm):
    s = h1.shape[0]
    row = lambda i: (i, 0)
    const = lambda i: (0, 0)
    full = lambda a: pl.BlockSpec(a.shape, const)
    return pl.pallas_call(
        _ffn_kernel,
        grid=(s // tm,),
        in_specs=[pl.BlockSpec((tm, D_MODEL), row),
                  pl.BlockSpec(w_up.shape, const, pipeline_mode=pl.Buffered(1)),
                  full(conv_w), full(conv_b),
                  pl.BlockSpec(w_down.shape, const, pipeline_mode=pl.Buffered(1)),
                  full(ln_g), full(ln_b)],
        out_specs=pl.BlockSpec((tm, D_MODEL), row),
        out_shape=jax.ShapeDtypeStruct((s, D_MODEL), F32),
        scratch_shapes=[pltpu.VMEM((HALO_ROWS, 2 * D_FF), F32),
                        pltpu.VMEM((tm, D_FF), BF16)],
        compiler_params=pltpu.CompilerParams(dimension_semantics=("arbitrary",),
                                             vmem_limit_bytes=VMEM_LIMIT),
        name="ffn",
    )(h1, w_up, conv_w, conv_b, w_down, ln_g, ln_b)


def _permute_w_in(w):
    dt_lo = SSM_D_INNER + SSM_CONV_DIM
    dt_hi = dt_lo + SSM_HEADS
    pad = jnp.zeros((w.shape[0], DT_PAD - SSM_HEADS), w.dtype)
    return jnp.concatenate([w[:, :dt_lo], w[:, dt_hi:], w[:, dt_lo:dt_hi], pad], axis=1).astype(BF16)


def _row(v, width=None):
    v = v.astype(F32).reshape(1, -1)
    if width is not None and v.shape[1] < width:
        v = jnp.pad(v, ((0, 0), (0, width - v.shape[1])))
    return v


def kernel(x, rel_bias, w_in, b_gate, ssm_conv_w, ssm_conv_b, ssm_dt_bias, ssm_a_log, ssm_d, ssm_norm_w,
           attn_sinks, w_branch_ssm, w_branch_attn, w_mix_out, ln1_g, ln1_b, w_up, ffn_conv_w, ffn_conv_b,
           w_down, ln2_g, ln2_b):
    b, s, d = x.shape
    assert (b, d) == (1, D_MODEL) and s % 512 == 0 and w_in.shape[0] == DEPTH
    t = _tiles()
    h = x.reshape(s, d)
    bias_rows = _bias_rows(rel_bias)
    for l in range(DEPTH):
        zs, xc, qkv, gates, dt = _inproj(h, _permute_w_in(w_in[l]), ssm_conv_w[l].astype(F32),
                                         _row(ssm_conv_b[l]), _row(ssm_dt_bias[l], DT_PAD),
                                         _row(b_gate[l]), t["inproj"])
        y_ssm = _ssd(xc, zs, dt, _row(ssm_a_log[l], DT_PAD),
                     _row(jnp.repeat(ssm_d[l], SSM_HEADDIM)), _row(ssm_norm_w[l]), t["ssd"])
        y_attn = _attn(qkv, bias_rows, attn_sinks[l].astype(F32), t["attn"])
        h1 = _merge(y_ssm, y_attn, gates, h, w_branch_ssm[l].astype(BF16), w_branch_attn[l].astype(BF16),
                    w_mix_out[l].astype(BF16), _row(ln1_g[l]), _row(ln1_b[l]), t["merge"])
        h = _ffn(h1, w_up[l].astype(BF16), ffn_conv_w[l].astype(F32), _row(ffn_conv_b[l]),
                 w_down[l].astype(BF16), _row(ln2_g[l]), _row(ln2_b[l]), t["ffn"])
    return h.reshape(b, s, d)
```

```python
import functools
import math

import numpy as np
import jax
import jax.numpy as jnp
from jax import lax
from jax.experimental import pallas as pl
from jax.experimental.pallas import tpu as pltpu

F32 = jnp.float32
BF16 = jnp.bfloat16

D_MODEL = 1024
SSM_D_INNER = 2048
SSM_HEADDIM = 64
SSM_HEADS = 32
SSM_GROUPS = 4
SSM_HEADS_PER_GROUP = 8
SSM_STATE = 128
SSM_CONV = 4
SSM_CHUNK = 128
SSM_CONV_DIM = SSM_D_INNER + 2 * SSM_GROUPS * SSM_STATE
GROUP_WIDTH = SSM_HEADS_PER_GROUP * SSM_HEADDIM
ATTN_HEADS = 16
ATTN_KV_HEADS = 2
ATTN_HEADDIM = 64
WINDOW = 128
REL_BUCKETS = 32
REL_MAX_DIST = 128
Q_COLS = ATTN_HEADS * ATTN_HEADDIM
KV_COLS = ATTN_KV_HEADS * ATTN_HEADDIM
QKV_COLS = Q_COLS + 2 * KV_COLS
GATE_COLS = 2 * D_MODEL
D_FF = 2816
FFN_CONV = 3
DEPTH = 1
DEEPNORM_ALPHA = (2.0 * DEPTH) ** 0.25
LN_EPS = 1e-5
RMS_EPS = 1e-5

LANES = 128
SUBLANES = 8
DT_PAD = LANES
HALO_ROWS = SUBLANES
VMEM_LIMIT = 56 * 1024 * 1024

PROJ_SEGMENTS = (SSM_D_INNER, SSM_CONV_DIM, QKV_COLS, GATE_COLS, DT_PAD)
PROJ_COLS = sum(PROJ_SEGMENTS)

NEG_BIG = -1e30


def _tiles():
    assert SSM_CHUNK == WINDOW
    return dict(inproj=256, mixer=SSM_CHUNK, merge=256, ffn=512)


def _dot(a, b):
    return jnp.dot(a, b, preferred_element_type=F32)


def _dot_nt(a, b):
    return lax.dot_general(a, b, (((1,), (1,)), ((), ())), preferred_element_type=F32)


def _sigmoid(x):
    return 1.0 / (1.0 + jnp.exp(-x))


def _silu(x):
    return x * _sigmoid(x)


def _layer_norm(r, g, b):
    mu = jnp.mean(r, axis=-1, keepdims=True)
    rc = r - mu
    var = jnp.mean(rc * rc, axis=-1, keepdims=True)
    return rc * lax.rsqrt(var + LN_EPS) * g + b


def _shift_rows(x, prev, j):
    row = lax.broadcasted_iota(jnp.int32, prev.shape, 0)
    sh = pltpu.roll(x, j, 0)
    top = jnp.where(row < j, pltpu.roll(prev, j, 0), sh[:HALO_ROWS])
    return jnp.concatenate([top, sh[HALO_ROWS:]], axis=0)


def _causal_conv(x, prev, w_ref, b_ref, cols, taps):
    acc = b_ref[:, cols] + w_ref[taps - 1:taps, cols] * x
    for j in range(1, taps):
        acc = acc + w_ref[taps - 1 - j:taps - j, cols] * _shift_rows(x, prev, j)
    return acc


def _inproj_kernel(x_ref, w_ref, cw_ref, cb_ref, dtb_ref, bg_ref,
                   zs_ref, xc_ref, qkv_ref, g_ref, dt_ref, halo_ref):
    tm = x_ref.shape[0]

    @pl.when(pl.program_id(0) == 0)
    def _():
        halo_ref[...] = jnp.zeros_like(halo_ref)

    xb = x_ref[...].astype(BF16)
    chunk = 256
    offs = [sum(PROJ_SEGMENTS[:i]) for i in range(len(PROJ_SEGMENTS))]

    def proj(seg, c, w):
        return _dot(xb, w_ref[:, offs[seg] + c:offs[seg] + c + w])

    for c in range(0, SSM_D_INNER, chunk):
        zs_ref[:, c:c + chunk] = _silu(proj(0, c, chunk)).astype(BF16)
    for c in range(0, SSM_CONV_DIM, chunk):
        cols = slice(c, c + chunk)
        u = proj(1, c, chunk)
        prev = halo_ref[:, cols]
        halo_ref[:, cols] = u[tm - HALO_ROWS:, :]
        xc_ref[:, cols] = _silu(_causal_conv(u, prev, cw_ref, cb_ref, cols, SSM_CONV)).astype(BF16)
    for c in range(0, QKV_COLS, chunk):
        w = min(chunk, QKV_COLS - c)
        qkv_ref[:, c:c + w] = proj(2, c, w).astype(BF16)
    for c in range(0, GATE_COLS, chunk):
        g_ref[:, c:c + chunk] = _sigmoid(proj(3, c, chunk) + bg_ref[:, c:c + chunk]).astype(BF16)
    dtr = proj(4, 0, DT_PAD) + dtb_ref[...]
    softplus = jnp.maximum(dtr, 0.0) + jnp.log(1.0 + jnp.exp(-jnp.abs(dtr)))
    lane_ok = lax.broadcasted_iota(jnp.int32, dtr.shape, 1) < SSM_HEADS
    dt_ref[...] = jnp.where(lane_ok, softplus, 0.0)


def _inproj(x2d, w_perm, conv_w, conv_b, dt_bias, b_gate, tm):
    s = x2d.shape[0]
    row = lambda i: (i, 0)
    const = lambda i: (0, 0)
    full = lambda a: pl.BlockSpec(a.shape, const)
    outs = [jax.ShapeDtypeStruct((s, n), dt) for n, dt in
            zip(PROJ_SEGMENTS, (BF16, BF16, BF16, BF16, F32))]
    return pl.pallas_call(
        _inproj_kernel,
        grid=(s // tm,),
        in_specs=[pl.BlockSpec((tm, D_MODEL), row),
                  pl.BlockSpec((D_MODEL, PROJ_COLS), const, pipeline_mode=pl.Buffered(1)),
                  full(conv_w), full(conv_b), full(dt_bias), full(b_gate)],
        out_specs=[pl.BlockSpec((tm, n), row) for n in PROJ_SEGMENTS],
        out_shape=outs,
        scratch_shapes=[pltpu.VMEM((HALO_ROWS, SSM_CONV_DIM), F32)],
        compiler_params=pltpu.CompilerParams(dimension_semantics=("arbitrary",),
                                             vmem_limit_bytes=VMEM_LIMIT),
        name="inproj",
    )(x2d, w_perm, conv_w, conv_b, dt_bias, b_gate)


def _split3(x):
    hi = x.astype(BF16).astype(F32)
    r = x - hi
    mid = r.astype(BF16).astype(F32)
    lo = (r - mid).astype(BF16).astype(F32)
    return hi, mid, lo


def _pack3(x):
    hi, mid, lo = _split3(x)
    return (hi + pltpu.roll(mid, SSM_HEADS, 1) + pltpu.roll(lo, 2 * SSM_HEADS, 1)).astype(BF16)


def _ssd_body(conv_ref, zs_ref, dt_ref, alog_ref, dexp_ref, nw_ref, e3_ref, ec3_ref, y_ref, state_ref):
    L = SSM_CHUNK
    lane_ok = lax.broadcasted_iota(jnp.int32, (L, DT_PAD), 1) < SSM_HEADS
    dt = dt_ref[...]
    a_dt = dt * (-jnp.exp(alog_ref[...]))

    ri = lax.broadcasted_iota(jnp.int32, (L, L), 0)
    ci = lax.broadcasted_iota(jnp.int32, (L, L), 1)
    causal = ri >= ci
    tril = jnp.where(causal, 1.0, 0.0).astype(BF16)
    hi, mid, lo = _split3(a_dt)
    a_cs = _dot(tril, hi.astype(BF16)) + _dot(tril, mid.astype(BF16)) + _dot(tril, lo.astype(BF16))

    a_last = a_cs[L - 1:L, :]
    decay_out = jnp.where(lane_ok, jnp.exp(a_cs), 0.0)
    w_state = dt * jnp.exp(a_last - a_cs)

    e3 = e3_ref[...]
    dt_e = _dot(_pack3(dt), e3)
    ws_e = _dot(_pack3(w_state), e3)
    do_e = _dot(_pack3(decay_out), e3)
    cd_e = do_e[L - 1:L, :]

    a_cs_t = a_cs.T
    acs3 = _pack3(a_cs)
    lane_lo = lax.broadcasted_iota(jnp.int32, (L, LANES), 1) < SSM_HEADDIM

    for g in range(SSM_GROUPS):
        gsl = slice(g * GROUP_WIDTH, (g + 1) * GROUP_WIDTH)
        xs_g = conv_ref[:, gsl].astype(F32)
        b_off = SSM_D_INNER + g * SSM_STATE
        c_off = SSM_D_INNER + SSM_GROUPS * SSM_STATE + g * SSM_STATE
        b_g = conv_ref[:, b_off:b_off + SSM_STATE]
        c_g = conv_ref[:, c_off:c_off + SSM_STATE]
        cb = _dot_nt(c_g, b_g)
        xdt_b = (xs_g * dt_e[:, gsl]).astype(BF16)
        xdtd_b = (xs_g * ws_e[:, gsl]).astype(BF16)

        st = state_ref[g]
        y_off = _dot(c_g, st.astype(BF16)) * do_e[:, gsl]
        state_ref[g] = st * cd_e[:, gsl] + _dot(b_g.astype(F32).T.astype(BF16), xdtd_b)

        pieces = []
        for pr in range(SSM_HEADS_PER_GROUP // 2):
            h1 = g * SSM_HEADS_PER_GROUP + 2 * pr
            acol = _dot(acs3, ec3_ref[:, h1 * L:(h1 + 2) * L])
            ms = []
            for k in range(2):
                seg = acol[:, k * L:(k + 1) * L] - a_cs_t[h1 + k:h1 + k + 1, :]
                ms.append((cb * jnp.exp(jnp.where(causal, seg, NEG_BIG))).astype(BF16))
            xp = xdt_b[:, pr * LANES:(pr + 1) * LANES]
            zero = jnp.zeros_like(xp)
            rhs = jnp.concatenate([jnp.where(lane_lo, xp, zero), jnp.where(lane_lo, zero, xp)], axis=0)
            pieces.append(_dot(jnp.concatenate(ms, axis=1), rhs))
            if pr % 2 == 1:
                yield
        y_g =jnp.concatenate(pieces, axis=1) + y_off + xs_g * dexp_ref[:, gsl]

        yz = y_g * zs_ref[:, gsl].astype(F32)
        ms_ = jnp.mean(yz * yz, axis=-1, keepdims=True)
        y_ref[:, gsl] = (yz * lax.rsqrt(ms_ + RMS_EPS) * nw_ref[:, gsl]).astype(BF16)


def _expansion_matrices():
    k = np.arange(LANES)[:, None]
    valid = k < 3 * SSM_HEADS
    head = k % SSM_HEADS
    e3 = (valid & (head == (np.arange(SSM_D_INNER)[None, :] // SSM_HEADDIM))).astype(np.float32)
    ec3 = (valid & (head == (np.arange(SSM_HEADS * SSM_CHUNK)[None, :] // SSM_CHUNK))).astype(np.float32)
    return jnp.asarray(e3, BF16), jnp.asarray(ec3, BF16)


def _build_bias_mask(ur_ref, bm_ref):
    W = WINDOW
    first = lax.broadcasted_iota(jnp.int32, (W, 2 * W), 1) >= W
    for h in range(ATTN_HEADS):
        band_h = pltpu.roll(jnp.broadcast_to(ur_ref[h:h + 1, :], (W, 2 * W)), W, 1,
                            stride=1, stride_axis=0)
        bm_ref[1, h] = band_h
        bm_ref[0, h] = jnp.where(first, band_h, -jnp.inf)


def _attn_body(sink_ref, q_ref, kvc_ref, kvp_ref, o_ref, bm_ref):
    W = WINDOW
    lane_lo2 = lax.broadcasted_iota(jnp.int32, (2 * W, LANES), 1) < ATTN_HEADDIM
    lane_lo = lax.broadcasted_iota(jnp.int32, (W, LANES), 1) < ATTN_HEADDIM
    slab = jnp.minimum(pl.program_id(0), 1)

    def band(lo_col):
        return jnp.concatenate([kvp_ref[:, lo_col:lo_col + KV_COLS].astype(F32),
                                kvc_ref[:, lo_col:lo_col + KV_COLS].astype(F32)], axis=0)

    kband = band(0) * (ATTN_HEADDIM ** -0.5)
    vband = band(KV_COLS)
    ksw = pltpu.roll(kband, ATTN_HEADDIM, 1)
    vsw = pltpu.roll(vband, ATTN_HEADDIM, 1)
    k_var = ((jnp.where(lane_lo2, kband, 0.0).astype(BF16), jnp.where(lane_lo2, 0.0, ksw).astype(BF16)),
             (jnp.where(lane_lo2, ksw, 0.0).astype(BF16), jnp.where(lane_lo2, 0.0, kband).astype(BF16)))
    v_bd = (jnp.concatenate([jnp.where(lane_lo2, vband, 0.0), jnp.where(lane_lo2, 0.0, vsw)], axis=0).astype(BF16),
            jnp.concatenate([jnp.where(lane_lo2, vsw, 0.0), jnp.where(lane_lo2, 0.0, vband)], axis=0).astype(BF16))

    for i in range(ATTN_HEADS // 2):
        c = (2 * i) // (ATTN_HEADS // ATTN_KV_HEADS)
        qp = q_ref[:, i * LANES:(i + 1) * LANES]
        ps, rs = [], []
        for j in range(2):
            h = 2 * i + j
            sink = sink_ref[h]
            s = _dot_nt(qp, k_var[c][j]) + bm_ref[slab, h]
            m = jnp.maximum(jnp.max(s, axis=-1, keepdims=True), sink)
            p = jnp.exp(s - m)
            den = jnp.sum(p, axis=-1, keepdims=True) + jnp.exp(sink - m)
            ps.append(p.astype(BF16))
            rs.append(1.0 / den)
        pv = _dot(jnp.concatenate(ps, axis=1), v_bd[c])
        o_ref[:, i * LANES:(i + 1) * LANES] = (pv * jnp.where(lane_lo, rs[0], rs[1])).astype(BF16)
        yield


def _mixer_kernel(sink_ref, conv_ref, zs_ref, dt_ref, alog_ref, dexp_ref, nw_ref, e3_ref, ec3_ref,
                  q_ref, kvc_ref, kvp_ref, ur_ref, ys_ref, ya_ref, state_ref, bm_ref):
    @pl.when(pl.program_id(0) == 0)
    def _():
        state_ref[...] = jnp.zeros_like(state_ref)
        _build_bias_mask(ur_ref, bm_ref)

    live = [_ssd_body(conv_ref, zs_ref, dt_ref, alog_ref, dexp_ref, nw_ref, e3_ref, ec3_ref, ys_ref, state_ref),
            _attn_body(sink_ref, q_ref, kvc_ref, kvp_ref, ya_ref, bm_ref)]
    while live:
        for body in list(live):
            if next(body, "done") == "done":
                live.remove(body)


def _mixer(xc, zs, dt, qkv, a_log, d_exp, norm_w, bias_rows, sinks, tm):
    s = xc.shape[0]
    e3, ec3 = _expansion_matrices()
    row = lambda i: (i, 0)
    const = lambda i: (0, 0)
    full = lambda a: pl.BlockSpec(a.shape, const)
    kv_blk = Q_COLS // (2 * KV_COLS)
    return pl.pallas_call(
        _mixer_kernel,
        grid=(s // tm,),
        in_specs=[pl.BlockSpec(memory_space=pltpu.SMEM),
                  pl.BlockSpec((tm, SSM_CONV_DIM), row),
                  pl.BlockSpec((tm, SSM_D_INNER), row),
                  pl.BlockSpec((tm, DT_PAD), row),
                  full(a_log), full(d_exp), full(norm_w), full(e3), full(ec3),
                  pl.BlockSpec((tm, Q_COLS), row),
                  pl.BlockSpec((tm, 2 * KV_COLS), lambda i: (i, kv_blk)),
                  pl.BlockSpec((tm, 2 * KV_COLS), lambda i: (jnp.maximum(i - 1, 0), kv_blk)),
                  full(bias_rows)],
        out_specs=[pl.BlockSpec((tm, SSM_D_INNER), row), pl.BlockSpec((tm, Q_COLS), row)],
        out_shape=[jax.ShapeDtypeStruct((s, SSM_D_INNER), BF16), jax.ShapeDtypeStruct((s, Q_COLS), BF16)],
        scratch_shapes=[pltpu.VMEM((SSM_GROUPS, SSM_STATE, GROUP_WIDTH), F32),
                        pltpu.VMEM((2, ATTN_HEADS, WINDOW, 2 * WINDOW), F32)],
        compiler_params=pltpu.CompilerParams(dimension_semantics=("arbitrary",),
                                             vmem_limit_bytes=VMEM_LIMIT),
        name="mixer",
    )(sinks, xc, zs, dt, a_log, d_exp, norm_w, e3, ec3, qkv, qkv, qkv, bias_rows)


def _rel_bucket_static(n):
    max_exact = REL_BUCKETS // 2
    nf = np.maximum(n, 1).astype(np.float32)
    large = max_exact + (np.log(nf / max_exact) / math.log(REL_MAX_DIST / max_exact)
                         * (REL_BUCKETS - max_exact)).astype(np.int32)
    return np.where(n < max_exact, n, np.minimum(large, REL_BUCKETS - 1))


def _bias_rows(rel_bias):
    rel = (-np.arange(2 * WINDOW)) % (2 * WINDOW)
    idx = np.where(rel < WINDOW, _rel_bucket_static(rel), REL_BUCKETS)
    table = jnp.concatenate([rel_bias.astype(F32), jnp.full((1, ATTN_HEADS), -jnp.inf, F32)], axis=0)
    return table[idx].T


def _merge_kernel(ys_ref, ya_ref, g_ref, x_ref, wbs_ref, wba_ref, wmix_ref, lg_ref, lb_ref, h_ref):
    a = _dot(ys_ref[...], wbs_ref[...])
    b = _dot(ya_ref[...], wba_ref[...])
    merged = g_ref[:, :D_MODEL].astype(F32) * a + g_ref[:, D_MODEL:].astype(F32) * b
    mix = _dot(merged.astype(BF16), wmix_ref[...])
    h_ref[...] = _layer_norm(DEEPNORM_ALPHA * x_ref[...] + mix, lg_ref[...], lb_ref[...])


def _merge(y_ssm, y_attn, gates, x2d, w_bs, w_ba, w_mix, ln_g, ln_b, tm):
    s = x2d.shape[0]
    row = lambda i: (i, 0)
    const = lambda i: (0, 0)
    full = lambda a: pl.BlockSpec(a.shape, const)
    return pl.pallas_call(
        _merge_kernel,
        grid=(s // tm,),
        in_specs=[pl.BlockSpec((tm, SSM_D_INNER), row), pl.BlockSpec((tm, Q_COLS), row),
                  pl.BlockSpec((tm, GATE_COLS), row), pl.BlockSpec((tm, D_MODEL), row),
                  full(w_bs), full(w_ba), full(w_mix), full(ln_g), full(ln_b)],
        out_specs=pl.BlockSpec((tm, D_MODEL), row),
        out_shape=jax.ShapeDtypeStruct((s, D_MODEL), F32),
        compiler_params=pltpu.CompilerParams(dimension_semantics=("arbitrary",),
                                             vmem_limit_bytes=VMEM_LIMIT),
        name="merge",
    )(y_ssm, y_attn, gates, x2d, w_bs, w_ba, w_mix, ln_g, ln_b)


def _ffn_kernel(h_ref, wup_ref, cw_ref, cb_ref, wdn_ref, lg_ref, lb_ref, o_ref, halo_ref, act_ref):
    tm = h_ref.shape[0]

    @pl.when(pl.program_id(0) == 0)
    def _():
        halo_ref[...] = jnp.zeros_like(halo_ref)

    hb = h_ref[...].astype(BF16)
    chunk = 256

    def conv_cols(cols):
        u = _dot(hb, wup_ref[:, cols])
        prev = halo_ref[:, cols]
        halo_ref[:, cols] = u[tm - HALO_ROWS:, :]
        return _causal_conv(u, prev, cw_ref, cb_ref, cols, FFN_CONV)

    for c in range(0, D_FF, chunk):
        gate = conv_cols(slice(c, c + chunk))
        val = conv_cols(slice(D_FF + c, D_FF + c + chunk))
        act_ref[:, c:c + chunk] = (_silu(gate) * val).astype(BF16)

    out = _dot(act_ref[...], wdn_ref[...])
    o_ref[...] = _layer_norm(DEEPNORM_ALPHA * h_ref[...] + out, lg_ref[...], lb_ref[...])


def _ffn(h1, w_up, conv_w, conv_b, w_down, ln_g, ln_b, tm):
    s = h1.shape[0]
    row = lambda i: (i, 0)
    const = lambda i: (0, 0)
    full = lambda a: pl.BlockSpec(a.shape, const)
    return pl.pallas_call(
        _ffn_kernel,
        grid=(s // tm,),
        in_specs=[pl.BlockSpec((tm, D_MODEL), row),
                  pl.BlockSpec(w_up.shape, const, pipeline_mode=pl.Buffered(1)),
                  full(conv_w), full(conv_b),
                  pl.BlockSpec(w_down.shape, const, pipeline_mode=pl.Buffered(1)),
                  full(ln_g), full(ln_b)],
        out_specs=pl.BlockSpec((tm, D_MODEL), row),
        out_shape=jax.ShapeDtypeStruct((s, D_MODEL), F32),
        scratch_shapes=[pltpu.VMEM((HALO_ROWS, 2 * D_FF), F32),
                        pltpu.VMEM((tm, D_FF), BF16)],
        compiler_params=pltpu.CompilerParams(dimension_semantics=("arbitrary",),
                                             vmem_limit_bytes=VMEM_LIMIT),
        name="ffn",
    )(h1, w_up, conv_w, conv_b, w_down, ln_g, ln_b)


def _permute_w_in(w):
    dt_lo = SSM_D_INNER + SSM_CONV_DIM
    dt_hi = dt_lo + SSM_HEADS
    pad = jnp.zeros((w.shape[0], DT_PAD - SSM_HEADS), w.dtype)
    return jnp.concatenate([w[:, :dt_lo], w[:, dt_hi:], w[:, dt_lo:dt_hi], pad], axis=1).astype(BF16)


def _row(v, width=None):
    v = v.astype(F32).reshape(1, -1)
    if width is not None and v.shape[1] < width:
        v = jnp.pad(v, ((0, 0), (0, width - v.shape[1])))
    return v


def kernel(x, rel_bias, w_in, b_gate, ssm_conv_w, ssm_conv_b, ssm_dt_bias, ssm_a_log, ssm_d, ssm_norm_w,
           attn_sinks, w_branch_ssm, w_branch_attn, w_mix_out, ln1_g, ln1_b, w_up, ffn_conv_w, ffn_conv_b,
           w_down, ln2_g, ln2_b):
    b, s, d = x.shape
    assert (b, d) == (1, D_MODEL) and s % 512 == 0 and w_in.shape[0] == DEPTH
    t = _tiles()
    h = x.reshape(s, d)
    bias_rows = _bias_rows(rel_bias)
    for l in range(DEPTH):
        zs, xc, qkv, gates, dt = _inproj(h, _permute_w_in(w_in[l]), ssm_conv_w[l].astype(F32),
                                         _row(ssm_conv_b[l]), _row(ssm_dt_bias[l], DT_PAD),
                                         _row(b_gate[l]), t["inproj"])
        y_ssm, y_attn = _mixer(xc, zs, dt, qkv, _row(ssm_a_log[l], DT_PAD),
                               _row(jnp.repeat(ssm_d[l], SSM_HEADDIM)), _row(ssm_norm_w[l]),
                               bias_rows, attn_sinks[l].astype(F32), t["mixer"])
        h1 = _merge(y_ssm, y_attn, gates, h, w_branch_ssm[l].astype(BF16), w_branch_attn[l].astype(BF16),
                    w_mix_out[l].astype(BF16), _row(ln1_g[l]), _row(ln1_b[l]), t["merge"])
        h = _ffn(h1, w_up[l].astype(BF16), ffn_conv_w[l].astype(F32), _row(ffn_conv_b[l]),
                 w_down[l].astype(BF16), _row(ln2_g[l]), _row(ln2_b[l]), t["ffn"])
    return h.reshape(b, s, d)
```

```python
import functools
import math

import numpy as np
import jax
import jax.numpy as jnp
from jax import lax
from jax.experimental import pallas as pl
from jax.experimental.pallas import tpu as pltpu

F32 = jnp.float32
BF16 = jnp.bfloat16

D_MODEL = 1024
SSM_D_INNER = 2048
SSM_HEADDIM = 64
SSM_HEADS = 32
SSM_GROUPS = 4
SSM_HEADS_PER_GROUP = 8
SSM_STATE = 128
SSM_CONV = 4
SSM_CHUNK = 128
SSM_CONV_DIM = SSM_D_INNER + 2 * SSM_GROUPS * SSM_STATE
GROUP_WIDTH = SSM_HEADS_PER_GROUP * SSM_HEADDIM
ATTN_HEADS = 16
ATTN_KV_HEADS = 2
ATTN_HEADDIM = 64
WINDOW = 128
REL_BUCKETS = 32
REL_MAX_DIST = 128
Q_COLS = ATTN_HEADS * ATTN_HEADDIM
KV_COLS = ATTN_KV_HEADS * ATTN_HEADDIM
QKV_COLS = Q_COLS + 2 * KV_COLS
GATE_COLS = 2 * D_MODEL
D_FF = 2816
FFN_CONV = 3
DEPTH = 1
DEEPNORM_ALPHA = (2.0 * DEPTH) ** 0.25
LN_EPS = 1e-5
RMS_EPS = 1e-5

LANES = 128
SUBLANES = 8
DT_PAD = LANES
HALO_ROWS = SUBLANES
VMEM_LIMIT = 56 * 1024 * 1024

PROJ_SEGMENTS = (SSM_D_INNER, SSM_CONV_DIM, QKV_COLS, GATE_COLS, DT_PAD)
PROJ_COLS = sum(PROJ_SEGMENTS)

NEG_BIG = -1e30


def _tiles():
    assert SSM_CHUNK == WINDOW
    return dict(inproj=256, mixer=2 * SSM_CHUNK, merge=256, ffn=512)


def _dot(a, b):
    return jnp.dot(a, b, preferred_element_type=F32)


def _dot_nt(a, b):
    return lax.dot_general(a, b, (((1,), (1,)), ((), ())), preferred_element_type=F32)


def _sigmoid(x):
    return 1.0 / (1.0 + jnp.exp(-x))


def _silu(x):
    return x * _sigmoid(x)


def _layer_norm(r, g, b):
    mu = jnp.mean(r, axis=-1, keepdims=True)
    rc = r - mu
    var = jnp.mean(rc * rc, axis=-1, keepdims=True)
    return rc * lax.rsqrt(var + LN_EPS) * g + b


def _shift_rows(x, prev, j):
    row = lax.broadcasted_iota(jnp.int32, prev.shape, 0)
    sh = pltpu.roll(x, j, 0)
    top = jnp.where(row < j, pltpu.roll(prev, j, 0), sh[:HALO_ROWS])
    return jnp.concatenate([top, sh[HALO_ROWS:]], axis=0)


def _causal_conv(x, prev, w_ref, b_ref, cols, taps):
    acc = b_ref[:, cols] + w_ref[taps - 1:taps, cols] * x
    for j in range(1, taps):
        acc = acc + w_ref[taps - 1 - j:taps - j, cols] * _shift_rows(x, prev, j)
    return acc


def _inproj_kernel(x_ref, w_ref, cw_ref, cb_ref, dtb_ref, bg_ref,
                   zs_ref, xc_ref, qkv_ref, g_ref, dt_ref, halo_ref):
    tm = x_ref.shape[0]

    @pl.when(pl.program_id(0) == 0)
    def _():
        halo_ref[...] = jnp.zeros_like(halo_ref)

    xb = x_ref[...].astype(BF16)
    chunk = 256
    offs = [sum(PROJ_SEGMENTS[:i]) for i in range(len(PROJ_SEGMENTS))]

    def proj(seg, c, w):
        return _dot(xb, w_ref[:, offs[seg] + c:offs[seg] + c + w])

    for c in range(0, SSM_D_INNER, chunk):
        zs_ref[:, c:c + chunk] = _silu(proj(0, c, chunk)).astype(BF16)
    for c in range(0, SSM_CONV_DIM, chunk):
        cols = slice(c, c + chunk)
        u = proj(1, c, chunk)
        prev = halo_ref[:, cols]
        halo_ref[:, cols] = u[tm - HALO_ROWS:, :]
        xc_ref[:, cols] = _silu(_causal_conv(u, prev, cw_ref, cb_ref, cols, SSM_CONV)).astype(BF16)
    for c in range(0, QKV_COLS, chunk):
        w = min(chunk, QKV_COLS - c)
        qkv_ref[:, c:c + w] = proj(2, c, w).astype(BF16)
    for c in range(0, GATE_COLS, chunk):
        g_ref[:, c:c + chunk] = _sigmoid(proj(3, c, chunk) + bg_ref[:, c:c + chunk]).astype(BF16)
    dtr = proj(4, 0, DT_PAD) + dtb_ref[...]
    softplus = jnp.maximum(dtr, 0.0) + jnp.log(1.0 + jnp.exp(-jnp.abs(dtr)))
    lane_ok = lax.broadcasted_iota(jnp.int32, dtr.shape, 1) < SSM_HEADS
    dt_ref[...] = jnp.where(lane_ok, softplus, 0.0)


def _inproj(x2d, w_perm, conv_w, conv_b, dt_bias, b_gate, tm):
    s = x2d.shape[0]
    row = lambda i: (i, 0)
    const = lambda i: (0, 0)
    full = lambda a: pl.BlockSpec(a.shape, const)
    outs = [jax.ShapeDtypeStruct((s, n), dt) for n, dt in
            zip(PROJ_SEGMENTS, (BF16, BF16, BF16, BF16, F32))]
    return pl.pallas_call(
        _inproj_kernel,
        grid=(s // tm,),
        in_specs=[pl.BlockSpec((tm, D_MODEL), row),
                  pl.BlockSpec((D_MODEL, PROJ_COLS), const, pipeline_mode=pl.Buffered(1)),
                  full(conv_w), full(conv_b), full(dt_bias), full(b_gate)],
        out_specs=[pl.BlockSpec((tm, n), row) for n in PROJ_SEGMENTS],
        out_shape=outs,
        scratch_shapes=[pltpu.VMEM((HALO_ROWS, SSM_CONV_DIM), F32)],
        compiler_params=pltpu.CompilerParams(dimension_semantics=("arbitrary",),
                                             vmem_limit_bytes=VMEM_LIMIT),
        name="inproj",
    )(x2d, w_perm, conv_w, conv_b, dt_bias, b_gate)


def _split3(x):
    hi = x.astype(BF16).astype(F32)
    r = x - hi
    mid = r.astype(BF16).astype(F32)
    lo = (r - mid).astype(BF16).astype(F32)
    return hi, mid, lo


def _pack3(x):
    hi, mid, lo = _split3(x)
    return (hi + pltpu.roll(mid, SSM_HEADS, 1) + pltpu.roll(lo, 2 * SSM_HEADS, 1)).astype(BF16)


def _ssd_body(conv_ref, zs_ref, dt_ref, alog_ref, dexp_ref, nw_ref, e3_ref, ec3_ref, y_ref, state_ref):
    L = SSM_CHUNK
    lane_ok = lax.broadcasted_iota(jnp.int32, (L, DT_PAD), 1) < SSM_HEADS
    dt = dt_ref[...]
    a_dt = dt * (-jnp.exp(alog_ref[...]))

    ri = lax.broadcasted_iota(jnp.int32, (L, L), 0)
    ci = lax.broadcasted_iota(jnp.int32, (L, L), 1)
    causal = ri >= ci
    tril = jnp.where(causal, 1.0, 0.0).astype(BF16)
    hi, mid, lo = _split3(a_dt)
    a_cs = _dot(tril, hi.astype(BF16)) + _dot(tril, mid.astype(BF16)) + _dot(tril, lo.astype(BF16))

    a_last = a_cs[L - 1:L, :]
    decay_out_b = jnp.where(lane_ok, jnp.exp(a_cs), 0.0).astype(BF16)
    w_state_b = (dt * jnp.exp(a_last - a_cs)).astype(BF16)
    cd_e = jnp.exp(_dot(_pack3(a_cs[L - SUBLANES:, :]), e3_ref[...])[SUBLANES - 1:, :])

    a_sub_t = (a_cs - jnp.log(dt)).T
    acs3 = _pack3(a_cs)
    lane_lo = lax.broadcasted_iota(jnp.int32, (L, LANES), 1) < SSM_HEADDIM

    def group_start(g):
        gsl = slice(g * GROUP_WIDTH, (g + 1) * GROUP_WIDTH)
        xs_g = conv_ref[:, gsl].astype(F32)
        b_off = SSM_D_INNER + g * SSM_STATE
        c_off = SSM_D_INNER + SSM_GROUPS * SSM_STATE + g * SSM_STATE
        b_g = conv_ref[:, b_off:b_off + SSM_STATE]
        c_g = conv_ref[:, c_off:c_off + SSM_STATE]
        cb = _dot_nt(c_g, b_g)
        xdtd_b = (xs_g * _dot(w_state_b, e3_ref[:, gsl])).astype(BF16)
        st = state_ref[g]
        y_off = _dot(c_g, st.astype(BF16)) * _dot(decay_out_b, e3_ref[:, gsl])
        state_ref[g] = st * cd_e[:, gsl] + _dot(b_g.astype(F32).T.astype(BF16), xdtd_b)
        return cb, y_off + xs_g * dexp_ref[:, gsl]

    def acol_dot(p):
        return _dot(acs3, ec3_ref[:, 2 * p * L:(2 * p + 2) * L])

    pairs_per_group = SSM_HEADS_PER_GROUP // 2
    n_pairs = SSM_HEADS // 2
    started = {0: group_start(0)}
    acols = {0: acol_dot(0)}
    pieces = []
    for p in range(n_pairs):
        g, pr = divmod(p, pairs_per_group)
        if p + 1 < n_pairs:
            acols[p + 1] = acol_dot(p + 1)
        if pr == pairs_per_group - 2 and g + 1 < SSM_GROUPS:
            started[g + 1] = group_start(g + 1)
        cb, y_rest = started[g]
        acol = acols.pop(p)
        ms = []
        for k in range(2):
            seg = acol[:, k * L:(k + 1) * L] - a_sub_t[2 * p + k:2 * p + k + 1, :]
            ms.append((cb * jnp.exp(jnp.where(causal, seg, NEG_BIG))).astype(BF16))
        xp = conv_ref[:, p * LANES:(p + 1) * LANES]
        zero = jnp.zeros_like(xp)
        rhs = jnp.concatenate([jnp.where(lane_lo, xp, zero), jnp.where(lane_lo, zero, xp)], axis=0)
        pieces.append(_dot(jnp.concatenate(ms, axis=1), rhs))
        if pr == pairs_per_group - 1:
            gsl = slice(g * GROUP_WIDTH, (g + 1) * GROUP_WIDTH)
            yz = (jnp.concatenate(pieces, axis=1) + y_rest) * zs_ref[:, gsl].astype(F32)
            ms_ = jnp.mean(yz * yz, axis=-1, keepdims=True)
            y_ref[:, gsl] = (yz * lax.rsqrt(ms_ + RMS_EPS) * nw_ref[:, gsl]).astype(BF16)
            pieces = []
            del started[g]
        if p % 2 == 1:
            yield


def _expansion_matrices():
    k = np.arange(LANES)[:, None]
    valid = k < 3 * SSM_HEADS
    head = k % SSM_HEADS
    e3 = (valid & (head == (np.arange(SSM_D_INNER)[None, :] // SSM_HEADDIM))).astype(np.float32)
    ec3 = (valid & (head == (np.arange(SSM_HEADS * SSM_CHUNK)[None, :] // SSM_CHUNK))).astype(np.float32)
    return jnp.asarray(e3, BF16), jnp.asarray(ec3, BF16)


def _build_bias_mask(ur_ref, bm_ref):
    W = WINDOW
    first = lax.broadcasted_iota(jnp.int32, (W, 2 * W), 1) >= W
    for h in range(ATTN_HEADS):
        band_h = pltpu.roll(jnp.broadcast_to(ur_ref[h:h + 1, :], (W, 2 * W)), W, 1,
                            stride=1, stride_axis=0)
        bm_ref[1, h] = band_h
        bm_ref[0, h] = jnp.where(first, band_h, -jnp.inf)


def _attn_body(sink_ref, q_ref, kvc_ref, kvp_ref, o_ref, bm_ref, slab):
    W = WINDOW
    lane_lo2 = lax.broadcasted_iota(jnp.int32, (2 * W, LANES), 1) < ATTN_HEADDIM
    lane_lo = lax.broadcasted_iota(jnp.int32, (W, LANES), 1) < ATTN_HEADDIM

    def band(lo_col):
        return jnp.concatenate([kvp_ref[:, lo_col:lo_col + KV_COLS].astype(F32),
                                kvc_ref[:, lo_col:lo_col + KV_COLS].astype(F32)], axis=0)

    kband = band(0) * (ATTN_HEADDIM ** -0.5)
    vband = band(KV_COLS)
    ksw = pltpu.roll(kband, ATTN_HEADDIM, 1)
    vsw = pltpu.roll(vband, ATTN_HEADDIM, 1)
    k_var = ((jnp.where(lane_lo2, kband, 0.0).astype(BF16), jnp.where(lane_lo2, 0.0, ksw).astype(BF16)),
             (jnp.where(lane_lo2, ksw, 0.0).astype(BF16), jnp.where(lane_lo2, 0.0, kband).astype(BF16)))
    v_bd = (jnp.concatenate([jnp.where(lane_lo2, vband, 0.0), jnp.where(lane_lo2, 0.0, vsw)], axis=0).astype(BF16),
            jnp.concatenate([jnp.where(lane_lo2, vsw, 0.0), jnp.where(lane_lo2, 0.0, vband)], axis=0).astype(BF16))

    def kv_head(i):
        return (2 * i) // (ATTN_HEADS // ATTN_KV_HEADS)

    def logits(i):
        qp = q_ref[:, i * LANES:(i + 1) * LANES]
        return [_dot_nt(qp, k_var[kv_head(i)][j]) for j in range(2)]

    n_pairs = ATTN_HEADS // 2
    s_next = logits(0)
    for i in range(n_pairs):
        c = kv_head(i)
        s_cur = s_next
        if i + 1 < n_pairs:
            s_next = logits(i + 1)
        ps, rs = [], []
        for j in range(2):
            h = 2 * i + j
            sink = sink_ref[h]
            s = s_cur[j] + bm_ref[slab, h]
            m = jnp.maximum(jnp.max(s, axis=-1, keepdims=True), sink)
            p = jnp.exp(s - m)
            den = jnp.sum(p, axis=-1, keepdims=True) + jnp.exp(sink - m)
            ps.append(p.astype(BF16))
            rs.append(1.0 / den)
        pv = _dot(jnp.concatenate(ps, axis=1), v_bd[c])
        o_ref[:, i * LANES:(i + 1) * LANES] = (pv * jnp.where(lane_lo, rs[0], rs[1])).astype(BF16)
        yield


def _mixer_kernel(sink_ref, conv_ref, zs_ref, dt_ref, alog_ref, dexp_ref, nw_ref, e3_ref, ec3_ref,
                  q_ref, kvc_ref, kvp_ref, ur_ref, ys_ref, ya_ref, state_ref, bm_ref):
    @pl.when(pl.program_id(0) == 0)
    def _():
        state_ref[...] = jnp.zeros_like(state_ref)
        _build_bias_mask(ur_ref, bm_ref)

    L = SSM_CHUNK
    live = []
    for blk in range(conv_ref.shape[0] // L):
        rows = pl.ds(blk * L, L)
        live.append(_ssd_body(conv_ref.at[rows], zs_ref.at[rows], dt_ref.at[rows], alog_ref, dexp_ref, nw_ref,
                              e3_ref, ec3_ref, ys_ref.at[rows], state_ref))
        prev = kvp_ref if blk == 0 else kvc_ref.at[pl.ds((blk - 1) * L, L)]
        slab = jnp.minimum(pl.program_id(0), 1) if blk == 0 else 1
        live.append(_attn_body(sink_ref, q_ref.at[rows], kvc_ref.at[rows], prev, ya_ref.at[rows], bm_ref, slab))
    while live:
        for body in list(live):
            if next(body, "done") == "done":
                live.remove(body)


def _mixer(xc, zs, dt, qkv, a_log, d_exp, norm_w, bias_rows, sinks, tm):
    s = xc.shape[0]
    e3, ec3 = _expansion_matrices()
    row = lambda i: (i, 0)
    const = lambda i: (0, 0)
    full = lambda a: pl.BlockSpec(a.shape, const)
    kv_blk = Q_COLS // (2 * KV_COLS)
    return pl.pallas_call(
        _mixer_kernel,
        grid=(s // tm,),
        in_specs=[pl.BlockSpec(memory_space=pltpu.SMEM),
                  pl.BlockSpec((tm, SSM_CONV_DIM), row),
                  pl.BlockSpec((tm, SSM_D_INNER), row),
                  pl.BlockSpec((tm, DT_PAD), row),
                  full(a_log), full(d_exp), full(norm_w), full(e3), full(ec3),
                  pl.BlockSpec((tm, Q_COLS), row),
                  pl.BlockSpec((tm, 2 * KV_COLS), lambda i: (i, kv_blk)),
                  pl.BlockSpec((WINDOW, 2 * KV_COLS),
                               lambda i: (jnp.maximum(i * (tm // WINDOW) - 1, 0), kv_blk)),
                  full(bias_rows)],
        out_specs=[pl.BlockSpec((tm, SSM_D_INNER), row), pl.BlockSpec((tm, Q_COLS), row)],
        out_shape=[jax.ShapeDtypeStruct((s, SSM_D_INNER), BF16), jax.ShapeDtypeStruct((s, Q_COLS), BF16)],
        scratch_shapes=[pltpu.VMEM((SSM_GROUPS, SSM_STATE, GROUP_WIDTH), F32),
                        pltpu.VMEM((2, ATTN_HEADS, WINDOW, 2 * WINDOW), F32)],
        compiler_params=pltpu.CompilerParams(dimension_semantics=("arbitrary",),
                                             vmem_limit_bytes=VMEM_LIMIT),
        name="mixer",
    )(sinks, xc, zs, dt, a_log, d_exp, norm_w, e3, ec3, qkv, qkv, qkv, bias_rows)


def _rel_bucket_static(n):
    max_exact = REL_BUCKETS // 2
    nf = np.maximum(n, 1).astype(np.float32)
    large = max_exact + (np.log(nf / max_exact) / math.log(REL_MAX_DIST / max_exact)
                         * (REL_BUCKETS - max_exact)).astype(np.int32)
    return np.where(n < max_exact, n, np.minimum(large, REL_BUCKETS - 1))


def _bias_rows(rel_bias):
    rel = (-np.arange(2 * WINDOW)) % (2 * WINDOW)
    idx = np.where(rel < WINDOW, _rel_bucket_static(rel), REL_BUCKETS)
    table = jnp.concatenate([rel_bias.astype(F32), jnp.full((1, ATTN_HEADS), -jnp.inf, F32)], axis=0)
    return table[idx].T


def _merge_kernel(ys_ref, ya_ref, g_ref, x_ref, wbs_ref, wba_ref, wmix_ref, lg_ref, lb_ref, h_ref):
    a = _dot(ys_ref[...], wbs_ref[...])
    b = _dot(ya_ref[...], wba_ref[...])
    merged = g_ref[:, :D_MODEL].astype(F32) * a + g_ref[:, D_MODEL:].astype(F32) * b
    mix = _dot(merged.astype(BF16), wmix_ref[...])
    h_ref[...] = _layer_norm(DEEPNORM_ALPHA * x_ref[...] + mix, lg_ref[...], lb_ref[...])


def _merge(y_ssm, y_attn, gates, x2d, w_bs, w_ba, w_mix, ln_g, ln_b, tm):
    s = x2d.shape[0]
    row = lambda i: (i, 0)
    const = lambda i: (0, 0)
    full = lambda a: pl.BlockSpec(a.shape, const)
    return pl.pallas_call(
        _merge_kernel,
        grid=(s // tm,),
        in_specs=[pl.BlockSpec((tm, SSM_D_INNER), row), pl.BlockSpec((tm, Q_COLS), row),
                  pl.BlockSpec((tm, GATE_COLS), row), pl.BlockSpec((tm, D_MODEL), row),
                  full(w_bs), full(w_ba), full(w_mix), full(ln_g), full(ln_b)],
        out_specs=pl.BlockSpec((tm, D_MODEL), row),
        out_shape=jax.ShapeDtypeStruct((s, D_MODEL), F32),
        compiler_params=pltpu.CompilerParams(dimension_semantics=("arbitrary",),
                                             vmem_limit_bytes=VMEM_LIMIT),
        name="merge",
    )(y_ssm, y_attn, gates, x2d, w_bs, w_ba, w_mix, ln_g, ln_b)


def _ffn_kernel(h_ref, wup_ref, cw_ref, cb_ref, wdn_ref, lg_ref, lb_ref, o_ref, halo_ref, act_ref):
    tm = h_ref.shape[0]

    @pl.when(pl.program_id(0) == 0)
    def _():
        halo_ref[...] = jnp.zeros_like(halo_ref)

    hb = h_ref[...].astype(BF16)
    chunk = 256

    def conv_cols(cols):
        u = _dot(hb, wup_ref[:, cols])
        prev = halo_ref[:, cols]
        halo_ref[:, cols] = u[tm - HALO_ROWS:, :]
        return _causal_conv(u, prev, cw_ref, cb_ref, cols, FFN_CONV)

    for c in range(0, D_FF, chunk):
        gate = conv_cols(slice(c, c + chunk))
        val = conv_cols(slice(D_FF + c, D_FF + c + chunk))
        act_ref[:, c:c + chunk] = (_silu(gate) * val).astype(BF16)

    out = _dot(act_ref[...], wdn_ref[...])
    o_ref[...] = _layer_norm(DEEPNORM_ALPHA * h_ref[...] + out, lg_ref[...], lb_ref[...])


def _ffn(h1, w_up, conv_w, conv_b, w_down, ln_g, ln_b, tm):
    s = h1.shape[0]
    row = lambda i: (i, 0)
    const = lambda i: (0, 0)
    full = lambda a: pl.BlockSpec(a.shape, const)
    return pl.pallas_call(
        _ffn_kernel,
        grid=(s // tm,),
        in_specs=[pl.BlockSpec((tm, D_MODEL), row),
                  pl.BlockSpec(w_up.shape, const, pipeline_mode=pl.Buffered(1)),
                  full(conv_w), full(conv_b),
                  pl.BlockSpec(w_down.shape, const, pipeline_mode=pl.Buffered(1)),
                  full(ln_g), full(ln_b)],
        out_specs=pl.BlockSpec((tm, D_MODEL), row),
        out_shape=jax.ShapeDtypeStruct((s, D_MODEL), F32),
        scratch_shapes=[pltpu.VMEM((HALO_ROWS, 2 * D_FF), F32),
                        pltpu.VMEM((tm, D_FF), BF16)],
        compiler_params=pltpu.CompilerParams(dimension_semantics=("arbitrary",),
                                             vmem_limit_bytes=VMEM_LIMIT),
        name="ffn",
    )(h1, w_up, conv_w, conv_b, w_down, ln_g, ln_b)


def _permute_w_in(w):
    dt_lo = SSM_D_INNER + SSM_CONV_DIM
    dt_hi = dt_lo + SSM_HEADS
    pad = jnp.zeros((w.shape[0], DT_PAD - SSM_HEADS), w.dtype)
    return jnp.concatenate([w[:, :dt_lo], w[:, dt_hi:], w[:, dt_lo:dt_hi], pad], axis=1).astype(BF16)


def _row(v, width=None):
    v = v.astype(F32).reshape(1, -1)
    if width is not None and v.shape[1] < width:
        v = jnp.pad(v, ((0, 0), (0, width - v.shape[1])))
    return v


def kernel(x, rel_bias, w_in, b_gate, ssm_conv_w, ssm_conv_b, ssm_dt_bias, ssm_a_log, ssm_d, ssm_norm_w,
           attn_sinks, w_branch_ssm, w_branch_attn, w_mix_out, ln1_g, ln1_b, w_up, ffn_conv_w, ffn_conv_b,
           w_down, ln2_g, ln2_b):
    b, s, d = x.shape
    assert (b, d) == (1, D_MODEL) and s % 512 == 0 and w_in.shape[0] == DEPTH
    t = _tiles()
    h = x.reshape(s, d)
    bias_rows = _bias_rows(rel_bias)
    for l in range(DEPTH):
        zs, xc, qkv, gates, dt = _inproj(h, _permute_w_in(w_in[l]), ssm_conv_w[l].astype(F32),
                                         _row(ssm_conv_b[l]), _row(ssm_dt_bias[l], DT_PAD),
                                         _row(b_gate[l]), t["inproj"])
        y_ssm, y_attn = _mixer(xc, zs, dt, qkv, _row(ssm_a_log[l], DT_PAD),
                               _row(jnp.repeat(ssm_d[l], SSM_HEADDIM)), _row(ssm_norm_w[l]),
                               bias_rows, attn_sinks[l].astype(F32), t["mixer"])
        h1 = _merge(y_ssm, y_attn, gates, h, w_branch_ssm[l].astype(BF16), w_branch_attn[l].astype(BF16),
                    w_mix_out[l].astype(BF16), _row(ln1_g[l]), _row(ln1_b[l]), t["merge"])
        h = _ffn(h1, w_up[l].astype(BF16), ffn_conv_w[l].astype(F32), _row(ffn_conv_b[l]),
                 w_down[l].astype(BF16), _row(ln2_g[l]), _row(ln2_b[l]), t["ffn"])
    return h.reshape(b, s, d)
```

```python
import functools
import math

import numpy as np
import jax
import jax.numpy as jnp
from jax import lax
from jax.experimental import pallas as pl
from jax.experimental.pallas import tpu as pltpu

F32 = jnp.float32
BF16 = jnp.bfloat16

D_MODEL = 1024
SSM_D_INNER = 2048
SSM_HEADDIM = 64
SSM_HEADS = 32
SSM_GROUPS = 4
SSM_HEADS_PER_GROUP = 8
SSM_STATE = 128
SSM_CONV = 4
SSM_CHUNK = 128
SSM_CONV_DIM = SSM_D_INNER + 2 * SSM_GROUPS * SSM_STATE
GROUP_WIDTH = SSM_HEADS_PER_GROUP * SSM_HEADDIM
ATTN_HEADS = 16
ATTN_KV_HEADS = 2
ATTN_HEADDIM = 64
WINDOW = 128
REL_BUCKETS = 32
REL_MAX_DIST = 128
Q_COLS = ATTN_HEADS * ATTN_HEADDIM
KV_COLS = ATTN_KV_HEADS * ATTN_HEADDIM
QKV_COLS = Q_COLS + 2 * KV_COLS
GATE_COLS = 2 * D_MODEL
D_FF = 2816
FFN_CONV = 3
DEPTH = 1
DEEPNORM_ALPHA = (2.0 * DEPTH) ** 0.25
LN_EPS = 1e-5
RMS_EPS = 1e-5

LANES = 128
SUBLANES = 8
DT_PAD = LANES
HALO_ROWS = SUBLANES
ROW_BLOCK = 128
VMEM_LIMIT = 56 * 1024 * 1024

PROJ_SEGMENTS = (SSM_D_INNER, SSM_CONV_DIM, QKV_COLS, GATE_COLS, DT_PAD)
PROJ_COLS = sum(PROJ_SEGMENTS)

NEG_BIG = -1e30


def _tiles():
    assert SSM_CHUNK == WINDOW
    return dict(inproj=256, mixer=2 * SSM_CHUNK, merge=256, ffn=512)


def _dot(a, b):
    return jnp.dot(a, b, preferred_element_type=F32)


def _dot_nt(a, b):
    return lax.dot_general(a, b, (((1,), (1,)), ((), ())), preferred_element_type=F32)


def _sigmoid(x):
    return 1.0 / (1.0 + jnp.exp(-x))


def _silu(x):
    return x * _sigmoid(x)


def _layer_norm(r, g, b):
    mu = jnp.mean(r, axis=-1, keepdims=True)
    rc = r - mu
    var = jnp.mean(rc * rc, axis=-1, keepdims=True)
    return rc * lax.rsqrt(var + LN_EPS) * g + b


def _shift_rows(x, prev, j):
    row = lax.broadcasted_iota(jnp.int32, prev.shape, 0)
    sh = pltpu.roll(x, j, 0)
    top = jnp.where(row < j, pltpu.roll(prev, j, 0), sh[:HALO_ROWS])
    return jnp.concatenate([top, sh[HALO_ROWS:]], axis=0)


def _causal_conv(x, prev, w_ref, b_ref, cols, taps):
    acc = b_ref[:, cols] + w_ref[taps - 1:taps, cols] * x
    for j in range(1, taps):
        acc = acc + w_ref[taps - 1 - j:taps - j, cols] * _shift_rows(x, prev, j)
    return acc


def _inproj_kernel(x_ref, w_ref, cw_ref, cb_ref, dtb_ref, bg_ref,
                   zs_ref, xc_ref, qkv_ref, g_ref, dt_ref, halo_ref):
    tm = x_ref.shape[0]

    @pl.when(pl.program_id(0) == 0)
    def _():
        halo_ref[...] = jnp.zeros_like(halo_ref)

    chunk = 256
    row_blocks = [slice(r, r + ROW_BLOCK) for r in range(0, tm, ROW_BLOCK)]
    xbs = [x_ref[rows, :].astype(BF16) for rows in row_blocks]
    offs = [sum(PROJ_SEGMENTS[:i]) for i in range(len(PROJ_SEGMENTS))]

    def proj(i, seg, c, w):
        return _dot(xbs[i], w_ref[:, offs[seg] + c:offs[seg] + c + w])

    for c in range(0, SSM_D_INNER, chunk):
        for i, rows in enumerate(row_blocks):
            zs_ref[rows, c:c + chunk] = _silu(proj(i, 0, c, chunk)).astype(BF16)
    for c in range(0, SSM_CONV_DIM, chunk):
        cols = slice(c, c + chunk)
        prev = halo_ref[:, cols]
        for i, rows in enumerate(row_blocks):
            u = proj(i, 1, c, chunk)
            xc_ref[rows, cols] = _silu(_causal_conv(u, prev, cw_ref, cb_ref, cols, SSM_CONV)).astype(BF16)
            prev = u[ROW_BLOCK - HALO_ROWS:, :]
        halo_ref[:, cols] = prev
    for c in range(0, QKV_COLS, chunk):
        w = min(chunk, QKV_COLS - c)
        for i, rows in enumerate(row_blocks):
            qkv_ref[rows, c:c + w] = proj(i, 2, c, w).astype(BF16)
    for c in range(0, GATE_COLS, chunk):
        for i, rows in enumerate(row_blocks):
            g_ref[rows, c:c + chunk] = _sigmoid(proj(i, 3, c, chunk) + bg_ref[:, c:c + chunk]).astype(BF16)
    lane_ok = lax.broadcasted_iota(jnp.int32, (ROW_BLOCK, DT_PAD), 1) < SSM_HEADS
    for i, rows in enumerate(row_blocks):
        dtr = proj(i, 4, 0, DT_PAD) + dtb_ref[...]
        softplus = jnp.maximum(dtr, 0.0) + jnp.log(1.0 + jnp.exp(-jnp.abs(dtr)))
        dt_ref[rows, :] = jnp.where(lane_ok, softplus, 0.0)


def _inproj(x2d, w_perm, conv_w, conv_b, dt_bias, b_gate, tm):
    s = x2d.shape[0]
    row = lambda i: (i, 0)
    const = lambda i: (0, 0)
    full = lambda a: pl.BlockSpec(a.shape, const)
    outs = [jax.ShapeDtypeStruct((s, n), dt) for n, dt in
            zip(PROJ_SEGMENTS, (BF16, BF16, BF16, BF16, F32))]
    return pl.pallas_call(
        _inproj_kernel,
        grid=(s // tm,),
        in_specs=[pl.BlockSpec((tm, D_MODEL), row),
                  pl.BlockSpec((D_MODEL, PROJ_COLS), const, pipeline_mode=pl.Buffered(1)),
                  full(conv_w), full(conv_b), full(dt_bias), full(b_gate)],
        out_specs=[pl.BlockSpec((tm, n), row) for n in PROJ_SEGMENTS],
        out_shape=outs,
        scratch_shapes=[pltpu.VMEM((HALO_ROWS, SSM_CONV_DIM), F32)],
        compiler_params=pltpu.CompilerParams(dimension_semantics=("arbitrary",),
                                             vmem_limit_bytes=VMEM_LIMIT),
        name="inproj",
    )(x2d, w_perm, conv_w, conv_b, dt_bias, b_gate)


def _split3(x):
    hi = x.astype(BF16).astype(F32)
    r = x - hi
    mid = r.astype(BF16).astype(F32)
    lo = (r - mid).astype(BF16).astype(F32)
    return hi, mid, lo


def _pack3(x):
    hi, mid, lo = _split3(x)
    return (hi + pltpu.roll(mid, SSM_HEADS, 1) + pltpu.roll(lo, 2 * SSM_HEADS, 1)).astype(BF16)


def _ssd_body(conv_ref, zs_ref, dt_ref, alog_ref, dexp_ref, nw_ref, e3_ref, ec3_ref, y_ref, state_ref):
    L = SSM_CHUNK
    lane_ok = lax.broadcasted_iota(jnp.int32, (L, DT_PAD), 1) < SSM_HEADS
    dt = dt_ref[...]
    a_dt = dt * (-jnp.exp(alog_ref[...]))

    ri = lax.broadcasted_iota(jnp.int32, (L, L), 0)
    ci = lax.broadcasted_iota(jnp.int32, (L, L), 1)
    causal = ri >= ci
    tril = jnp.where(causal, 1.0, 0.0).astype(BF16)
    hi, mid, lo = _split3(a_dt)
    a_cs = _dot(tril, hi.astype(BF16)) + _dot(tril, mid.astype(BF16)) + _dot(tril, lo.astype(BF16))

    a_last = a_cs[L - 1:L, :]
    decay_out_b = jnp.where(lane_ok, jnp.exp(a_cs), 0.0).astype(BF16)
    w_state_b = (dt * jnp.exp(a_last - a_cs)).astype(BF16)
    cd_e = jnp.exp(_dot(_pack3(a_cs[L - SUBLANES:, :]), e3_ref[...])[SUBLANES - 1:, :])

    a_sub_t = (a_cs - jnp.log(dt)).T
    acs3 = _pack3(a_cs)
    lane_lo = lax.broadcasted_iota(jnp.int32, (L, LANES), 1) < SSM_HEADDIM

    def group_start(g):
        gsl = slice(g * GROUP_WIDTH, (g + 1) * GROUP_WIDTH)
        xs_g = conv_ref[:, gsl].astype(F32)
        b_off = SSM_D_INNER + g * SSM_STATE
        c_off = SSM_D_INNER + SSM_GROUPS * SSM_STATE + g * SSM_STATE
        b_g = conv_ref[:, b_off:b_off + SSM_STATE]
        c_g = conv_ref[:, c_off:c_off + SSM_STATE]
        cb = _dot_nt(c_g, b_g)
        xdtd_b = (xs_g * _dot(w_state_b, e3_ref[:, gsl])).astype(BF16)
        st = state_ref[g]
        y_off = _dot(c_g, st.astype(BF16)) * _dot(decay_out_b, e3_ref[:, gsl])
        state_ref[g] = st * cd_e[:, gsl] + _dot(b_g.astype(F32).T.astype(BF16), xdtd_b)
        return cb, y_off + xs_g * dexp_ref[:, gsl]

    def acol_dot(p):
        return _dot(acs3, ec3_ref[:, 2 * p * L:(2 * p + 2) * L])

    pairs_per_group = SSM_HEADS_PER_GROUP // 2
    n_pairs = SSM_HEADS // 2
    started = {0: group_start(0)}
    acols = {0: acol_dot(0)}
    pieces = []
    for p in range(n_pairs):
        g, pr = divmod(p, pairs_per_group)
        if p + 1 < n_pairs:
            acols[p + 1] = acol_dot(p + 1)
        if pr == pairs_per_group - 2 and g + 1 < SSM_GROUPS:
            started[g + 1] = group_start(g + 1)
        cb, y_rest = started[g]
        acol = acols.pop(p)
        ms = []
        for k in range(2):
            seg = acol[:, k * L:(k + 1) * L] - a_sub_t[2 * p + k:2 * p + k + 1, :]
            ms.append((cb * jnp.exp(jnp.where(causal, seg, NEG_BIG))).astype(BF16))
        xp = conv_ref[:, p * LANES:(p + 1) * LANES]
        zero = jnp.zeros_like(xp)
        rhs = jnp.concatenate([jnp.where(lane_lo, xp, zero), jnp.where(lane_lo, zero, xp)], axis=0)
        pieces.append(_dot(jnp.concatenate(ms, axis=1), rhs))
        if pr == pairs_per_group - 1:
            gsl = slice(g * GROUP_WIDTH, (g + 1) * GROUP_WIDTH)
            yz = (jnp.concatenate(pieces, axis=1) + y_rest) * zs_ref[:, gsl].astype(F32)
            ms_ = jnp.mean(yz * yz, axis=-1, keepdims=True)
            y_ref[:, gsl] = (yz * lax.rsqrt(ms_ + RMS_EPS) * nw_ref[:, gsl]).astype(BF16)
            pieces = []
            del started[g]
        if p % 2 == 1:
            yield


def _expansion_matrices():
    k = np.arange(LANES)[:, None]
    valid = k < 3 * SSM_HEADS
    head = k % SSM_HEADS
    e3 = (valid & (head == (np.arange(SSM_D_INNER)[None, :] // SSM_HEADDIM))).astype(np.float32)
    ec3 = (valid & (head == (np.arange(SSM_HEADS * SSM_CHUNK)[None, :] // SSM_CHUNK))).astype(np.float32)
    return jnp.asarray(e3, BF16), jnp.asarray(ec3, BF16)


def _build_bias_mask(ur_ref, bm_ref):
    W = WINDOW
    first = lax.broadcasted_iota(jnp.int32, (W, 2 * W), 1) >= W
    for h in range(ATTN_HEADS):
        band_h = pltpu.roll(jnp.broadcast_to(ur_ref[h:h + 1, :], (W, 2 * W)), W, 1,
                            stride=1, stride_axis=0)
        bm_ref[1, h] = band_h
        bm_ref[0, h] = jnp.where(first, band_h, -jnp.inf)


def _attn_body(sink_ref, q_ref, kvc_ref, kvp_ref, o_ref, bm_ref, slab):
    W = WINDOW
    lane_lo2 = lax.broadcasted_iota(jnp.int32, (2 * W, LANES), 1) < ATTN_HEADDIM
    lane_lo = lax.broadcasted_iota(jnp.int32, (W, LANES), 1) < ATTN_HEADDIM

    def band(lo_col):
        return jnp.concatenate([kvp_ref[:, lo_col:lo_col + KV_COLS].astype(F32),
                                kvc_ref[:, lo_col:lo_col + KV_COLS].astype(F32)], axis=0)

    kband = band(0) * (ATTN_HEADDIM ** -0.5)
    vband = band(KV_COLS)
    ksw = pltpu.roll(kband, ATTN_HEADDIM, 1)
    vsw = pltpu.roll(vband, ATTN_HEADDIM, 1)
    k_var = ((jnp.where(lane_lo2, kband, 0.0).astype(BF16), jnp.where(lane_lo2, 0.0, ksw).astype(BF16)),
             (jnp.where(lane_lo2, ksw, 0.0).astype(BF16), jnp.where(lane_lo2, 0.0, kband).astype(BF16)))
    v_bd = (jnp.concatenate([jnp.where(lane_lo2, vband, 0.0), jnp.where(lane_lo2, 0.0, vsw)], axis=0).astype(BF16),
            jnp.concatenate([jnp.where(lane_lo2, vsw, 0.0), jnp.where(lane_lo2, 0.0, vband)], axis=0).astype(BF16))

    def kv_head(i):
        return (2 * i) // (ATTN_HEADS // ATTN_KV_HEADS)

    def logits(i):
        qp = q_ref[:, i * LANES:(i + 1) * LANES]
        return [_dot_nt(qp, k_var[kv_head(i)][j]) for j in range(2)]

    n_pairs = ATTN_HEADS // 2
    s_next = logits(0)
    for i in range(n_pairs):
        c = kv_head(i)
        s_cur = s_next
        if i + 1 < n_pairs:
            s_next = logits(i + 1)
        ps, rs = [], []
        for j in range(2):
            h = 2 * i + j
            sink = sink_ref[h]
            s = s_cur[j] + bm_ref[slab, h]
            m = jnp.maximum(jnp.max(s, axis=-1, keepdims=True), sink)
            p = jnp.exp(s - m)
            den = jnp.sum(p, axis=-1, keepdims=True) + jnp.exp(sink - m)
            ps.append(p.astype(BF16))
            rs.append(1.0 / den)
        pv = _dot(jnp.concatenate(ps, axis=1), v_bd[c])
        o_ref[:, i * LANES:(i + 1) * LANES] = (pv * jnp.where(lane_lo, rs[0], rs[1])).astype(BF16)
        yield


def _mixer_kernel(sink_ref, conv_ref, zs_ref, dt_ref, alog_ref, dexp_ref, nw_ref, e3_ref, ec3_ref,
                  q_ref, kvc_ref, kvp_ref, ur_ref, ys_ref, ya_ref, state_ref, bm_ref):
    @pl.when(pl.program_id(0) == 0)
    def _():
        state_ref[...] = jnp.zeros_like(state_ref)
        _build_bias_mask(ur_ref, bm_ref)

    L = SSM_CHUNK
    live = []
    for blk in range(conv_ref.shape[0] // L):
        rows = pl.ds(blk * L, L)
        live.append(_ssd_body(conv_ref.at[rows], zs_ref.at[rows], dt_ref.at[rows], alog_ref, dexp_ref, nw_ref,
                              e3_ref, ec3_ref, ys_ref.at[rows], state_ref))
        prev = kvp_ref if blk == 0 else kvc_ref.at[pl.ds((blk - 1) * L, L)]
        slab = jnp.minimum(pl.program_id(0), 1) if blk == 0 else 1
        live.append(_attn_body(sink_ref, q_ref.at[rows], kvc_ref.at[rows], prev, ya_ref.at[rows], bm_ref, slab))
    while live:
        for body in list(live):
            if next(body, "done") == "done":
                live.remove(body)


def _mixer(xc, zs, dt, qkv, a_log, d_exp, norm_w, bias_rows, sinks, tm):
    s = xc.shape[0]
    e3, ec3 = _expansion_matrices()
    row = lambda i: (i, 0)
    const = lambda i: (0, 0)
    full = lambda a: pl.BlockSpec(a.shape, const)
    kv_blk = Q_COLS // (2 * KV_COLS)
    return pl.pallas_call(
        _mixer_kernel,
        grid=(s // tm,),
        in_specs=[pl.BlockSpec(memory_space=pltpu.SMEM),
                  pl.BlockSpec((tm, SSM_CONV_DIM), row),
                  pl.BlockSpec((tm, SSM_D_INNER), row),
                  pl.BlockSpec((tm, DT_PAD), row),
                  full(a_log), full(d_exp), full(norm_w), full(e3), full(ec3),
                  pl.BlockSpec((tm, Q_COLS), row),
                  pl.BlockSpec((tm, 2 * KV_COLS), lambda i: (i, kv_blk)),
                  pl.BlockSpec((WINDOW, 2 * KV_COLS),
                               lambda i: (jnp.maximum(i * (tm // WINDOW) - 1, 0), kv_blk)),
                  full(bias_rows)],
        out_specs=[pl.BlockSpec((tm, SSM_D_INNER), row), pl.BlockSpec((tm, Q_COLS), row)],
        out_shape=[jax.ShapeDtypeStruct((s, SSM_D_INNER), BF16), jax.ShapeDtypeStruct((s, Q_COLS), BF16)],
        scratch_shapes=[pltpu.VMEM((SSM_GROUPS, SSM_STATE, GROUP_WIDTH), F32),
                        pltpu.VMEM((2, ATTN_HEADS, WINDOW, 2 * WINDOW), F32)],
        compiler_params=pltpu.CompilerParams(dimension_semantics=("arbitrary",),
                                             vmem_limit_bytes=VMEM_LIMIT),
        name="mixer",
    )(sinks, xc, zs, dt, a_log, d_exp, norm_w, e3, ec3, qkv, qkv, qkv, bias_rows)


def _rel_bucket_static(n):
    max_exact = REL_BUCKETS // 2
    nf = np.maximum(n, 1).astype(np.float32)
    large = max_exact + (np.log(nf / max_exact) / math.log(REL_MAX_DIST / max_exact)
                         * (REL_BUCKETS - max_exact)).astype(np.int32)
    return np.where(n < max_exact, n, np.minimum(large, REL_BUCKETS - 1))


def _bias_rows(rel_bias):
    rel = (-np.arange(2 * WINDOW)) % (2 * WINDOW)
    idx = np.where(rel < WINDOW, _rel_bucket_static(rel), REL_BUCKETS)
    table = jnp.concatenate([rel_bias.astype(F32), jnp.full((1, ATTN_HEADS), -jnp.inf, F32)], axis=0)
    return table[idx].T


def _merge_kernel(ys_ref, ya_ref, g_ref, x_ref, wbs_ref, wba_ref, wmix_ref, lg_ref, lb_ref, h_ref):
    chunk = 256
    rb = 2 * ROW_BLOCK
    for r in range(0, ys_ref.shape[0], rb):
        rows = slice(r, r + rb)
        ys, ya = ys_ref[rows, :], ya_ref[rows, :]
        merged = []
        for c in range(0, D_MODEL, chunk):
            a = _dot(ys, wbs_ref[:, c:c + chunk])
            b = _dot(ya, wba_ref[:, c:c + chunk])
            merged.append((g_ref[rows, c:c + chunk].astype(F32) * a
                           + g_ref[rows, D_MODEL + c:D_MODEL + c + chunk].astype(F32) * b).astype(BF16))
        mix = _dot(jnp.concatenate(merged, axis=1), wmix_ref[...])
        h_ref[rows, :] = _layer_norm(DEEPNORM_ALPHA * x_ref[rows, :] + mix, lg_ref[...], lb_ref[...])


def _merge(y_ssm, y_attn, gates, x2d, w_bs, w_ba, w_mix, ln_g, ln_b, tm):
    s = x2d.shape[0]
    row = lambda i: (i, 0)
    const = lambda i: (0, 0)
    full = lambda a: pl.BlockSpec(a.shape, const)
    return pl.pallas_call(
        _merge_kernel,
        grid=(s // tm,),
        in_specs=[pl.BlockSpec((tm, SSM_D_INNER), row), pl.BlockSpec((tm, Q_COLS), row),
                  pl.BlockSpec((tm, GATE_COLS), row), pl.BlockSpec((tm, D_MODEL), row),
                  full(w_bs), full(w_ba), full(w_mix), full(ln_g), full(ln_b)],
        out_specs=pl.BlockSpec((tm, D_MODEL), row),
        out_shape=jax.ShapeDtypeStruct((s, D_MODEL), F32),
        compiler_params=pltpu.CompilerParams(dimension_semantics=("arbitrary",),
                                             vmem_limit_bytes=VMEM_LIMIT),
        name="merge",
    )(y_ssm, y_attn, gates, x2d, w_bs, w_ba, w_mix, ln_g, ln_b)


def _ffn_kernel(h_ref, wup_ref, cw_ref, cb_ref, wdn_ref, lg_ref, lb_ref, o_ref, halo_ref, act_ref):
    tm = h_ref.shape[0]

    @pl.when(pl.program_id(0) == 0)
    def _():
        halo_ref[...] = jnp.zeros_like(halo_ref)

    chunk = 256
    rb = 2 * ROW_BLOCK
    row_blocks = [slice(r, r + rb) for r in range(0, tm, rb)]
    hbs = [h_ref[rows, :].astype(BF16) for rows in row_blocks]

    def conv_cols(hb, cols, prev):
        u = _dot(hb, wup_ref[:, cols])
        return _causal_conv(u, prev, cw_ref, cb_ref, cols, FFN_CONV), u[rb - HALO_ROWS:, :]

    for c in range(0, D_FF, chunk):
        gcols, vcols = slice(c, c + chunk), slice(D_FF + c, D_FF + c + chunk)
        gprev, vprev = halo_ref[:, gcols], halo_ref[:, vcols]
        for hb, rows in zip(hbs, row_blocks):
            gate, gprev = conv_cols(hb, gcols, gprev)
            val, vprev = conv_cols(hb, vcols, vprev)
            act_ref[rows, c:c + chunk] = (_silu(gate) * val).astype(BF16)
        halo_ref[:, gcols] = gprev
        halo_ref[:, vcols] = vprev

    for rows in row_blocks:
        out = _dot(act_ref[rows, :], wdn_ref[...])
        o_ref[rows, :] = _layer_norm(DEEPNORM_ALPHA * h_ref[rows, :] + out, lg_ref[...], lb_ref[...])


def _ffn(h1, w_up, conv_w, conv_b, w_down, ln_g, ln_b, tm):
    s = h1.shape[0]
    row = lambda i: (i, 0)
    const = lambda i: (0, 0)
    full = lambda a: pl.BlockSpec(a.shape, const)
    return pl.pallas_call(
        _ffn_kernel,
        grid=(s // tm,),
        in_specs=[pl.BlockSpec((tm, D_MODEL), row),
                  pl.BlockSpec(w_up.shape, const, pipeline_mode=pl.Buffered(1)),
                  full(conv_w), full(conv_b),
                  pl.BlockSpec(w_down.shape, const, pipeline_mode=pl.Buffered(1)),
                  full(ln_g), full(ln_b)],
        out_specs=pl.BlockSpec((tm, D_MODEL), row),
        out_shape=jax.ShapeDtypeStruct((s, D_MODEL), F32),
        scratch_shapes=[pltpu.VMEM((HALO_ROWS, 2 * D_FF), F32),
                        pltpu.VMEM((tm, D_FF), BF16)],
        compiler_params=pltpu.CompilerParams(dimension_semantics=("arbitrary",),
                                             vmem_limit_bytes=VMEM_LIMIT),
        name="ffn",
    )(h1, w_up, conv_w, conv_b, w_down, ln_g, ln_b)


def _permute_w_in(w):
    dt_lo = SSM_D_INNER + SSM_CONV_DIM
    dt_hi = dt_lo + SSM_HEADS
    pad = jnp.zeros((w.shape[0], DT_PAD - SSM_HEADS), w.dtype)
    return jnp.concatenate([w[:, :dt_lo], w[:, dt_hi:], w[:, dt_lo:dt_hi], pad], axis=1).astype(BF16)


def _row(v, width=None):
    v = v.astype(F32).reshape(1, -1)
    if width is not None and v.shape[1] < width:
        v = jnp.pad(v, ((0, 0), (0, width - v.shape[1])))
    return v


def kernel(x, rel_bias, w_in, b_gate, ssm_conv_w, ssm_conv_b, ssm_dt_bias, ssm_a_log, ssm_d, ssm_norm_w,
           attn_sinks, w_branch_ssm, w_branch_attn, w_mix_out, ln1_g, ln1_b, w_up, ffn_conv_w, ffn_conv_b,
           w_down, ln2_g, ln2_b):
    b, s, d = x.shape
    assert (b, d) == (1, D_MODEL) and s % 512 == 0 and w_in.shape[0] == DEPTH
    t = _tiles()
    h = x.reshape(s, d)
    bias_rows = _bias_rows(rel_bias)
    for l in range(DEPTH):
        zs, xc, qkv, gates, dt = _inproj(h, _permute_w_in(w_in[l]), ssm_conv_w[l].astype(F32),
                                         _row(ssm_conv_b[l]), _row(ssm_dt_bias[l], DT_PAD),
                                         _row(b_gate[l]), t["inproj"])
        y_ssm, y_attn = _mixer(xc, zs, dt, qkv, _row(ssm_a_log[l], DT_PAD),
                               _row(jnp.repeat(ssm_d[l], SSM_HEADDIM)), _row(ssm_norm_w[l]),
                               bias_rows, attn_sinks[l].astype(F32), t["mixer"])
        h1 = _merge(y_ssm, y_attn, gates, h, w_branch_ssm[l].astype(BF16), w_branch_attn[l].astype(BF16),
                    w_mix_out[l].astype(BF16), _row(ln1_g[l]), _row(ln1_b[l]), t["merge"])
        h = _ffn(h1, w_up[l].astype(BF16), ffn_conv_w[l].astype(F32), _row(ffn_conv_b[l]),
                 w_down[l].astype(BF16), _row(ln2_g[l]), _row(ln2_b[l]), t["ffn"])
    return h.reshape(b, s, d)
```

```python
import functools
import math

import numpy as np
import jax
import jax.numpy as jnp
from jax import lax
from jax.experimental import pallas as pl
from jax.experimental.pallas import tpu as pltpu

F32 = jnp.float32
BF16 = jnp.bfloat16

D_MODEL = 1024
SSM_D_INNER = 2048
SSM_HEADDIM = 64
SSM_HEADS = 32
SSM_GROUPS = 4
SSM_HEADS_PER_GROUP = 8
SSM_STATE = 128
SSM_CONV = 4
SSM_CHUNK = 128
SSM_CONV_DIM = SSM_D_INNER + 2 * SSM_GROUPS * SSM_STATE
GROUP_WIDTH = SSM_HEADS_PER_GROUP * SSM_HEADDIM
ATTN_HEADS = 16
ATTN_KV_HEADS = 2
ATTN_HEADDIM = 64
WINDOW = 128
REL_BUCKETS = 32
REL_MAX_DIST = 128
Q_COLS = ATTN_HEADS * ATTN_HEADDIM
KV_COLS = ATTN_KV_HEADS * ATTN_HEADDIM
QKV_COLS = Q_COLS + 2 * KV_COLS
GATE_COLS = 2 * D_MODEL
D_FF = 2816
FFN_CONV = 3
DEPTH = 1
DEEPNORM_ALPHA = (2.0 * DEPTH) ** 0.25
LN_EPS = 1e-5
RMS_EPS = 1e-5

LANES = 128
SUBLANES = 8
DT_PAD = LANES
HALO_ROWS = SUBLANES
SSD_LOOKAHEAD = 4
ATTN_LOOKAHEAD = 2
ROW_BLOCK = 128
VMEM_LIMIT = 56 * 1024 * 1024

PROJ_SEGMENTS = (SSM_D_INNER, SSM_CONV_DIM, QKV_COLS, GATE_COLS, DT_PAD)
PROJ_COLS = sum(PROJ_SEGMENTS)

NEG_BIG = -1e30


def _tiles():
    assert SSM_CHUNK == WINDOW
    return dict(inproj=256, mixer=2 * SSM_CHUNK, merge=256, ffn=512)


def _dot(a, b):
    return jnp.dot(a, b, preferred_element_type=F32)


def _dot_nt(a, b):
    return lax.dot_general(a, b, (((1,), (1,)), ((), ())), preferred_element_type=F32)


def _sigmoid(x):
    return 1.0 / (1.0 + jnp.exp(-x))


def _silu(x):
    return x * _sigmoid(x)


def _layer_norm(r, g, b):
    mu = jnp.mean(r, axis=-1, keepdims=True)
    rc = r - mu
    var = jnp.mean(rc * rc, axis=-1, keepdims=True)
    return rc * lax.rsqrt(var + LN_EPS) * g + b


def _shift_rows(x, prev, j):
    row = lax.broadcasted_iota(jnp.int32, prev.shape, 0)
    sh = pltpu.roll(x, j, 0)
    top = jnp.where(row < j, pltpu.roll(prev, j, 0), sh[:HALO_ROWS])
    return jnp.concatenate([top, sh[HALO_ROWS:]], axis=0)


def _causal_conv(x, prev, w_ref, b_ref, cols, taps):
    acc = b_ref[:, cols] + w_ref[taps - 1:taps, cols] * x
    for j in range(1, taps):
        acc = acc + w_ref[taps - 1 - j:taps - j, cols] * _shift_rows(x, prev, j)
    return acc


def _inproj_kernel(x_ref, w_ref, cw_ref, cb_ref, dtb_ref, bg_ref,
                   zs_ref, xc_ref, qkv_ref, g_ref, dt_ref, halo_ref):
    tm = x_ref.shape[0]

    @pl.when(pl.program_id(0) == 0)
    def _():
        halo_ref[...] = jnp.zeros_like(halo_ref)

    chunk = 256
    row_blocks = [slice(r, r + ROW_BLOCK) for r in range(0, tm, ROW_BLOCK)]
    xbs = [x_ref[rows, :].astype(BF16) for rows in row_blocks]
    offs = [sum(PROJ_SEGMENTS[:i]) for i in range(len(PROJ_SEGMENTS))]

    def proj(i, seg, c, w):
        return _dot(xbs[i], w_ref[:, offs[seg] + c:offs[seg] + c + w])

    for c in range(0, SSM_D_INNER, chunk):
        for i, rows in enumerate(row_blocks):
            zs_ref[rows, c:c + chunk] = _silu(proj(i, 0, c, chunk)).astype(BF16)
    for c in range(0, SSM_CONV_DIM, chunk):
        cols = slice(c, c + chunk)
        prev = halo_ref[:, cols]
        for i, rows in enumerate(row_blocks):
            u = proj(i, 1, c, chunk)
            xc_ref[rows, cols] = _silu(_causal_conv(u, prev, cw_ref, cb_ref, cols, SSM_CONV)).astype(BF16)
            prev = u[ROW_BLOCK - HALO_ROWS:, :]
        halo_ref[:, cols] = prev
    for c in range(0, QKV_COLS, chunk):
        w = min(chunk, QKV_COLS - c)
        for i, rows in enumerate(row_blocks):
            qkv_ref[rows, c:c + w] = proj(i, 2, c, w).astype(BF16)
    for c in range(0, GATE_COLS, chunk):
        for i, rows in enumerate(row_blocks):
            g_ref[rows, c:c + chunk] = _sigmoid(proj(i, 3, c, chunk) + bg_ref[:, c:c + chunk]).astype(BF16)
    lane_ok = lax.broadcasted_iota(jnp.int32, (ROW_BLOCK, DT_PAD), 1) < SSM_HEADS
    for i, rows in enumerate(row_blocks):
        dtr = proj(i, 4, 0, DT_PAD) + dtb_ref[...]
        softplus = jnp.maximum(dtr, 0.0) + jnp.log(1.0 + jnp.exp(-jnp.abs(dtr)))
        dt_ref[rows, :] = jnp.where(lane_ok, softplus, 0.0)


def _inproj(x2d, w_perm, conv_w, conv_b, dt_bias, b_gate, tm):
    s = x2d.shape[0]
    row = lambda i: (i, 0)
    const = lambda i: (0, 0)
    full = lambda a: pl.BlockSpec(a.shape, const)
    outs = [jax.ShapeDtypeStruct((s, n), dt) for n, dt in
            zip(PROJ_SEGMENTS, (BF16, BF16, BF16, BF16, F32))]
    return pl.pallas_call(
        _inproj_kernel,
        grid=(s // tm,),
        in_specs=[pl.BlockSpec((tm, D_MODEL), row),
                  pl.BlockSpec((D_MODEL, PROJ_COLS), const, pipeline_mode=pl.Buffered(1)),
                  full(conv_w), full(conv_b), full(dt_bias), full(b_gate)],
        out_specs=[pl.BlockSpec((tm, n), row) for n in PROJ_SEGMENTS],
        out_shape=outs,
        scratch_shapes=[pltpu.VMEM((HALO_ROWS, SSM_CONV_DIM), F32)],
        compiler_params=pltpu.CompilerParams(dimension_semantics=("arbitrary",),
                                             vmem_limit_bytes=VMEM_LIMIT),
        name="inproj",
    )(x2d, w_perm, conv_w, conv_b, dt_bias, b_gate)


def _split3(x):
    hi = x.astype(BF16).astype(F32)
    r = x - hi
    mid = r.astype(BF16).astype(F32)
    lo = (r - mid).astype(BF16).astype(F32)
    return hi, mid, lo


def _pack3(x):
    hi, mid, lo = _split3(x)
    return (hi + pltpu.roll(mid, SSM_HEADS, 1) + pltpu.roll(lo, 2 * SSM_HEADS, 1)).astype(BF16)


def _ssd_body(conv_ref, zs_ref, dt_ref, alog_ref, dexp_ref, nw_ref, e3_ref, ec3_ref, y_ref, state_ref):
    L = SSM_CHUNK
    lane_ok = lax.broadcasted_iota(jnp.int32, (L, DT_PAD), 1) < SSM_HEADS
    dt = dt_ref[...]
    a_dt = dt * (-jnp.exp(alog_ref[...]))

    ri = lax.broadcasted_iota(jnp.int32, (L, L), 0)
    ci = lax.broadcasted_iota(jnp.int32, (L, L), 1)
    causal = ri >= ci
    tril = jnp.where(causal, 1.0, 0.0).astype(BF16)
    hi, mid, lo = _split3(a_dt)
    a_cs = _dot(tril, hi.astype(BF16)) + _dot(tril, mid.astype(BF16)) + _dot(tril, lo.astype(BF16))

    a_last = a_cs[L - 1:L, :]
    decay_out_b = jnp.where(lane_ok, jnp.exp(a_cs), 0.0).astype(BF16)
    w_state_b = (dt * jnp.exp(a_last - a_cs)).astype(BF16)
    cd_e = jnp.exp(_dot(_pack3(a_cs[L - SUBLANES:, :]), e3_ref[...])[SUBLANES - 1:, :])

    a_sub_t = (a_cs - jnp.log(dt)).T
    acs3 = _pack3(a_cs)
    lane_lo = lax.broadcasted_iota(jnp.int32, (L, LANES), 1) < SSM_HEADDIM

    def group_start(g):
        gsl = slice(g * GROUP_WIDTH, (g + 1) * GROUP_WIDTH)
        xs_g = conv_ref[:, gsl].astype(F32)
        b_off = SSM_D_INNER + g * SSM_STATE
        c_off = SSM_D_INNER + SSM_GROUPS * SSM_STATE + g * SSM_STATE
        b_g = conv_ref[:, b_off:b_off + SSM_STATE]
        c_g = conv_ref[:, c_off:c_off + SSM_STATE]
        cb = _dot_nt(c_g, b_g)
        xdtd_b = (xs_g * _dot(w_state_b, e3_ref[:, gsl])).astype(BF16)
        st = state_ref[g]
        y_off = _dot(c_g, st.astype(BF16)) * _dot(decay_out_b, e3_ref[:, gsl])
        state_ref[g] = st * cd_e[:, gsl] + _dot(b_g.astype(F32).T.astype(BF16), xdtd_b)
        return cb, y_off + xs_g * dexp_ref[:, gsl]

    def acol_dot(p):
        return _dot(acs3, ec3_ref[:, 2 * p * L:(2 * p + 2) * L])

    pairs_per_group = SSM_HEADS_PER_GROUP // 2
    n_pairs = SSM_HEADS // 2
    started = {0: group_start(0)}
    acols = {p: acol_dot(p) for p in range(SSD_LOOKAHEAD)}
    pieces = []
    for p in range(n_pairs):
        g, pr = divmod(p, pairs_per_group)
        if p + SSD_LOOKAHEAD < n_pairs:
            acols[p + SSD_LOOKAHEAD] = acol_dot(p + SSD_LOOKAHEAD)
        if pr == pairs_per_group - 2 and g + 1 < SSM_GROUPS:
            started[g + 1] = group_start(g + 1)
        cb, y_rest = started[g]
        acol = acols.pop(p)
        ms = []
        for k in range(2):
            seg = acol[:, k * L:(k + 1) * L] - a_sub_t[2 * p + k:2 * p + k + 1, :]
            ms.append((cb * jnp.exp(jnp.where(causal, seg, NEG_BIG))).astype(BF16))
        xp = conv_ref[:, p * LANES:(p + 1) * LANES]
        zero = jnp.zeros_like(xp)
        rhs = jnp.concatenate([jnp.where(lane_lo, xp, zero), jnp.where(lane_lo, zero, xp)], axis=0)
        pieces.append(_dot(jnp.concatenate(ms, axis=1), rhs))
        if pr == pairs_per_group - 1:
            gsl = slice(g * GROUP_WIDTH, (g + 1) * GROUP_WIDTH)
            yz = (jnp.concatenate(pieces, axis=1) + y_rest) * zs_ref[:, gsl].astype(F32)
            ms_ = jnp.mean(yz * yz, axis=-1, keepdims=True)
            y_ref[:, gsl] = (yz * lax.rsqrt(ms_ + RMS_EPS) * nw_ref[:, gsl]).astype(BF16)
            pieces = []
            del started[g]
        if p % 2 == 1:
            yield


def _expansion_matrices():
    k = np.arange(LANES)[:, None]
    valid = k < 3 * SSM_HEADS
    head = k % SSM_HEADS
    e3 = (valid & (head == (np.arange(SSM_D_INNER)[None, :] // SSM_HEADDIM))).astype(np.float32)
    ec3 = (valid & (head == (np.arange(SSM_HEADS * SSM_CHUNK)[None, :] // SSM_CHUNK))).astype(np.float32)
    return jnp.asarray(e3, BF16), jnp.asarray(ec3, BF16)


def _build_bias_mask(ur_ref, bm_ref):
    W = WINDOW
    first = lax.broadcasted_iota(jnp.int32, (W, 2 * W), 1) >= W
    for h in range(ATTN_HEADS):
        band_h = pltpu.roll(jnp.broadcast_to(ur_ref[h:h + 1, :], (W, 2 * W)), W, 1,
                            stride=1, stride_axis=0)
        bm_ref[1, h] = band_h
        bm_ref[0, h] = jnp.where(first, band_h, -jnp.inf)


def _attn_body(sink_ref, q_ref, kvc_ref, kvp_ref, o_ref, bm_ref, slab):
    W = WINDOW
    lane_lo2 = lax.broadcasted_iota(jnp.int32, (2 * W, LANES), 1) < ATTN_HEADDIM
    lane_lo = lax.broadcasted_iota(jnp.int32, (W, LANES), 1) < ATTN_HEADDIM

    def band(lo_col):
        return jnp.concatenate([kvp_ref[:, lo_col:lo_col + KV_COLS].astype(F32),
                                kvc_ref[:, lo_col:lo_col + KV_COLS].astype(F32)], axis=0)

    kband = band(0) * (ATTN_HEADDIM ** -0.5)
    vband = band(KV_COLS)
    ksw = pltpu.roll(kband, ATTN_HEADDIM, 1)
    vsw = pltpu.roll(vband, ATTN_HEADDIM, 1)
    k_var = ((jnp.where(lane_lo2, kband, 0.0).astype(BF16), jnp.where(lane_lo2, 0.0, ksw).astype(BF16)),
             (jnp.where(lane_lo2, ksw, 0.0).astype(BF16), jnp.where(lane_lo2, 0.0, kband).astype(BF16)))
    v_bd = (jnp.concatenate([jnp.where(lane_lo2, vband, 0.0), jnp.where(lane_lo2, 0.0, vsw)], axis=0).astype(BF16),
            jnp.concatenate([jnp.where(lane_lo2, vsw, 0.0), jnp.where(lane_lo2, 0.0, vband)], axis=0).astype(BF16))

    def kv_head(i):
        return (2 * i) // (ATTN_HEADS // ATTN_KV_HEADS)

    def logits(i):
        qp = q_ref[:, i * LANES:(i + 1) * LANES]
        return [_dot_nt(qp, k_var[kv_head(i)][j]) for j in range(2)]

    n_pairs = ATTN_HEADS // 2
    pending = {i: logits(i) for i in range(ATTN_LOOKAHEAD)}
    for i in range(n_pairs):
        c = kv_head(i)
        if i + ATTN_LOOKAHEAD < n_pairs:
            pending[i + ATTN_LOOKAHEAD] = logits(i + ATTN_LOOKAHEAD)
        s_cur = pending.pop(i)
        ps, rs = [], []
        for j in range(2):
            h = 2 * i + j
            sink = sink_ref[h]
            s = s_cur[j] + bm_ref[slab, h]
            m = jnp.maximum(jnp.max(s, axis=-1, keepdims=True), sink)
            p = jnp.exp(s - m)
            den = jnp.sum(p, axis=-1, keepdims=True) + jnp.exp(sink - m)
            ps.append(p.astype(BF16))
            rs.append(1.0 / den)
        pv = _dot(jnp.concatenate(ps, axis=1), v_bd[c])
        o_ref[:, i * LANES:(i + 1) * LANES] = (pv * jnp.where(lane_lo, rs[0], rs[1])).astype(BF16)
        yield


def _mixer_kernel(sink_ref, conv_ref, zs_ref, dt_ref, alog_ref, dexp_ref, nw_ref, e3_ref, ec3_ref,
                  q_ref, kvc_ref, kvp_ref, ur_ref, ys_ref, ya_ref, state_ref, bm_ref):
    @pl.when(pl.program_id(0) == 0)
    def _():
        state_ref[...] = jnp.zeros_like(state_ref)
        _build_bias_mask(ur_ref, bm_ref)

    L = SSM_CHUNK
    live = []
    for blk in range(conv_ref.shape[0] // L):
        rows = pl.ds(blk * L, L)
        live.append(_ssd_body(conv_ref.at[rows], zs_ref.at[rows], dt_ref.at[rows], alog_ref, dexp_ref, nw_ref,
                              e3_ref, ec3_ref, ys_ref.at[rows], state_ref))
        prev = kvp_ref if blk == 0 else kvc_ref.at[pl.ds((blk - 1) * L, L)]
        slab = jnp.minimum(pl.program_id(0), 1) if blk == 0 else 1
        live.append(_attn_body(sink_ref, q_ref.at[rows], kvc_ref.at[rows], prev, ya_ref.at[rows], bm_ref, slab))
    while live:
        for body in list(live):
            if next(body, "done") == "done":
                live.remove(body)


def _mixer(xc, zs, dt, qkv, a_log, d_exp, norm_w, bias_rows, sinks, tm):
    s = xc.shape[0]
    e3, ec3 = _expansion_matrices()
    row = lambda i: (i, 0)
    const = lambda i: (0, 0)
    full = lambda a: pl.BlockSpec(a.shape, const)
    kv_blk = Q_COLS // (2 * KV_COLS)
    return pl.pallas_call(
        _mixer_kernel,
        grid=(s // tm,),
        in_specs=[pl.BlockSpec(memory_space=pltpu.SMEM),
                  pl.BlockSpec((tm, SSM_CONV_DIM), row),
                  pl.BlockSpec((tm, SSM_D_INNER), row),
                  pl.BlockSpec((tm, DT_PAD), row),
                  full(a_log), full(d_exp), full(norm_w), full(e3), full(ec3),
                  pl.BlockSpec((tm, Q_COLS), row),
                  pl.BlockSpec((tm, 2 * KV_COLS), lambda i: (i, kv_blk)),
                  pl.BlockSpec((WINDOW, 2 * KV_COLS),
                               lambda i: (jnp.maximum(i * (tm // WINDOW) - 1, 0), kv_blk)),
                  full(bias_rows)],
        out_specs=[pl.BlockSpec((tm, SSM_D_INNER), row), pl.BlockSpec((tm, Q_COLS), row)],
        out_shape=[jax.ShapeDtypeStruct((s, SSM_D_INNER), BF16), jax.ShapeDtypeStruct((s, Q_COLS), BF16)],
        scratch_shapes=[pltpu.VMEM((SSM_GROUPS, SSM_STATE, GROUP_WIDTH), F32),
                        pltpu.VMEM((2, ATTN_HEADS, WINDOW, 2 * WINDOW), F32)],
        compiler_params=pltpu.CompilerParams(dimension_semantics=("arbitrary",),
                                             vmem_limit_bytes=VMEM_LIMIT),
        name="mixer",
    )(sinks, xc, zs, dt, a_log, d_exp, norm_w, e3, ec3, qkv, qkv, qkv, bias_rows)


def _rel_bucket_static(n):
    max_exact = REL_BUCKETS // 2
    nf = np.maximum(n, 1).astype(np.float32)
    large = max_exact + (np.log(nf / max_exact) / math.log(REL_MAX_DIST / max_exact)
                         * (REL_BUCKETS - max_exact)).astype(np.int32)
    return np.where(n < max_exact, n, np.minimum(large, REL_BUCKETS - 1))


def _bias_rows(rel_bias):
    rel = (-np.arange(2 * WINDOW)) % (2 * WINDOW)
    idx = np.where(rel < WINDOW, _rel_bucket_static(rel), REL_BUCKETS)
    table = jnp.concatenate([rel_bias.astype(F32), jnp.full((1, ATTN_HEADS), -jnp.inf, F32)], axis=0)
    return table[idx].T


def _merge_kernel(ys_ref, ya_ref, g_ref, x_ref, wbs_ref, wba_ref, wmix_ref, lg_ref, lb_ref, h_ref):
    chunk = 256
    rb = 2 * ROW_BLOCK
    for r in range(0, ys_ref.shape[0], rb):
        rows = slice(r, r + rb)
        ys, ya = ys_ref[rows, :], ya_ref[rows, :]
        merged = []
        for c in range(0, D_MODEL, chunk):
            a = _dot(ys, wbs_ref[:, c:c + chunk])
            b = _dot(ya, wba_ref[:, c:c + chunk])
            merged.append((g_ref[rows, c:c + chunk].astype(F32) * a
                           + g_ref[rows, D_MODEL + c:D_MODEL + c + chunk].astype(F32) * b).astype(BF16))
        mix = _dot(jnp.concatenate(merged, axis=1), wmix_ref[...])
        h_ref[rows, :] = _layer_norm(DEEPNORM_ALPHA * x_ref[rows, :] + mix, lg_ref[...], lb_ref[...])


def _merge(y_ssm, y_attn, gates, x2d, w_bs, w_ba, w_mix, ln_g, ln_b, tm):
    s = x2d.shape[0]
    row = lambda i: (i, 0)
    const = lambda i: (0, 0)
    full = lambda a: pl.BlockSpec(a.shape, const)
    return pl.pallas_call(
        _merge_kernel,
        grid=(s // tm,),
        in_specs=[pl.BlockSpec((tm, SSM_D_INNER), row), pl.BlockSpec((tm, Q_COLS), row),
                  pl.BlockSpec((tm, GATE_COLS), row), pl.BlockSpec((tm, D_MODEL), row),
                  full(w_bs), full(w_ba), full(w_mix), full(ln_g), full(ln_b)],
        out_specs=pl.BlockSpec((tm, D_MODEL), row),
        out_shape=jax.ShapeDtypeStruct((s, D_MODEL), F32),
        compiler_params=pltpu.CompilerParams(dimension_semantics=("arbitrary",),
                                             vmem_limit_bytes=VMEM_LIMIT),
        name="merge",
    )(y_ssm, y_attn, gates, x2d, w_bs, w_ba, w_mix, ln_g, ln_b)


def _ffn_kernel(h_ref, wup_ref, cw_ref, cb_ref, wdn_ref, lg_ref, lb_ref, o_ref, halo_ref, act_ref):
    tm = h_ref.shape[0]

    @pl.when(pl.program_id(0) == 0)
    def _():
        halo_ref[...] = jnp.zeros_like(halo_ref)

    chunk = 256
    rb = 2 * ROW_BLOCK
    row_blocks = [slice(r, r + rb) for r in range(0, tm, rb)]
    hbs = [h_ref[rows, :].astype(BF16) for rows in row_blocks]

    def conv_cols(hb, cols, prev):
        u = _dot(hb, wup_ref[:, cols])
        return _causal_conv(u, prev, cw_ref, cb_ref, cols, FFN_CONV), u[rb - HALO_ROWS:, :]

    for c in range(0, D_FF, chunk):
        gcols, vcols = slice(c, c + chunk), slice(D_FF + c, D_FF + c + chunk)
        gprev, vprev = halo_ref[:, gcols], halo_ref[:, vcols]
        for hb, rows in zip(hbs, row_blocks):
            gate, gprev = conv_cols(hb, gcols, gprev)
            val, vprev = conv_cols(hb, vcols, vprev)
            act_ref[rows, c:c + chunk] = (_silu(gate) * val).astype(BF16)
        halo_ref[:, gcols] = gprev
        halo_ref[:, vcols] = vprev

    for rows in row_blocks:
        out = _dot(act_ref[rows, :], wdn_ref[...])
        o_ref[rows, :] = _layer_norm(DEEPNORM_ALPHA * h_ref[rows, :] + out, lg_ref[...], lb_ref[...])


def _ffn(h1, w_up, conv_w, conv_b, w_down, ln_g, ln_b, tm):
    s = h1.shape[0]
    row = lambda i: (i, 0)
    const = lambda i: (0, 0)
    full = lambda a: pl.BlockSpec(a.shape, const)
    return pl.pallas_call(
        _ffn_kernel,
        grid=(s // tm,),
        in_specs=[pl.BlockSpec((tm, D_MODEL), row),
                  pl.BlockSpec(w_up.shape, const, pipeline_mode=pl.Buffered(1)),
                  full(conv_w), full(conv_b),
                  pl.BlockSpec(w_down.shape, const, pipeline_mode=pl.Buffered(1)),
                  full(ln_g), full(ln_b)],
        out_specs=pl.BlockSpec((tm, D_MODEL), row),
        out_shape=jax.ShapeDtypeStruct((s, D_MODEL), F32),
        scratch_shapes=[pltpu.VMEM((HALO_ROWS, 2 * D_FF), F32),
                        pltpu.VMEM((tm, D_FF), BF16)],
        compiler_params=pltpu.CompilerParams(dimension_semantics=("arbitrary",),
                                             vmem_limit_bytes=VMEM_LIMIT),
        name="ffn",
    )(h1, w_up, conv_w, conv_b, w_down, ln_g, ln_b)


PREP_BLOCK = 512
DT_LO = SSM_D_INNER + SSM_CONV_DIM
DT_DST = PROJ_COLS - DT_PAD


def _permute_kernel(a_ref, b_ref, d_ref, o_ref):
    j = pl.program_id(0)
    first_shifted = DT_LO // PREP_BLOCK
    last = (PROJ_COLS - 1) // PREP_BLOCK

    @pl.when(j < first_shifted)
    def _():
        o_ref[...] = a_ref[...].astype(BF16)

    def shifted():
        return jnp.concatenate([a_ref[:, SSM_HEADS:], b_ref[:, :SSM_HEADS]], axis=1)

    @pl.when((j >= first_shifted) & (j < last))
    def _():
        o_ref[...] = shifted().astype(BF16)

    @pl.when(j == last)
    def _():
        n_tail = DT_DST - last * PREP_BLOCK
        tail = a_ref[:, SSM_HEADS:SSM_HEADS + n_tail]
        lane = lax.broadcasted_iota(jnp.int32, (D_MODEL, DT_PAD), 1)
        dt_cols = jnp.where(lane < SSM_HEADS, d_ref[:, :DT_PAD], 0.0)
        o_ref[:, :n_tail + DT_PAD] = jnp.concatenate([tail, dt_cols], axis=1).astype(BF16)


def _permute_w_in(w):
    assert DT_LO % PREP_BLOCK == 0 and w.shape == (D_MODEL, DT_LO + SSM_HEADS + QKV_COLS + GATE_COLS)
    first_shifted = DT_LO // PREP_BLOCK
    last_src = (w.shape[1] - 1) // PREP_BLOCK
    blk = lambda f: pl.BlockSpec((D_MODEL, PREP_BLOCK), f)
    return pl.pallas_call(
        _permute_kernel,
        grid=(pl.cdiv(PROJ_COLS, PREP_BLOCK),),
        in_specs=[blk(lambda j: (0, jnp.minimum(j, last_src))),
                  blk(lambda j: (0, jnp.clip(j + 1, first_shifted, last_src))),
                  blk(lambda j: (0, first_shifted))],
        out_specs=blk(lambda j: (0, j)),
        out_shape=jax.ShapeDtypeStruct((D_MODEL, PROJ_COLS), BF16),
        compiler_params=pltpu.CompilerParams(dimension_semantics=("arbitrary",),
                                             vmem_limit_bytes=VMEM_LIMIT),
        name="permute_w_in",
    )(w, w, w)


def _row(v, width=None):
    v = v.astype(F32).reshape(1, -1)
    if width is not None and v.shape[1] < width:
        v = jnp.pad(v, ((0, 0), (0, width - v.shape[1])))
    return v


def kernel(x, rel_bias, w_in, b_gate, ssm_conv_w, ssm_conv_b, ssm_dt_bias, ssm_a_log, ssm_d, ssm_norm_w,
           attn_sinks, w_branch_ssm, w_branch_attn, w_mix_out, ln1_g, ln1_b, w_up, ffn_conv_w, ffn_conv_b,
           w_down, ln2_g, ln2_b):
    b, s, d = x.shape
    assert (b, d) == (1, D_MODEL) and s % 512 == 0 and w_in.shape[0] == DEPTH
    t = _tiles()
    h = x.reshape(s, d)
    bias_rows = _bias_rows(rel_bias)
    for l in range(DEPTH):
        zs, xc, qkv, gates, dt = _inproj(h, _permute_w_in(w_in[l]), ssm_conv_w[l].astype(F32),
                                         _row(ssm_conv_b[l]), _row(ssm_dt_bias[l], DT_PAD),
                                         _row(b_gate[l]), t["inproj"])
        y_ssm, y_attn = _mixer(xc, zs, dt, qkv, _row(ssm_a_log[l], DT_PAD),
                               _row(jnp.repeat(ssm_d[l], SSM_HEADDIM)), _row(ssm_norm_w[l]),
                               bias_rows, attn_sinks[l].astype(F32), t["mixer"])
        h1 = _merge(y_ssm, y_attn, gates, h, w_branch_ssm[l].astype(BF16), w_branch_attn[l].astype(BF16),
                    w_mix_out[l].astype(BF16), _row(ln1_g[l]), _row(ln1_b[l]), t["merge"])
        h = _ffn(h1, w_up[l].astype(BF16), ffn_conv_w[l].astype(F32), _row(ffn_conv_b[l]),
                 w_down[l].astype(BF16), _row(ln2_g[l]), _row(ln2_b[l]), t["ffn"])
    return h.reshape(b, s, d)
```

```python
import functools
import math

import numpy as np
import jax
import jax.numpy as jnp
from jax import lax
from jax.experimental import pallas as pl
from jax.experimental.pallas import tpu as pltpu

F32 = jnp.float32
BF16 = jnp.bfloat16

D_MODEL = 1024
SSM_D_INNER = 2048
SSM_HEADDIM = 64
SSM_HEADS = 32
SSM_GROUPS = 4
SSM_HEADS_PER_GROUP = 8
SSM_STATE = 128
SSM_CONV = 4
SSM_CHUNK = 128
SSM_CONV_DIM = SSM_D_INNER + 2 * SSM_GROUPS * SSM_STATE
GROUP_WIDTH = SSM_HEADS_PER_GROUP * SSM_HEADDIM
ATTN_HEADS = 16
ATTN_KV_HEADS = 2
ATTN_HEADDIM = 64
WINDOW = 128
REL_BUCKETS = 32
REL_MAX_DIST = 128
Q_COLS = ATTN_HEADS * ATTN_HEADDIM
KV_COLS = ATTN_KV_HEADS * ATTN_HEADDIM
QKV_COLS = Q_COLS + 2 * KV_COLS
GATE_COLS = 2 * D_MODEL
D_FF = 2816
FFN_CONV = 3
DEPTH = 1
DEEPNORM_ALPHA = (2.0 * DEPTH) ** 0.25
LN_EPS = 1e-5
RMS_EPS = 1e-5

LANES = 128
SUBLANES = 8
DT_PAD = LANES
HALO_ROWS = SUBLANES
SSD_LOOKAHEAD = 1
ATTN_LOOKAHEAD = 1
ROW_BLOCK = 128
VMEM_LIMIT = 56 * 1024 * 1024

PROJ_SEGMENTS = (SSM_D_INNER, SSM_CONV_DIM, QKV_COLS, GATE_COLS, DT_PAD)
PROJ_COLS = sum(PROJ_SEGMENTS)

NEG_BIG = -1e30


def _tiles():
    assert SSM_CHUNK == WINDOW
    return dict(inproj=256, mixer=2 * SSM_CHUNK, merge=256, ffn=512)


def _dot(a, b):
    return jnp.dot(a, b, preferred_element_type=F32)


def _dot_nt(a, b):
    return lax.dot_general(a, b, (((1,), (1,)), ((), ())), preferred_element_type=F32)


def _sigmoid(x):
    return 1.0 / (1.0 + jnp.exp(-x))


def _silu(x):
    return x * _sigmoid(x)


def _layer_norm(r, g, b):
    mu = jnp.mean(r, axis=-1, keepdims=True)
    rc = r - mu
    var = jnp.mean(rc * rc, axis=-1, keepdims=True)
    return rc * lax.rsqrt(var + LN_EPS) * g + b


def _shift_rows(x, prev, j):
    row = lax.broadcasted_iota(jnp.int32, prev.shape, 0)
    sh = pltpu.roll(x, j, 0)
    top = jnp.where(row < j, pltpu.roll(prev, j, 0), sh[:HALO_ROWS])
    return jnp.concatenate([top, sh[HALO_ROWS:]], axis=0)


def _causal_conv(x, prev, w_ref, b_ref, cols, taps):
    acc = b_ref[:, cols] + w_ref[taps - 1:taps, cols] * x
    for j in range(1, taps):
        acc = acc + w_ref[taps - 1 - j:taps - j, cols] * _shift_rows(x, prev, j)
    return acc


def _inproj_kernel(x_ref, w_ref, cw_ref, cb_ref, dtb_ref, bg_ref,
                   zs_ref, xc_ref, qkv_ref, g_ref, dt_ref, halo_ref):
    tm = x_ref.shape[0]

    @pl.when(pl.program_id(0) == 0)
    def _():
        halo_ref[...] = jnp.zeros_like(halo_ref)

    chunk = 256
    row_blocks = [slice(r, r + ROW_BLOCK) for r in range(0, tm, ROW_BLOCK)]
    xbs = [x_ref[rows, :].astype(BF16) for rows in row_blocks]
    offs = [sum(PROJ_SEGMENTS[:i]) for i in range(len(PROJ_SEGMENTS))]

    def proj(i, seg, c, w):
        return _dot(xbs[i], w_ref[:, offs[seg] + c:offs[seg] + c + w])

    for c in range(0, SSM_D_INNER, chunk):
        for i, rows in enumerate(row_blocks):
            zs_ref[rows, c:c + chunk] = _silu(proj(i, 0, c, chunk)).astype(BF16)
    for c in range(0, SSM_CONV_DIM, chunk):
        cols = slice(c, c + chunk)
        prev = halo_ref[:, cols]
        for i, rows in enumerate(row_blocks):
            u = proj(i, 1, c, chunk)
            xc_ref[rows, cols] = _silu(_causal_conv(u, prev, cw_ref, cb_ref, cols, SSM_CONV)).astype(BF16)
            prev = u[ROW_BLOCK - HALO_ROWS:, :]
        halo_ref[:, cols] = prev
    for c in range(0, QKV_COLS, chunk):
        w = min(chunk, QKV_COLS - c)
        for i, rows in enumerate(row_blocks):
            qkv_ref[rows, c:c + w] = proj(i, 2, c, w).astype(BF16)
    for c in range(0, GATE_COLS, chunk):
        for i, rows in enumerate(row_blocks):
            g_ref[rows, c:c + chunk] = _sigmoid(proj(i, 3, c, chunk) + bg_ref[:, c:c + chunk]).astype(BF16)
    lane_ok = lax.broadcasted_iota(jnp.int32, (ROW_BLOCK, DT_PAD), 1) < SSM_HEADS
    for i, rows in enumerate(row_blocks):
        dtr = proj(i, 4, 0, DT_PAD) + dtb_ref[...]
        softplus = jnp.maximum(dtr, 0.0) + jnp.log(1.0 + jnp.exp(-jnp.abs(dtr)))
        dt_ref[rows, :] = jnp.where(lane_ok, softplus, 0.0)


def _inproj(x2d, w_perm, conv_w, conv_b, dt_bias, b_gate, tm):
    s = x2d.shape[0]
    row = lambda i: (i, 0)
    const = lambda i: (0, 0)
    full = lambda a: pl.BlockSpec(a.shape, const)
    outs = [jax.ShapeDtypeStruct((s, n), dt) for n, dt in
            zip(PROJ_SEGMENTS, (BF16, BF16, BF16, BF16, F32))]
    return pl.pallas_call(
        _inproj_kernel,
        grid=(s // tm,),
        in_specs=[pl.BlockSpec((tm, D_MODEL), row),
                  pl.BlockSpec((D_MODEL, PROJ_COLS), const, pipeline_mode=pl.Buffered(1)),
                  full(conv_w), full(conv_b), full(dt_bias), full(b_gate)],
        out_specs=[pl.BlockSpec((tm, n), row) for n in PROJ_SEGMENTS],
        out_shape=outs,
        scratch_shapes=[pltpu.VMEM((HALO_ROWS, SSM_CONV_DIM), F32)],
        compiler_params=pltpu.CompilerParams(dimension_semantics=("arbitrary",),
                                             vmem_limit_bytes=VMEM_LIMIT),
        name="inproj",
    )(x2d, w_perm, conv_w, conv_b, dt_bias, b_gate)


def _split3(x):
    hi = x.astype(BF16).astype(F32)
    r = x - hi
    mid = r.astype(BF16).astype(F32)
    lo = (r - mid).astype(BF16).astype(F32)
    return hi, mid, lo


def _pack3(x):
    hi, mid, lo = _split3(x)
    return (hi + pltpu.roll(mid, SSM_HEADS, 1) + pltpu.roll(lo, 2 * SSM_HEADS, 1)).astype(BF16)


def _ssd_body(conv_ref, zs_ref, dt_ref, alog_ref, dexp_ref, nw_ref, e3_ref, ec3_ref, y_ref, state_ref):
    L = SSM_CHUNK
    lane_ok = lax.broadcasted_iota(jnp.int32, (L, DT_PAD), 1) < SSM_HEADS
    dt = dt_ref[...]
    a_dt = dt * (-jnp.exp(alog_ref[...]))

    ri = lax.broadcasted_iota(jnp.int32, (L, L), 0)
    ci = lax.broadcasted_iota(jnp.int32, (L, L), 1)
    causal = ri >= ci
    tril = jnp.where(causal, 1.0, 0.0).astype(BF16)
    hi, mid, lo = _split3(a_dt)
    a_cs = _dot(tril, hi.astype(BF16)) + _dot(tril, mid.astype(BF16)) + _dot(tril, lo.astype(BF16))

    a_last = a_cs[L - 1:L, :]
    decay_out_b = jnp.where(lane_ok, jnp.exp(a_cs), 0.0).astype(BF16)
    w_state_b = (dt * jnp.exp(a_last - a_cs)).astype(BF16)
    cd_e = jnp.exp(_dot(_pack3(a_cs[L - SUBLANES:, :]), e3_ref[...])[SUBLANES - 1:, :])

    a_sub_t = (a_cs - jnp.log(dt)).T
    acs3 = _pack3(a_cs)
    lane_lo = lax.broadcasted_iota(jnp.int32, (L, LANES), 1) < SSM_HEADDIM

    def group_start(g):
        gsl = slice(g * GROUP_WIDTH, (g + 1) * GROUP_WIDTH)
        xs_g = conv_ref[:, gsl].astype(F32)
        b_off = SSM_D_INNER + g * SSM_STATE
        c_off = SSM_D_INNER + SSM_GROUPS * SSM_STATE + g * SSM_STATE
        b_g = conv_ref[:, b_off:b_off + SSM_STATE]
        c_g = conv_ref[:, c_off:c_off + SSM_STATE]
        cb = _dot_nt(c_g, b_g)
        xdtd_b = (xs_g * _dot(w_state_b, e3_ref[:, gsl])).astype(BF16)
        st = state_ref[g]
        y_off = _dot(c_g, st.astype(BF16)) * _dot(decay_out_b, e3_ref[:, gsl])
        state_ref[g] = st * cd_e[:, gsl] + _dot(b_g.astype(F32).T.astype(BF16), xdtd_b)
        return cb, y_off + xs_g * dexp_ref[:, gsl]

    def acol_dot(p):
        return _dot(acs3, ec3_ref[:, 2 * p * L:(2 * p + 2) * L])

    pairs_per_group = SSM_HEADS_PER_GROUP // 2
    n_pairs = SSM_HEADS // 2
    started = {0: group_start(0)}
    acols = {p: acol_dot(p) for p in range(SSD_LOOKAHEAD)}
    pieces = []
    for p in range(n_pairs):
        g, pr = divmod(p, pairs_per_group)
        if p + SSD_LOOKAHEAD < n_pairs:
            acols[p + SSD_LOOKAHEAD] = acol_dot(p + SSD_LOOKAHEAD)
        if pr == pairs_per_group - 2 and g + 1 < SSM_GROUPS:
            started[g + 1] = group_start(g + 1)
        cb, y_rest = started[g]
        acol = acols.pop(p)
        ms = []
        for k in range(2):
            seg = acol[:, k * L:(k + 1) * L] - a_sub_t[2 * p + k:2 * p + k + 1, :]
            ms.append((cb * jnp.exp(jnp.where(causal, seg, NEG_BIG))).astype(BF16))
        xp = conv_ref[:, p * LANES:(p + 1) * LANES]
        zero = jnp.zeros_like(xp)
        rhs = jnp.concatenate([jnp.where(lane_lo, xp, zero), jnp.where(lane_lo, zero, xp)], axis=0)
        pieces.append(_dot(jnp.concatenate(ms, axis=1), rhs))
        if pr == pairs_per_group - 1:
            gsl = slice(g * GROUP_WIDTH, (g + 1) * GROUP_WIDTH)
            yz = (jnp.concatenate(pieces, axis=1) + y_rest) * zs_ref[:, gsl].astype(F32)
            ms_ = jnp.mean(yz * yz, axis=-1, keepdims=True)
            y_ref[:, gsl] = (yz * lax.rsqrt(ms_ + RMS_EPS) * nw_ref[:, gsl]).astype(BF16)
            pieces = []
            del started[g]
        if p % 2 == 1:
            yield


def _expansion_matrices():
    k = np.arange(LANES)[:, None]
    valid = k < 3 * SSM_HEADS
    head = k % SSM_HEADS
    e3 = (valid & (head == (np.arange(SSM_D_INNER)[None, :] // SSM_HEADDIM))).astype(np.float32)
    ec3 = (valid & (head == (np.arange(SSM_HEADS * SSM_CHUNK)[None, :] // SSM_CHUNK))).astype(np.float32)
    return jnp.asarray(e3, BF16), jnp.asarray(ec3, BF16)


def _build_bias_mask(ur_ref, bm_ref):
    W = WINDOW
    first = lax.broadcasted_iota(jnp.int32, (W, 2 * W), 1) >= W
    for h in range(ATTN_HEADS):
        band_h = pltpu.roll(jnp.broadcast_to(ur_ref[h:h + 1, :], (W, 2 * W)), W, 1,
                            stride=1, stride_axis=0)
        bm_ref[1, h] = band_h
        bm_ref[0, h] = jnp.where(first, band_h, -jnp.inf)


def _attn_body(sink_ref, q_ref, kvc_ref, kvp_ref, o_ref, bm_ref, slab):
    W = WINDOW
    lane_lo2 = lax.broadcasted_iota(jnp.int32, (2 * W, LANES), 1) < ATTN_HEADDIM
    lane_lo = lax.broadcasted_iota(jnp.int32, (W, LANES), 1) < ATTN_HEADDIM

    def band(lo_col):
        return jnp.concatenate([kvp_ref[:, lo_col:lo_col + KV_COLS].astype(F32),
                                kvc_ref[:, lo_col:lo_col + KV_COLS].astype(F32)], axis=0)

    kband = band(0) * (ATTN_HEADDIM ** -0.5)
    vband = band(KV_COLS)
    ksw = pltpu.roll(kband, ATTN_HEADDIM, 1)
    vsw = pltpu.roll(vband, ATTN_HEADDIM, 1)
    k_var = ((jnp.where(lane_lo2, kband, 0.0).astype(BF16), jnp.where(lane_lo2, 0.0, ksw).astype(BF16)),
             (jnp.where(lane_lo2, ksw, 0.0).astype(BF16), jnp.where(lane_lo2, 0.0, kband).astype(BF16)))
    v_bd = (jnp.concatenate([jnp.where(lane_lo2, vband, 0.0), jnp.where(lane_lo2, 0.0, vsw)], axis=0).astype(BF16),
            jnp.concatenate([jnp.where(lane_lo2, vsw, 0.0), jnp.where(lane_lo2, 0.0, vband)], axis=0).astype(BF16))

    def kv_head(i):
        return (2 * i) // (ATTN_HEADS // ATTN_KV_HEADS)

    def logits(i):
        qp = q_ref[:, i * LANES:(i + 1) * LANES]
        return [_dot_nt(qp, k_var[kv_head(i)][j]) for j in range(2)]

    n_pairs = ATTN_HEADS // 2
    pending = {i: logits(i) for i in range(ATTN_LOOKAHEAD)}
    for i in range(n_pairs):
        c = kv_head(i)
        if i + ATTN_LOOKAHEAD < n_pairs:
            pending[i + ATTN_LOOKAHEAD] = logits(i + ATTN_LOOKAHEAD)
        s_cur = pending.pop(i)
        ps, rs = [], []
        for j in range(2):
            h = 2 * i + j
            sink = sink_ref[h]
            s = s_cur[j] + bm_ref[slab, h]
            m = jnp.maximum(jnp.max(s, axis=-1, keepdims=True), sink)
            p = jnp.exp(s - m)
            den = jnp.sum(p, axis=-1, keepdims=True) + jnp.exp(sink - m)
            ps.append(p.astype(BF16))
            rs.append(1.0 / den)
        pv = _dot(jnp.concatenate(ps, axis=1), v_bd[c])
        o_ref[:, i * LANES:(i + 1) * LANES] = (pv * jnp.where(lane_lo, rs[0], rs[1])).astype(BF16)
        yield


def _mixer_kernel(sink_ref, conv_ref, zs_ref, dt_ref, alog_ref, dexp_ref, nw_ref, e3_ref, ec3_ref,
                  q_ref, kvc_ref, kvp_ref, ur_ref, ys_ref, ya_ref, state_ref, bm_ref):
    @pl.when(pl.program_id(0) == 0)
    def _():
        state_ref[...] = jnp.zeros_like(state_ref)
        _build_bias_mask(ur_ref, bm_ref)

    L = SSM_CHUNK
    live = []
    for blk in range(conv_ref.shape[0] // L):
        rows = pl.ds(blk * L, L)
        live.append(_ssd_body(conv_ref.at[rows], zs_ref.at[rows], dt_ref.at[rows], alog_ref, dexp_ref, nw_ref,
                              e3_ref, ec3_ref, ys_ref.at[rows], state_ref))
        prev = kvp_ref if blk == 0 else kvc_ref.at[pl.ds((blk - 1) * L, L)]
        slab = jnp.minimum(pl.program_id(0), 1) if blk == 0 else 1
        live.append(_attn_body(sink_ref, q_ref.at[rows], kvc_ref.at[rows], prev, ya_ref.at[rows], bm_ref, slab))
    while live:
        for body in list(live):
            if next(body, "done") == "done":
                live.remove(body)


def _mixer(xc, zs, dt, qkv, a_log, d_exp, norm_w, bias_rows, sinks, tm):
    s = xc.shape[0]
    e3, ec3 = _expansion_matrices()
    row = lambda i: (i, 0)
    const = lambda i: (0, 0)
    full = lambda a: pl.BlockSpec(a.shape, const)
    kv_blk = Q_COLS // (2 * KV_COLS)
    return pl.pallas_call(
        _mixer_kernel,
        grid=(s // tm,),
        in_specs=[pl.BlockSpec(memory_space=pltpu.SMEM),
                  pl.BlockSpec((tm, SSM_CONV_DIM), row),
                  pl.BlockSpec((tm, SSM_D_INNER), row),
                  pl.BlockSpec((tm, DT_PAD), row),
                  full(a_log), full(d_exp), full(norm_w), full(e3), full(ec3),
                  pl.BlockSpec((tm, Q_COLS), row),
                  pl.BlockSpec((tm, 2 * KV_COLS), lambda i: (i, kv_blk)),
                  pl.BlockSpec((WINDOW, 2 * KV_COLS),
                               lambda i: (jnp.maximum(i * (tm // WINDOW) - 1, 0), kv_blk)),
                  full(bias_rows)],
        out_specs=[pl.BlockSpec((tm, SSM_D_INNER), row), pl.BlockSpec((tm, Q_COLS), row)],
        out_shape=[jax.ShapeDtypeStruct((s, SSM_D_INNER), BF16), jax.ShapeDtypeStruct((s, Q_COLS), BF16)],
        scratch_shapes=[pltpu.VMEM((SSM_GROUPS, SSM_STATE, GROUP_WIDTH), F32),
                        pltpu.VMEM((2, ATTN_HEADS, WINDOW, 2 * WINDOW), F32)],
        compiler_params=pltpu.CompilerParams(dimension_semantics=("arbitrary",),
                                             vmem_limit_bytes=VMEM_LIMIT),
        name="mixer",
    )(sinks, xc, zs, dt, a_log, d_exp, norm_w, e3, ec3, qkv, qkv, qkv, bias_rows)


def _rel_bucket_static(n):
    max_exact = REL_BUCKETS // 2
    nf = np.maximum(n, 1).astype(np.float32)
    large = max_exact + (np.log(nf / max_exact) / math.log(REL_MAX_DIST / max_exact)
                         * (REL_BUCKETS - max_exact)).astype(np.int32)
    return np.where(n < max_exact, n, np.minimum(large, REL_BUCKETS - 1))


def _bias_rows(rel_bias):
    rel = (-np.arange(2 * WINDOW)) % (2 * WINDOW)
    idx = np.where(rel < WINDOW, _rel_bucket_static(rel), REL_BUCKETS)
    table = jnp.concatenate([rel_bias.astype(F32), jnp.full((1, ATTN_HEADS), -jnp.inf, F32)], axis=0)
    return table[idx].T


def _merge_kernel(ys_ref, ya_ref, g_ref, x_ref, wbs_ref, wba_ref, wmix_ref, lg_ref, lb_ref, h_ref):
    chunk = 256
    rb = 2 * ROW_BLOCK
    for r in range(0, ys_ref.shape[0], rb):
        rows = slice(r, r + rb)
        ys, ya = ys_ref[rows, :], ya_ref[rows, :]
        merged = []
        for c in range(0, D_MODEL, chunk):
            a = _dot(ys, wbs_ref[:, c:c + chunk])
            b = _dot(ya, wba_ref[:, c:c + chunk])
            merged.append((g_ref[rows, c:c + chunk].astype(F32) * a
                           + g_ref[rows, D_MODEL + c:D_MODEL + c + chunk].astype(F32) * b).astype(BF16))
        mix = _dot(jnp.concatenate(merged, axis=1), wmix_ref[...])
        h_ref[rows, :] = _layer_norm(DEEPNORM_ALPHA * x_ref[rows, :] + mix, lg_ref[...], lb_ref[...])


def _merge(y_ssm, y_attn, gates, x2d, w_bs, w_ba, w_mix, ln_g, ln_b, tm):
    s = x2d.shape[0]
    row = lambda i: (i, 0)
    const = lambda i: (0, 0)
    full = lambda a: pl.BlockSpec(a.shape, const)
    return pl.pallas_call(
        _merge_kernel,
        grid=(s // tm,),
        in_specs=[pl.BlockSpec((tm, SSM_D_INNER), row), pl.BlockSpec((tm, Q_COLS), row),
                  pl.BlockSpec((tm, GATE_COLS), row), pl.BlockSpec((tm, D_MODEL), row),
                  full(w_bs), full(w_ba), full(w_mix), full(ln_g), full(ln_b)],
        out_specs=pl.BlockSpec((tm, D_MODEL), row),
        out_shape=jax.ShapeDtypeStruct((s, D_MODEL), F32),
        compiler_params=pltpu.CompilerParams(dimension_semantics=("arbitrary",),
                                             vmem_limit_bytes=VMEM_LIMIT),
        name="merge",
    )(y_ssm, y_attn, gates, x2d, w_bs, w_ba, w_mix, ln_g, ln_b)


def _ffn_kernel(h_ref, wup_ref, cw_ref, cb_ref, wdn_ref, lg_ref, lb_ref, o_ref, halo_ref, act_ref):
    tm = h_ref.shape[0]

    @pl.when(pl.program_id(0) == 0)
    def _():
        halo_ref[...] = jnp.zeros_like(halo_ref)

    chunk = 256
    rb = 2 * ROW_BLOCK
    row_blocks = [slice(r, r + rb) for r in range(0, tm, rb)]
    hbs = [h_ref[rows, :].astype(BF16) for rows in row_blocks]

    def conv_cols(hb, cols, prev):
        u = _dot(hb, wup_ref[:, cols])
        return _causal_conv(u, prev, cw_ref, cb_ref, cols, FFN_CONV), u[rb - HALO_ROWS:, :]

    for c in range(0, D_FF, chunk):
        gcols, vcols = slice(c, c + chunk), slice(D_FF + c, D_FF + c + chunk)
        gprev, vprev = halo_ref[:, gcols], halo_ref[:, vcols]
        for hb, rows in zip(hbs, row_blocks):
            gate, gprev = conv_cols(hb, gcols, gprev)
            val, vprev = conv_cols(hb, vcols, vprev)
            act_ref[rows, c:c + chunk] = (_silu(gate) * val).astype(BF16)
        halo_ref[:, gcols] = gprev
        halo_ref[:, vcols] = vprev

    for rows in row_blocks:
        out = _dot(act_ref[rows, :], wdn_ref[...])
        o_ref[rows, :] = _layer_norm(DEEPNORM_ALPHA * h_ref[rows, :] + out, lg_ref[...], lb_ref[...])


def _ffn(h1, w_up, conv_w, conv_b, w_down, ln_g, ln_b, tm):
    s = h1.shape[0]
    row = lambda i: (i, 0)
    const = lambda i: (0, 0)
    full = lambda a: pl.BlockSpec(a.shape, const)
    return pl.pallas_call(
        _ffn_kernel,
        grid=(s // tm,),
        in_specs=[pl.BlockSpec((tm, D_MODEL), row),
                  pl.BlockSpec(w_up.shape, const, pipeline_mode=pl.Buffered(1)),
                  full(conv_w), full(conv_b),
                  pl.BlockSpec(w_down.shape, const, pipeline_mode=pl.Buffered(1)),
                  full(ln_g), full(ln_b)],
        out_specs=pl.BlockSpec((tm, D_MODEL), row),
        out_shape=jax.ShapeDtypeStruct((s, D_MODEL), F32),
        scratch_shapes=[pltpu.VMEM((HALO_ROWS, 2 * D_FF), F32),
                        pltpu.VMEM((tm, D_FF), BF16)],
        compiler_params=pltpu.CompilerParams(dimension_semantics=("arbitrary",),
                                             vmem_limit_bytes=VMEM_LIMIT),
        name="ffn",
    )(h1, w_up, conv_w, conv_b, w_down, ln_g, ln_b)


PREP_BLOCK = 512
DT_LO = SSM_D_INNER + SSM_CONV_DIM
DT_DST = PROJ_COLS - DT_PAD


def _permute_kernel(a_ref, b_ref, d_ref, o_ref):
    j = pl.program_id(0)
    first_shifted = DT_LO // PREP_BLOCK
    last = (PROJ_COLS - 1) // PREP_BLOCK

    @pl.when(j < first_shifted)
    def _():
        o_ref[...] = a_ref[...].T.astype(BF16)

    @pl.when((j >= first_shifted) & (j < last))
    def _():
        rows = jnp.concatenate([a_ref[SSM_HEADS:, :], b_ref[:SSM_HEADS, :]], axis=0)
        o_ref[...] = rows.T.astype(BF16)

    @pl.when(j == last)
    def _():
        n_tail = DT_DST - last * PREP_BLOCK
        rows = jnp.concatenate([a_ref[SSM_HEADS:SSM_HEADS + n_tail, :], d_ref[:SSM_HEADS, :],
                                jnp.zeros((DT_PAD - SSM_HEADS, D_MODEL), F32)], axis=0)
        o_ref[:, :n_tail + DT_PAD] = rows.T.astype(BF16)


def _permute_w_in(w_t):
    n_src = DT_LO + SSM_HEADS + QKV_COLS + GATE_COLS
    assert DT_LO % PREP_BLOCK == 0 and w_t.shape == (n_src, D_MODEL)
    first_shifted = DT_LO // PREP_BLOCK
    last_src = (n_src - 1) // PREP_BLOCK
    blk = lambda f: pl.BlockSpec((PREP_BLOCK, D_MODEL), f)
    return pl.pallas_call(
        _permute_kernel,
        grid=(pl.cdiv(PROJ_COLS, PREP_BLOCK),),
        in_specs=[blk(lambda j: (jnp.minimum(j, last_src), 0)),
                  blk(lambda j: (jnp.clip(j + 1, first_shifted, last_src), 0)),
                  blk(lambda j: (first_shifted, 0))],
        out_specs=pl.BlockSpec((D_MODEL, PREP_BLOCK), lambda j: (0, j)),
        out_shape=jax.ShapeDtypeStruct((D_MODEL, PROJ_COLS), BF16),
        compiler_params=pltpu.CompilerParams(dimension_semantics=("arbitrary",),
                                             vmem_limit_bytes=VMEM_LIMIT),
        name="permute_w_in",
    )(w_t, w_t, w_t)


def _row(v, width=None):
    v = v.astype(F32).reshape(1, -1)
    if width is not None and v.shape[1] < width:
        v = jnp.pad(v, ((0, 0), (0, width - v.shape[1])))
    return v


def kernel(x, rel_bias, w_in, b_gate, ssm_conv_w, ssm_conv_b, ssm_dt_bias, ssm_a_log, ssm_d, ssm_norm_w,
           attn_sinks, w_branch_ssm, w_branch_attn, w_mix_out, ln1_g, ln1_b, w_up, ffn_conv_w, ffn_conv_b,
           w_down, ln2_g, ln2_b):
    b, s, d = x.shape
    assert (b, d) == (1, D_MODEL) and s % 512 == 0 and w_in.shape[0] == DEPTH
    t = _tiles()
    h = x.reshape(s, d)
    bias_rows = _bias_rows(rel_bias)
    for l in range(DEPTH):
        zs, xc, qkv, gates, dt = _inproj(h, _permute_w_in(w_in[l].T), ssm_conv_w[l].astype(F32),
                                         _row(ssm_conv_b[l]), _row(ssm_dt_bias[l], DT_PAD),
                                         _row(b_gate[l]), t["inproj"])
        y_ssm, y_attn = _mixer(xc, zs, dt, qkv, _row(ssm_a_log[l], DT_PAD),
                               _row(jnp.repeat(ssm_d[l], SSM_HEADDIM)), _row(ssm_norm_w[l]),
                               bias_rows, attn_sinks[l].astype(F32), t["mixer"])
        h1 = _merge(y_ssm, y_attn, gates, h, w_branch_ssm[l].astype(BF16), w_branch_attn[l].astype(BF16),
                    w_mix_out[l].astype(BF16), _row(ln1_g[l]), _row(ln1_b[l]), t["merge"])
        h = _ffn(h1, w_up[l].astype(BF16), ffn_conv_w[l].astype(F32), _row(ffn_conv_b[l]),
                 w_down[l].astype(BF16), _row(ln2_g[l]), _row(ln2_b[l]), t["ffn"])
    return h.reshape(b, s, d)
```

```python
import functools
import math

import numpy as np
import jax
import jax.numpy as jnp
from jax import lax
from jax.experimental import pallas as pl
from jax.experimental.pallas import tpu as pltpu

F32 = jnp.float32
BF16 = jnp.bfloat16

D_MODEL = 1024
SSM_D_INNER = 2048
SSM_HEADDIM = 64
SSM_HEADS = 32
SSM_GROUPS = 4
SSM_HEADS_PER_GROUP = 8
SSM_STATE = 128
SSM_CONV = 4
SSM_CHUNK = 128
SSM_CONV_DIM = SSM_D_INNER + 2 * SSM_GROUPS * SSM_STATE
GROUP_WIDTH = SSM_HEADS_PER_GROUP * SSM_HEADDIM
ATTN_HEADS = 16
ATTN_KV_HEADS = 2
ATTN_HEADDIM = 64
WINDOW = 128
REL_BUCKETS = 32
REL_MAX_DIST = 128
Q_COLS = ATTN_HEADS * ATTN_HEADDIM
KV_COLS = ATTN_KV_HEADS * ATTN_HEADDIM
QKV_COLS = Q_COLS + 2 * KV_COLS
GATE_COLS = 2 * D_MODEL
D_FF = 2816
FFN_CONV = 3
DEPTH = 1
DEEPNORM_ALPHA = (2.0 * DEPTH) ** 0.25
LN_EPS = 1e-5
RMS_EPS = 1e-5

LANES = 128
SUBLANES = 8
DT_PAD = LANES
HALO_ROWS = SUBLANES
SSD_LOOKAHEAD = 1
ATTN_LOOKAHEAD = 1
ROW_BLOCK = 128
VMEM_LIMIT = 56 * 1024 * 1024

PROJ_SEGMENTS = (SSM_D_INNER, SSM_CONV_DIM, QKV_COLS, GATE_COLS, DT_PAD)
PROJ_COLS = sum(PROJ_SEGMENTS)

NEG_BIG = -1e30


def _tiles():
    assert SSM_CHUNK == WINDOW
    return dict(inproj=256, mixer=2 * SSM_CHUNK, ffn=512)


def _dot(a, b):
    return jnp.dot(a, b, preferred_element_type=F32)


def _dot_nt(a, b):
    return lax.dot_general(a, b, (((1,), (1,)), ((), ())), preferred_element_type=F32)


def _sigmoid(x):
    return 1.0 / (1.0 + jnp.exp(-x))


def _silu(x):
    return x * _sigmoid(x)


def _layer_norm(r, g, b):
    mu = jnp.mean(r, axis=-1, keepdims=True)
    rc = r - mu
    var = jnp.mean(rc * rc, axis=-1, keepdims=True)
    return rc * lax.rsqrt(var + LN_EPS) * g + b


def _shift_rows(x, prev, j):
    row = lax.broadcasted_iota(jnp.int32, prev.shape, 0)
    sh = pltpu.roll(x, j, 0)
    top = jnp.where(row < j, pltpu.roll(prev, j, 0), sh[:HALO_ROWS])
    return jnp.concatenate([top, sh[HALO_ROWS:]], axis=0)


def _causal_conv(x, prev, w_ref, b_ref, cols, taps):
    acc = b_ref[:, cols] + w_ref[taps - 1:taps, cols] * x
    for j in range(1, taps):
        acc = acc + w_ref[taps - 1 - j:taps - j, cols] * _shift_rows(x, prev, j)
    return acc


def _inproj_kernel(x_ref, w_ref, cw_ref, cb_ref, dtb_ref, bg_ref,
                   zs_ref, xc_ref, qkv_ref, g_ref, dt_ref, halo_ref):
    tm = x_ref.shape[0]

    @pl.when(pl.program_id(0) == 0)
    def _():
        halo_ref[...] = jnp.zeros_like(halo_ref)

    chunk = 256
    row_blocks = [slice(r, r + ROW_BLOCK) for r in range(0, tm, ROW_BLOCK)]
    xbs = [x_ref[rows, :].astype(BF16) for rows in row_blocks]
    offs = [sum(PROJ_SEGMENTS[:i]) for i in range(len(PROJ_SEGMENTS))]

    def proj(i, seg, c, w):
        return _dot(xbs[i], w_ref[:, offs[seg] + c:offs[seg] + c + w])

    for c in range(0, SSM_D_INNER, chunk):
        for i, rows in enumerate(row_blocks):
            zs_ref[rows, c:c + chunk] = _silu(proj(i, 0, c, chunk)).astype(BF16)
    for c in range(0, SSM_CONV_DIM, chunk):
        cols = slice(c, c + chunk)
        prev = halo_ref[:, cols]
        for i, rows in enumerate(row_blocks):
            u = proj(i, 1, c, chunk)
            xc_ref[rows, cols] = _silu(_causal_conv(u, prev, cw_ref, cb_ref, cols, SSM_CONV)).astype(BF16)
            prev = u[ROW_BLOCK - HALO_ROWS:, :]
        halo_ref[:, cols] = prev
    for c in range(0, QKV_COLS, chunk):
        w = min(chunk, QKV_COLS - c)
        for i, rows in enumerate(row_blocks):
            qkv_ref[rows, c:c + w] = proj(i, 2, c, w).astype(BF16)
    for c in range(0, GATE_COLS, chunk):
        for i, rows in enumerate(row_blocks):
            g_ref[rows, c:c + chunk] = _sigmoid(proj(i, 3, c, chunk) + bg_ref[:, c:c + chunk]).astype(BF16)
    lane_ok = lax.broadcasted_iota(jnp.int32, (ROW_BLOCK, DT_PAD), 1) < SSM_HEADS
    for i, rows in enumerate(row_blocks):
        dtr = proj(i, 4, 0, DT_PAD) + dtb_ref[...]
        softplus = jnp.maximum(dtr, 0.0) + jnp.log(1.0 + jnp.exp(-jnp.abs(dtr)))
        dt_ref[rows, :] = jnp.where(lane_ok, softplus, 0.0)


def _inproj(x2d, w_perm, conv_w, conv_b, dt_bias, b_gate, tm):
    s = x2d.shape[0]
    row = lambda i: (i, 0)
    const = lambda i: (0, 0)
    full = lambda a: pl.BlockSpec(a.shape, const)
    outs = [jax.ShapeDtypeStruct((s, n), dt) for n, dt in
            zip(PROJ_SEGMENTS, (BF16, BF16, BF16, BF16, F32))]
    return pl.pallas_call(
        _inproj_kernel,
        grid=(s // tm,),
        in_specs=[pl.BlockSpec((tm, D_MODEL), row),
                  pl.BlockSpec((D_MODEL, PROJ_COLS), const, pipeline_mode=pl.Buffered(1)),
                  full(conv_w), full(conv_b), full(dt_bias), full(b_gate)],
        out_specs=[pl.BlockSpec((tm, n), row) for n in PROJ_SEGMENTS],
        out_shape=outs,
        scratch_shapes=[pltpu.VMEM((HALO_ROWS, SSM_CONV_DIM), F32)],
        compiler_params=pltpu.CompilerParams(dimension_semantics=("arbitrary",),
                                             vmem_limit_bytes=VMEM_LIMIT),
        name="inproj",
    )(x2d, w_perm, conv_w, conv_b, dt_bias, b_gate)


def _split3(x):
    hi = x.astype(BF16).astype(F32)
    r = x - hi
    mid = r.astype(BF16).astype(F32)
    lo = (r - mid).astype(BF16).astype(F32)
    return hi, mid, lo


def _pack3(x):
    hi, mid, lo = _split3(x)
    return (hi + pltpu.roll(mid, SSM_HEADS, 1) + pltpu.roll(lo, 2 * SSM_HEADS, 1)).astype(BF16)


def _ssd_body(conv_ref, zs_ref, dt_ref, alog_ref, dexp_ref, nw_ref, e3_ref, ec3_ref, y_ref, state_ref):
    L = SSM_CHUNK
    lane_ok = lax.broadcasted_iota(jnp.int32, (L, DT_PAD), 1) < SSM_HEADS
    dt = dt_ref[...]
    a_dt = dt * (-jnp.exp(alog_ref[...]))

    ri = lax.broadcasted_iota(jnp.int32, (L, L), 0)
    ci = lax.broadcasted_iota(jnp.int32, (L, L), 1)
    causal = ri >= ci
    tril = jnp.where(causal, 1.0, 0.0).astype(BF16)
    hi, mid, lo = _split3(a_dt)
    a_cs = _dot(tril, hi.astype(BF16)) + _dot(tril, mid.astype(BF16)) + _dot(tril, lo.astype(BF16))

    a_last = a_cs[L - 1:L, :]
    decay_out_b = jnp.where(lane_ok, jnp.exp(a_cs), 0.0).astype(BF16)
    w_state_b = (dt * jnp.exp(a_last - a_cs)).astype(BF16)
    cd_e = jnp.exp(_dot(_pack3(a_cs[L - SUBLANES:, :]), e3_ref[...])[SUBLANES - 1:, :])

    a_sub_t = (a_cs - jnp.log(dt)).T
    acs3 = _pack3(a_cs)
    lane_lo = lax.broadcasted_iota(jnp.int32, (L, LANES), 1) < SSM_HEADDIM

    def group_start(g):
        gsl = slice(g * GROUP_WIDTH, (g + 1) * GROUP_WIDTH)
        xs_g = conv_ref[:, gsl].astype(F32)
        b_off = SSM_D_INNER + g * SSM_STATE
        c_off = SSM_D_INNER + SSM_GROUPS * SSM_STATE + g * SSM_STATE
        b_g = conv_ref[:, b_off:b_off + SSM_STATE]
        c_g = conv_ref[:, c_off:c_off + SSM_STATE]
        cb = _dot_nt(c_g, b_g)
        xdtd_b = (xs_g * _dot(w_state_b, e3_ref[:, gsl])).astype(BF16)
        st = state_ref[g]
        y_off = _dot(c_g, st.astype(BF16)) * _dot(decay_out_b, e3_ref[:, gsl])
        state_ref[g] = st * cd_e[:, gsl] + _dot(b_g.astype(F32).T.astype(BF16), xdtd_b)
        return cb, y_off + xs_g * dexp_ref[:, gsl]

    def acol_dot(p):
        return _dot(acs3, ec3_ref[:, 2 * p * L:(2 * p + 2) * L])

    pairs_per_group = SSM_HEADS_PER_GROUP // 2
    n_pairs = SSM_HEADS // 2
    started = {0: group_start(0)}
    acols = {p: acol_dot(p) for p in range(SSD_LOOKAHEAD)}
    pieces = []
    for p in range(n_pairs):
        g, pr = divmod(p, pairs_per_group)
        if p + SSD_LOOKAHEAD < n_pairs:
            acols[p + SSD_LOOKAHEAD] = acol_dot(p + SSD_LOOKAHEAD)
        if pr == pairs_per_group - 2 and g + 1 < SSM_GROUPS:
            started[g + 1] = group_start(g + 1)
        cb, y_rest = started[g]
        acol = acols.pop(p)
        ms = []
        for k in range(2):
            seg = acol[:, k * L:(k + 1) * L] - a_sub_t[2 * p + k:2 * p + k + 1, :]
            ms.append((cb * jnp.exp(jnp.where(causal, seg, NEG_BIG))).astype(BF16))
        xp = conv_ref[:, p * LANES:(p + 1) * LANES]
        zero = jnp.zeros_like(xp)
        rhs = jnp.concatenate([jnp.where(lane_lo, xp, zero), jnp.where(lane_lo, zero, xp)], axis=0)
        pieces.append(_dot(jnp.concatenate(ms, axis=1), rhs))
        if pr == pairs_per_group - 1:
            gsl = slice(g * GROUP_WIDTH, (g + 1) * GROUP_WIDTH)
            yz = (jnp.concatenate(pieces, axis=1) + y_rest) * zs_ref[:, gsl].astype(F32)
            ms_ = jnp.mean(yz * yz, axis=-1, keepdims=True)
            y_ref[:, gsl] = (yz * lax.rsqrt(ms_ + RMS_EPS) * nw_ref[:, gsl]).astype(BF16)
            pieces = []
            del started[g]
        if p % 2 == 1:
            yield


def _expansion_matrices():
    k = np.arange(LANES)[:, None]
    valid = k < 3 * SSM_HEADS
    head = k % SSM_HEADS
    e3 = (valid & (head == (np.arange(SSM_D_INNER)[None, :] // SSM_HEADDIM))).astype(np.float32)
    ec3 = (valid & (head == (np.arange(SSM_HEADS * SSM_CHUNK)[None, :] // SSM_CHUNK))).astype(np.float32)
    return jnp.asarray(e3, BF16), jnp.asarray(ec3, BF16)


def _build_bias_mask(ur_ref, bm_ref):
    W = WINDOW
    first = lax.broadcasted_iota(jnp.int32, (W, 2 * W), 1) >= W
    for h in range(ATTN_HEADS):
        band_h = pltpu.roll(jnp.broadcast_to(ur_ref[h:h + 1, :], (W, 2 * W)), W, 1,
                            stride=1, stride_axis=0)
        bm_ref[1, h] = band_h
        bm_ref[0, h] = jnp.where(first, band_h, -jnp.inf)


def _attn_body(sink_ref, q_ref, kvc_ref, kvp_ref, o_ref, bm_ref, slab):
    W = WINDOW
    lane_lo2 = lax.broadcasted_iota(jnp.int32, (2 * W, LANES), 1) < ATTN_HEADDIM
    lane_lo = lax.broadcasted_iota(jnp.int32, (W, LANES), 1) < ATTN_HEADDIM

    def band(lo_col):
        return jnp.concatenate([kvp_ref[:, lo_col:lo_col + KV_COLS].astype(F32),
                                kvc_ref[:, lo_col:lo_col + KV_COLS].astype(F32)], axis=0)

    kband = band(0) * (ATTN_HEADDIM ** -0.5)
    vband = band(KV_COLS)
    ksw = pltpu.roll(kband, ATTN_HEADDIM, 1)
    vsw = pltpu.roll(vband, ATTN_HEADDIM, 1)
    k_var = ((jnp.where(lane_lo2, kband, 0.0).astype(BF16), jnp.where(lane_lo2, 0.0, ksw).astype(BF16)),
             (jnp.where(lane_lo2, ksw, 0.0).astype(BF16), jnp.where(lane_lo2, 0.0, kband).astype(BF16)))
    v_bd = (jnp.concatenate([jnp.where(lane_lo2, vband, 0.0), jnp.where(lane_lo2, 0.0, vsw)], axis=0).astype(BF16),
            jnp.concatenate([jnp.where(lane_lo2, vsw, 0.0), jnp.where(lane_lo2, 0.0, vband)], axis=0).astype(BF16))

    def kv_head(i):
        return (2 * i) // (ATTN_HEADS // ATTN_KV_HEADS)

    def logits(i):
        qp = q_ref[:, i * LANES:(i + 1) * LANES]
        return [_dot_nt(qp, k_var[kv_head(i)][j]) for j in range(2)]

    n_pairs = ATTN_HEADS // 2
    pending = {i: logits(i) for i in range(ATTN_LOOKAHEAD)}
    for i in range(n_pairs):
        c = kv_head(i)
        if i + ATTN_LOOKAHEAD < n_pairs:
            pending[i + ATTN_LOOKAHEAD] = logits(i + ATTN_LOOKAHEAD)
        s_cur = pending.pop(i)
        ps, rs = [], []
        for j in range(2):
            h = 2 * i + j
            sink = sink_ref[h]
            s = s_cur[j] + bm_ref[slab, h]
            m = jnp.maximum(jnp.max(s, axis=-1, keepdims=True), sink)
            p = jnp.exp(s - m)
            den = jnp.sum(p, axis=-1, keepdims=True) + jnp.exp(sink - m)
            ps.append(p.astype(BF16))
            rs.append(1.0 / den)
        pv = _dot(jnp.concatenate(ps, axis=1), v_bd[c])
        o_ref[:, i * LANES:(i + 1) * LANES] = (pv * jnp.where(lane_lo, rs[0], rs[1])).astype(BF16)
        yield


def _merge_body(ys_ref, ya_ref, g_ref, x_ref, wbs_ref, wba_ref, wmix_ref, lg_ref, lb_ref, h_ref):
    chunk = 256
    ys, ya = ys_ref[...], ya_ref[...]
    merged = []
    for c in range(0, D_MODEL, chunk):
        a = _dot(ys, wbs_ref[:, c:c + chunk])
        b = _dot(ya, wba_ref[:, c:c + chunk])
        merged.append((g_ref[:, c:c + chunk].astype(F32) * a
                       + g_ref[:, D_MODEL + c:D_MODEL + c + chunk].astype(F32) * b).astype(BF16))
        yield
    merged = jnp.concatenate(merged, axis=1)
    mix = []
    for c in range(0, D_MODEL, chunk):
        mix.append(_dot(merged, wmix_ref[:, c:c + chunk]))
        yield
    h_ref[...] = _layer_norm(DEEPNORM_ALPHA * x_ref[...] + jnp.concatenate(mix, axis=1), lg_ref[...], lb_ref[...])


def _mixer_kernel(sink_ref, conv_ref, zs_ref, dt_ref, alog_ref, dexp_ref, nw_ref, e3_ref, ec3_ref,
                  q_ref, kvc_ref, kvp_ref, ur_ref, g_ref, x_ref, wbs_ref, wba_ref, wmix_ref, lg_ref, lb_ref,
                  h_ref, state_ref, bm_ref, ys_ref, ya_ref):
    step = pl.program_id(0)

    @pl.when(step == 0)
    def _():
        state_ref[...] = jnp.zeros_like(state_ref)
        ys_ref[...] = jnp.zeros_like(ys_ref)
        ya_ref[...] = jnp.zeros_like(ya_ref)
        _build_bias_mask(ur_ref, bm_ref)

    L = SSM_CHUNK
    slot = lax.rem(step, 2)
    live = []
    for blk in range(conv_ref.shape[0] // L):
        rows = pl.ds(blk * L, L)
        live.append(_ssd_body(conv_ref.at[rows], zs_ref.at[rows], dt_ref.at[rows], alog_ref, dexp_ref, nw_ref,
                              e3_ref, ec3_ref, ys_ref.at[slot, rows], state_ref))
        prev = kvp_ref if blk == 0 else kvc_ref.at[pl.ds((blk - 1) * L, L)]
        slab = jnp.minimum(step, 1) if blk == 0 else 1
        live.append(_attn_body(sink_ref, q_ref.at[rows], kvc_ref.at[rows], prev, ya_ref.at[slot, rows],
                               bm_ref, slab))
    live.append(_merge_body(ys_ref.at[1 - slot], ya_ref.at[1 - slot], g_ref, x_ref, wbs_ref, wba_ref, wmix_ref,
                            lg_ref, lb_ref, h_ref))
    while live:
        for body in list(live):
            if next(body, "done") == "done":
                live.remove(body)


def _mixer(xc, zs, dt, qkv, gates, x2d, a_log, d_exp, norm_w, bias_rows, sinks, w_bs, w_ba, w_mix, ln_g, ln_b, tm):
    s = xc.shape[0]
    n = s // tm
    e3, ec3 = _expansion_matrices()
    cur = lambda i: (jnp.minimum(i, n - 1), 0)
    prev = lambda i: (jnp.maximum(i - 1, 0), 0)
    const = lambda i: (0, 0)
    full = lambda a: pl.BlockSpec(a.shape, const)
    once = lambda a: pl.BlockSpec(a.shape, const, pipeline_mode=pl.Buffered(1))
    kv_blk = Q_COLS // (2 * KV_COLS)
    return pl.pallas_call(
        _mixer_kernel,
        grid=(n + 1,),
        in_specs=[pl.BlockSpec(memory_space=pltpu.SMEM),
                  pl.BlockSpec((tm, SSM_CONV_DIM), cur),
                  pl.BlockSpec((tm, SSM_D_INNER), cur),
                  pl.BlockSpec((tm, DT_PAD), cur),
                  full(a_log), full(d_exp), full(norm_w), full(e3), full(ec3),
                  pl.BlockSpec((tm, Q_COLS), cur),
                  pl.BlockSpec((tm, 2 * KV_COLS), lambda i: (jnp.minimum(i, n - 1), kv_blk)),
                  pl.BlockSpec((WINDOW, 2 * KV_COLS),
                               lambda i: (jnp.maximum(jnp.minimum(i, n - 1) * (tm // WINDOW) - 1, 0), kv_blk)),
                  full(bias_rows),
                  pl.BlockSpec((tm, GATE_COLS), prev),
                  pl.BlockSpec((tm, D_MODEL), prev),
                  once(w_bs), once(w_ba), once(w_mix), full(ln_g), full(ln_b)],
        out_specs=pl.BlockSpec((tm, D_MODEL), prev),
        out_shape=jax.ShapeDtypeStruct((s, D_MODEL), F32),
        scratch_shapes=[pltpu.VMEM((SSM_GROUPS, SSM_STATE, GROUP_WIDTH), F32),
                        pltpu.VMEM((2, ATTN_HEADS, WINDOW, 2 * WINDOW), F32),
                        pltpu.VMEM((2, tm, SSM_D_INNER), BF16),
                        pltpu.VMEM((2, tm, Q_COLS), BF16)],
        compiler_params=pltpu.CompilerParams(dimension_semantics=("arbitrary",),
                                             vmem_limit_bytes=VMEM_LIMIT),
        name="mixer",
    )(sinks, xc, zs, dt, a_log, d_exp, norm_w, e3, ec3, qkv, qkv, qkv, bias_rows,
      gates, x2d, w_bs, w_ba, w_mix, ln_g, ln_b)


def _rel_bucket_static(n):
    max_exact = REL_BUCKETS // 2
    nf = np.maximum(n, 1).astype(np.float32)
    large = max_exact + (np.log(nf / max_exact) / math.log(REL_MAX_DIST / max_exact)
                         * (REL_BUCKETS - max_exact)).astype(np.int32)
    return np.where(n < max_exact, n, np.minimum(large, REL_BUCKETS - 1))


def _bias_rows(rel_bias):
    rel = (-np.arange(2 * WINDOW)) % (2 * WINDOW)
    idx = np.where(rel < WINDOW, _rel_bucket_static(rel), REL_BUCKETS)
    table = jnp.concatenate([rel_bias.astype(F32), jnp.full((1, ATTN_HEADS), -jnp.inf, F32)], axis=0)
    return table[idx].T


def _ffn_kernel(h_ref, wup_ref, cw_ref, cb_ref, wdn_ref, lg_ref, lb_ref, o_ref, halo_ref, act_ref):
    tm = h_ref.shape[0]

    @pl.when(pl.program_id(0) == 0)
    def _():
        halo_ref[...] = jnp.zeros_like(halo_ref)

    chunk = 256
    rb = 2 * ROW_BLOCK
    row_blocks = [slice(r, r + rb) for r in range(0, tm, rb)]
    hbs = [h_ref[rows, :].astype(BF16) for rows in row_blocks]

    def conv_cols(hb, cols, prev):
        u = _dot(hb, wup_ref[:, cols])
        return _causal_conv(u, prev, cw_ref, cb_ref, cols, FFN_CONV), u[rb - HALO_ROWS:, :]

    for c in range(0, D_FF, chunk):
        gcols, vcols = slice(c, c + chunk), slice(D_FF + c, D_FF + c + chunk)
        gprev, vprev = halo_ref[:, gcols], halo_ref[:, vcols]
        for hb, rows in zip(hbs, row_blocks):
            gate, gprev = conv_cols(hb, gcols, gprev)
            val, vprev = conv_cols(hb, vcols, vprev)
            act_ref[rows, c:c + chunk] = (_silu(gate) * val).astype(BF16)
        halo_ref[:, gcols] = gprev
        halo_ref[:, vcols] = vprev

    for rows in row_blocks:
        out = _dot(act_ref[rows, :], wdn_ref[...])
        o_ref[rows, :] = _layer_norm(DEEPNORM_ALPHA * h_ref[rows, :] + out, lg_ref[...], lb_ref[...])


def _ffn(h1, w_up, conv_w, conv_b, w_down, ln_g, ln_b, tm):
    s = h1.shape[0]
    row = lambda i: (i, 0)
    const = lambda i: (0, 0)
    full = lambda a: pl.BlockSpec(a.shape, const)
    return pl.pallas_call(
        _ffn_kernel,
        grid=(s // tm,),
        in_specs=[pl.BlockSpec((tm, D_MODEL), row),
                  pl.BlockSpec(w_up.shape, const, pipeline_mode=pl.Buffered(1)),
                  full(conv_w), full(conv_b),
                  pl.BlockSpec(w_down.shape, const, pipeline_mode=pl.Buffered(1)),
                  full(ln_g), full(ln_b)],
        out_specs=pl.BlockSpec((tm, D_MODEL), row),
        out_shape=jax.ShapeDtypeStruct((s, D_MODEL), F32),
        scratch_shapes=[pltpu.VMEM((HALO_ROWS, 2 * D_FF), F32),
                        pltpu.VMEM((tm, D_FF), BF16)],
        compiler_params=pltpu.CompilerParams(dimension_semantics=("arbitrary",),
                                             vmem_limit_bytes=VMEM_LIMIT),
        name="ffn",
    )(h1, w_up, conv_w, conv_b, w_down, ln_g, ln_b)


PREP_BLOCK = 512
DT_LO = SSM_D_INNER + SSM_CONV_DIM
DT_DST = PROJ_COLS - DT_PAD


def _permute_kernel(a_ref, b_ref, d_ref, o_ref):
    j = pl.program_id(0)
    first_shifted = DT_LO // PREP_BLOCK
    last = (PROJ_COLS - 1) // PREP_BLOCK

    @pl.when(j < first_shifted)
    def _():
        o_ref[...] = a_ref[...].T.astype(BF16)

    @pl.when((j >= first_shifted) & (j < last))
    def _():
        rows = jnp.concatenate([a_ref[SSM_HEADS:, :], b_ref[:SSM_HEADS, :]], axis=0)
        o_ref[...] = rows.T.astype(BF16)

    @pl.when(j == last)
    def _():
        n_tail = DT_DST - last * PREP_BLOCK
        rows = jnp.concatenate([a_ref[SSM_HEADS:SSM_HEADS + n_tail, :], d_ref[:SSM_HEADS, :],
                                jnp.zeros((DT_PAD - SSM_HEADS, D_MODEL), F32)], axis=0)
        o_ref[:, :n_tail + DT_PAD] = rows.T.astype(BF16)


def _permute_w_in(w_t):
    n_src = DT_LO + SSM_HEADS + QKV_COLS + GATE_COLS
    assert DT_LO % PREP_BLOCK == 0 and w_t.shape == (n_src, D_MODEL)
    first_shifted = DT_LO // PREP_BLOCK
    last_src = (n_src - 1) // PREP_BLOCK
    blk = lambda f: pl.BlockSpec((PREP_BLOCK, D_MODEL), f)
    return pl.pallas_call(
        _permute_kernel,
        grid=(pl.cdiv(PROJ_COLS, PREP_BLOCK),),
        in_specs=[blk(lambda j: (jnp.minimum(j, last_src), 0)),
                  blk(lambda j: (jnp.clip(j + 1, first_shifted, last_src), 0)),
                  blk(lambda j: (first_shifted, 0))],
        out_specs=pl.BlockSpec((D_MODEL, PREP_BLOCK), lambda j: (0, j)),
        out_shape=jax.ShapeDtypeStruct((D_MODEL, PROJ_COLS), BF16),
        compiler_params=pltpu.CompilerParams(dimension_semantics=("arbitrary",),
                                             vmem_limit_bytes=VMEM_LIMIT),
        name="permute_w_in",
    )(w_t, w_t, w_t)


def _row(v, width=None):
    v = v.astype(F32).reshape(1, -1)
    if width is not None and v.shape[1] < width:
        v = jnp.pad(v, ((0, 0), (0, width - v.shape[1])))
    return v


def kernel(x, rel_bias, w_in, b_gate, ssm_conv_w, ssm_conv_b, ssm_dt_bias, ssm_a_log, ssm_d, ssm_norm_w,
           attn_sinks, w_branch_ssm, w_branch_attn, w_mix_out, ln1_g, ln1_b, w_up, ffn_conv_w, ffn_conv_b,
           w_down, ln2_g, ln2_b):
    b, s, d = x.shape
    assert (b, d) == (1, D_MODEL) and s % 512 == 0 and w_in.shape[0] == DEPTH
    t = _tiles()
    h = x.reshape(s, d)
    bias_rows = _bias_rows(rel_bias)
    for l in range(DEPTH):
        zs, xc, qkv, gates, dt = _inproj(h, _permute_w_in(w_in[l].T), ssm_conv_w[l].astype(F32),
                                         _row(ssm_conv_b[l]), _row(ssm_dt_bias[l], DT_PAD),
                                         _row(b_gate[l]), t["inproj"])
        h1 = _mixer(xc, zs, dt, qkv, gates, h, _row(ssm_a_log[l], DT_PAD),
                    _row(jnp.repeat(ssm_d[l], SSM_HEADDIM)), _row(ssm_norm_w[l]),
                    bias_rows, attn_sinks[l].astype(F32),
                    w_branch_ssm[l].astype(BF16), w_branch_attn[l].astype(BF16), w_mix_out[l].astype(BF16),
                    _row(ln1_g[l]), _row(ln1_b[l]), t["mixer"])
        h = _ffn(h1, w_up[l].astype(BF16), ffn_conv_w[l].astype(F32), _row(ffn_conv_b[l]),
                 w_down[l].astype(BF16), _row(ln2_g[l]), _row(ln2_b[l]), t["ffn"])
    return h.reshape(b, s, d)
```

```python
import math

import numpy as np
import jax
import jax.numpy as jnp
from jax import lax
from jax.experimental import pallas as pl
from jax.experimental.pallas import tpu as pltpu

F32 = jnp.float32
BF16 = jnp.bfloat16

D_MODEL = 1024
SSM_D_INNER = 2048
SSM_HEADDIM = 64
SSM_HEADS = 32
SSM_GROUPS = 4
SSM_HEADS_PER_GROUP = 8
SSM_STATE = 128
SSM_CONV = 4
SSM_CHUNK = 128
SSM_CONV_DIM = SSM_D_INNER + 2 * SSM_GROUPS * SSM_STATE
GROUP_WIDTH = SSM_HEADS_PER_GROUP * SSM_HEADDIM
ATTN_HEADS = 16
ATTN_KV_HEADS = 2
ATTN_HEADDIM = 64
WINDOW = 128
REL_BUCKETS = 32
REL_MAX_DIST = 128
Q_COLS = ATTN_HEADS * ATTN_HEADDIM
KV_COLS = ATTN_KV_HEADS * ATTN_HEADDIM
QKV_COLS = Q_COLS + 2 * KV_COLS
GATE_COLS = 2 * D_MODEL
D_FF = 2816
FFN_CONV = 3
DEPTH = 1
DEEPNORM_ALPHA = (2.0 * DEPTH) ** 0.25
LN_EPS = 1e-5
RMS_EPS = 1e-5

LANES = 128
SUBLANES = 8
DT_PAD = LANES
HALO_ROWS = SUBLANES
SSD_LOOKAHEAD = 1
ATTN_LOOKAHEAD = 1
ROW_BLOCK = 128
VMEM_LIMIT = 56 * 1024 * 1024

PROJ_SEGMENTS = (SSM_D_INNER, SSM_CONV_DIM, QKV_COLS, GATE_COLS, DT_PAD)
PROJ_COLS = sum(PROJ_SEGMENTS)

NEG_BIG = -1e30
LOG2E = 1.4426950408889634


def _tiles():
    assert SSM_CHUNK == WINDOW
    return dict(inproj=256, mixer=2 * SSM_CHUNK, merge=256, ffn=512)


def _dot(a, b):
    return jnp.dot(a, b, preferred_element_type=F32)


def _dot_nt(a, b):
    return lax.dot_general(a, b, (((1,), (1,)), ((), ())), preferred_element_type=F32)


def _sigmoid(x):
    return 1.0 / (1.0 + jnp.exp2(x * (-LOG2E)))


def _silu(x):
    return x * _sigmoid(x)


def _layer_norm(r, g, b):
    mu = jnp.mean(r, axis=-1, keepdims=True)
    rc = r - mu
    var = jnp.mean(rc * rc, axis=-1, keepdims=True)
    return rc * lax.rsqrt(var + LN_EPS) * g + b


def _shift_rows(x, prev, j):
    row = lax.broadcasted_iota(jnp.int32, prev.shape, 0)
    sh = pltpu.roll(x, j, 0)
    top = jnp.where(row < j, pltpu.roll(prev, j, 0), sh[:HALO_ROWS])
    return jnp.concatenate([top, sh[HALO_ROWS:]], axis=0)


def _causal_conv(x, prev, w_ref, b_ref, cols, taps):
    acc = b_ref[:, cols] + w_ref[taps - 1:taps, cols] * x
    for j in range(1, taps):
        acc = acc + w_ref[taps - 1 - j:taps - j, cols] * _shift_rows(x, prev, j)
    return acc


def _inproj_kernel(x_ref, w_ref, cw_ref, cb_ref, dtb_ref, bg_ref,
                   zs_ref, xc_ref, qkv_ref, g_ref, dt_ref, halo_ref):
    tm = x_ref.shape[0]

    @pl.when(pl.program_id(0) == 0)
    def _():
        halo_ref[...] = jnp.zeros_like(halo_ref)

    chunk = 256
    row_blocks = [slice(r, r + ROW_BLOCK) for r in range(0, tm, ROW_BLOCK)]
    xbs = [x_ref[rows, :].astype(BF16) for rows in row_blocks]
    offs = [sum(PROJ_SEGMENTS[:i]) for i in range(len(PROJ_SEGMENTS))]

    def proj(i, seg, c, w):
        return _dot(xbs[i], w_ref[:, offs[seg] + c:offs[seg] + c + w])

    for c in range(0, SSM_D_INNER, chunk):
        for i, rows in enumerate(row_blocks):
            zs_ref[rows, c:c + chunk] = _silu(proj(i, 0, c, chunk)).astype(BF16)
    for c in range(0, SSM_CONV_DIM, chunk):
        cols = slice(c, c + chunk)
        prev = halo_ref[:, cols]
        for i, rows in enumerate(row_blocks):
            u = proj(i, 1, c, chunk)
            xc_ref[rows, cols] = _silu(_causal_conv(u, prev, cw_ref, cb_ref, cols, SSM_CONV)).astype(BF16)
            prev = u[ROW_BLOCK - HALO_ROWS:, :]
        halo_ref[:, cols] = prev
    for c in range(0, QKV_COLS, chunk):
        w = min(chunk, QKV_COLS - c)
        for i, rows in enumerate(row_blocks):
            qkv_ref[rows, c:c + w] = proj(i, 2, c, w).astype(BF16)
    for c in range(0, GATE_COLS, chunk):
        for i, rows in enumerate(row_blocks):
            g_ref[rows, c:c + chunk] = _sigmoid(proj(i, 3, c, chunk) + bg_ref[:, c:c + chunk]).astype(BF16)
    lane_ok = lax.broadcasted_iota(jnp.int32, (ROW_BLOCK, DT_PAD), 1) < SSM_HEADS
    for i, rows in enumerate(row_blocks):
        dtr = proj(i, 4, 0, DT_PAD) + dtb_ref[...]
        softplus = jnp.maximum(dtr, 0.0) + jnp.log(1.0 + jnp.exp(-jnp.abs(dtr)))
        dt_ref[rows, :] = jnp.where(lane_ok, softplus, 0.0)


def _inproj(x2d, w_perm, conv_w, conv_b, dt_bias, b_gate, tm):
    s = x2d.shape[0]
    row = lambda i: (i, 0)
    const = lambda i: (0, 0)
    full = lambda a: pl.BlockSpec(a.shape, const)
    outs = [jax.ShapeDtypeStruct((s, n), dt) for n, dt in
            zip(PROJ_SEGMENTS, (BF16, BF16, BF16, BF16, F32))]
    return pl.pallas_call(
        _inproj_kernel,
        grid=(s // tm,),
        in_specs=[pl.BlockSpec((tm, D_MODEL), row),
                  pl.BlockSpec((D_MODEL, PROJ_COLS), const, pipeline_mode=pl.Buffered(1)),
                  full(conv_w), full(conv_b), full(dt_bias), full(b_gate)],
        out_specs=[pl.BlockSpec((tm, n), row) for n in PROJ_SEGMENTS],
        out_shape=outs,
        scratch_shapes=[pltpu.VMEM((HALO_ROWS, SSM_CONV_DIM), F32)],
        compiler_params=pltpu.CompilerParams(dimension_semantics=("arbitrary",),
                                             vmem_limit_bytes=VMEM_LIMIT),
        name="inproj",
    )(x2d, w_perm, conv_w, conv_b, dt_bias, b_gate)


def _split3(x):
    hi = x.astype(BF16).astype(F32)
    r = x - hi
    mid = r.astype(BF16).astype(F32)
    lo = (r - mid).astype(BF16).astype(F32)
    return hi, mid, lo


def _pack3(x):
    hi, mid, lo = _split3(x)
    return (hi + pltpu.roll(mid, SSM_HEADS, 1) + pltpu.roll(lo, 2 * SSM_HEADS, 1)).astype(BF16)


def _ssd_body(conv_ref, zs_ref, dt_ref, alog_ref, dexp_ref, nw_ref, e3_ref, ec3_ref, y_ref, state_ref):
    L = SSM_CHUNK
    lane_ok = lax.broadcasted_iota(jnp.int32, (L, DT_PAD), 1) < SSM_HEADS
    dt = dt_ref[...]
    a_dt = dt * (-LOG2E * jnp.exp(alog_ref[...]))

    ri = lax.broadcasted_iota(jnp.int32, (L, L), 0)
    ci = lax.broadcasted_iota(jnp.int32, (L, L), 1)
    causal = ri >= ci
    tril = jnp.where(causal, 1.0, 0.0).astype(BF16)
    hi, mid, lo = _split3(a_dt)
    a_cs = _dot(tril, hi.astype(BF16)) + _dot(tril, mid.astype(BF16)) + _dot(tril, lo.astype(BF16))

    a_last = a_cs[L - 1:L, :]
    decay_out_b = jnp.where(lane_ok, jnp.exp2(a_cs), 0.0).astype(BF16)
    w_state_b = (dt * jnp.exp2(a_last - a_cs)).astype(BF16)
    cd_e = jnp.exp2(_dot(_pack3(a_cs[L - SUBLANES:, :]), e3_ref[...])[SUBLANES - 1:, :])

    a_sub_t = (a_cs - jnp.log2(dt)).T
    acs3 = _pack3(a_cs)
    lane_lo = lax.broadcasted_iota(jnp.int32, (L, LANES), 1) < SSM_HEADDIM

    def group_start(g):
        gsl = slice(g * GROUP_WIDTH, (g + 1) * GROUP_WIDTH)
        xs_g = conv_ref[:, gsl].astype(F32)
        b_off = SSM_D_INNER + g * SSM_STATE
        c_off = SSM_D_INNER + SSM_GROUPS * SSM_STATE + g * SSM_STATE
        b_g = conv_ref[:, b_off:b_off + SSM_STATE]
        c_g = conv_ref[:, c_off:c_off + SSM_STATE]
        cb = _dot_nt(c_g, b_g)
        xdtd_b = (xs_g * _dot(w_state_b, e3_ref[:, gsl])).astype(BF16)
        st = state_ref[g]
        y_off = _dot(c_g, st.astype(BF16)) * _dot(decay_out_b, e3_ref[:, gsl])
        state_ref[g] = st * cd_e[:, gsl] + _dot(b_g.astype(F32).T.astype(BF16), xdtd_b)
        return cb, y_off + xs_g * dexp_ref[:, gsl]

    def acol_dot(p):
        return _dot(acs3, ec3_ref[:, 2 * p * L:(2 * p + 2) * L])

    pairs_per_group = SSM_HEADS_PER_GROUP // 2
    n_pairs = SSM_HEADS // 2
    started = {0: group_start(0)}
    acols = {p: acol_dot(p) for p in range(SSD_LOOKAHEAD)}
    pieces = []
    for p in range(n_pairs):
        g, pr = divmod(p, pairs_per_group)
        if p + SSD_LOOKAHEAD < n_pairs:
            acols[p + SSD_LOOKAHEAD] = acol_dot(p + SSD_LOOKAHEAD)
        if pr == pairs_per_group - 2 and g + 1 < SSM_GROUPS:
            started[g + 1] = group_start(g + 1)
        cb, y_rest = started[g]
        acol = acols.pop(p)
        ms = []
        for k in range(2):
            seg = acol[:, k * L:(k + 1) * L] - a_sub_t[2 * p + k:2 * p + k + 1, :]
            ms.append((cb * jnp.exp2(jnp.where(causal, seg, NEG_BIG))).astype(BF16))
        xp = conv_ref[:, p * LANES:(p + 1) * LANES]
        zero = jnp.zeros_like(xp)
        rhs = jnp.concatenate([jnp.where(lane_lo, xp, zero), jnp.where(lane_lo, zero, xp)], axis=0)
        pieces.append(_dot(jnp.concatenate(ms, axis=1), rhs))
        if pr == pairs_per_group - 1:
            gsl = slice(g * GROUP_WIDTH, (g + 1) * GROUP_WIDTH)
            yz = (jnp.concatenate(pieces, axis=1) + y_rest) * zs_ref[:, gsl].astype(F32)
            ms_ = jnp.mean(yz * yz, axis=-1, keepdims=True)
            y_ref[:, gsl] = (yz * lax.rsqrt(ms_ + RMS_EPS) * nw_ref[:, gsl]).astype(BF16)
            pieces = []
            del started[g]
        if p % 2 == 1:
            yield


def _expansion_matrices():
    k = np.arange(LANES)[:, None]
    valid = k < 3 * SSM_HEADS
    head = k % SSM_HEADS
    e3 = (valid & (head == (np.arange(SSM_D_INNER)[None, :] // SSM_HEADDIM))).astype(np.float32)
    ec3 = (valid & (head == (np.arange(SSM_HEADS * SSM_CHUNK)[None, :] // SSM_CHUNK))).astype(np.float32)
    return jnp.asarray(e3, BF16), jnp.asarray(ec3, BF16)


def _build_bias_mask(ur_ref, bm_ref):
    W = WINDOW
    first = lax.broadcasted_iota(jnp.int32, (W, 2 * W), 1) >= W
    for h in range(ATTN_HEADS):
        band_h = LOG2E * pltpu.roll(jnp.broadcast_to(ur_ref[h:h + 1, :], (W, 2 * W)), W, 1,
                                    stride=1, stride_axis=0)
        bm_ref[1, h] = band_h
        bm_ref[0, h] = jnp.where(first, band_h, -jnp.inf)


def _attn_body(sink_ref, q_ref, kvc_ref, kvp_ref, o_ref, bm_ref, slab):
    W = WINDOW
    lane_lo2 = lax.broadcasted_iota(jnp.int32, (2 * W, LANES), 1) < ATTN_HEADDIM
    lane_lo = lax.broadcasted_iota(jnp.int32, (W, LANES), 1) < ATTN_HEADDIM

    def band(lo_col):
        return jnp.concatenate([kvp_ref[:, lo_col:lo_col + KV_COLS].astype(F32),
                                kvc_ref[:, lo_col:lo_col + KV_COLS].astype(F32)], axis=0)

    kband = band(0) * (ATTN_HEADDIM ** -0.5 * LOG2E)
    vband = band(KV_COLS)
    ksw = pltpu.roll(kband, ATTN_HEADDIM, 1)
    vsw = pltpu.roll(vband, ATTN_HEADDIM, 1)
    k_var = ((jnp.where(lane_lo2, kband, 0.0).astype(BF16), jnp.where(lane_lo2, 0.0, ksw).astype(BF16)),
             (jnp.where(lane_lo2, ksw, 0.0).astype(BF16), jnp.where(lane_lo2, 0.0, kband).astype(BF16)))
    v_bd = (jnp.concatenate([jnp.where(lane_lo2, vband, 0.0), jnp.where(lane_lo2, 0.0, vsw)], axis=0).astype(BF16),
            jnp.concatenate([jnp.where(lane_lo2, vsw, 0.0), jnp.where(lane_lo2, 0.0, vband)], axis=0).astype(BF16))

    def kv_head(i):
        return (2 * i) // (ATTN_HEADS // ATTN_KV_HEADS)

    def logits(i):
        qp = q_ref[:, i * LANES:(i + 1) * LANES]
        return [_dot_nt(qp, k_var[kv_head(i)][j]) for j in range(2)]

    n_pairs = ATTN_HEADS // 2
    pending = {i: logits(i) for i in range(ATTN_LOOKAHEAD)}
    for i in range(n_pairs):
        c = kv_head(i)
        if i + ATTN_LOOKAHEAD < n_pairs:
            pending[i + ATTN_LOOKAHEAD] = logits(i + ATTN_LOOKAHEAD)
        s_cur = pending.pop(i)
        ps, rs = [], []
        for j in range(2):
            h = 2 * i + j
            sink = sink_ref[h] * LOG2E
            s = s_cur[j] + bm_ref[slab, h]
            m = jnp.maximum(jnp.max(s, axis=-1, keepdims=True), sink)
            p = jnp.exp2(s - m)
            den = jnp.sum(p, axis=-1, keepdims=True) + jnp.exp2(sink - m)
            ps.append(p.astype(BF16))
            rs.append(1.0 / den)
        pv = _dot(jnp.concatenate(ps, axis=1), v_bd[c])
        o_ref[:, i * LANES:(i + 1) * LANES] = (pv * jnp.where(lane_lo, rs[0], rs[1])).astype(BF16)
        yield


def _mixer_kernel(sink_ref, conv_ref, zs_ref, dt_ref, alog_ref, dexp_ref, nw_ref, e3_ref, ec3_ref,
                  q_ref, kvc_ref, kvp_ref, ur_ref, ys_ref, ya_ref, state_ref, bm_ref):
    @pl.when(pl.program_id(0) == 0)
    def _():
        state_ref[...] = jnp.zeros_like(state_ref)
        _build_bias_mask(ur_ref, bm_ref)

    L = SSM_CHUNK
    live = []
    for blk in range(conv_ref.shape[0] // L):
        rows = pl.ds(blk * L, L)
        live.append(_ssd_body(conv_ref.at[rows], zs_ref.at[rows], dt_ref.at[rows], alog_ref, dexp_ref, nw_ref,
                              e3_ref, ec3_ref, ys_ref.at[rows], state_ref))
        prev = kvp_ref if blk == 0 else kvc_ref.at[pl.ds((blk - 1) * L, L)]
        slab = jnp.minimum(pl.program_id(0), 1) if blk == 0 else 1
        live.append(_attn_body(sink_ref, q_ref.at[rows], kvc_ref.at[rows], prev, ya_ref.at[rows], bm_ref, slab))
    while live:
        for body in list(live):
            if next(body, "done") == "done":
                live.remove(body)


def _mixer(xc, zs, dt, qkv, a_log, d_exp, norm_w, bias_rows, sinks, tm):
    s = xc.shape[0]
    e3, ec3 = _expansion_matrices()
    row = lambda i: (i, 0)
    const = lambda i: (0, 0)
    full = lambda a: pl.BlockSpec(a.shape, const)
    kv_blk = Q_COLS // (2 * KV_COLS)
    return pl.pallas_call(
        _mixer_kernel,
        grid=(s // tm,),
        in_specs=[pl.BlockSpec(memory_space=pltpu.SMEM),
                  pl.BlockSpec((tm, SSM_CONV_DIM), row),
                  pl.BlockSpec((tm, SSM_D_INNER), row),
                  pl.BlockSpec((tm, DT_PAD), row),
                  full(a_log), full(d_exp), full(norm_w), full(e3), full(ec3),
                  pl.BlockSpec((tm, Q_COLS), row),
                  pl.BlockSpec((tm, 2 * KV_COLS), lambda i: (i, kv_blk)),
                  pl.BlockSpec((WINDOW, 2 * KV_COLS),
                               lambda i: (jnp.maximum(i * (tm // WINDOW) - 1, 0), kv_blk)),
                  full(bias_rows)],
        out_specs=[pl.BlockSpec((tm, SSM_D_INNER), row), pl.BlockSpec((tm, Q_COLS), row)],
        out_shape=[jax.ShapeDtypeStruct((s, SSM_D_INNER), BF16), jax.ShapeDtypeStruct((s, Q_COLS), BF16)],
        scratch_shapes=[pltpu.VMEM((SSM_GROUPS, SSM_STATE, GROUP_WIDTH), F32),
                        pltpu.VMEM((2, ATTN_HEADS, WINDOW, 2 * WINDOW), F32)],
        compiler_params=pltpu.CompilerParams(dimension_semantics=("arbitrary",),
                                             vmem_limit_bytes=VMEM_LIMIT),
        name="mixer",
    )(sinks, xc, zs, dt, a_log, d_exp, norm_w, e3, ec3, qkv, qkv, qkv, bias_rows)


def _rel_bucket_static(n):
    max_exact = REL_BUCKETS // 2
    nf = np.maximum(n, 1).astype(np.float32)
    large = max_exact + (np.log(nf / max_exact) / math.log(REL_MAX_DIST / max_exact)
                         * (REL_BUCKETS - max_exact)).astype(np.int32)
    return np.where(n < max_exact, n, np.minimum(large, REL_BUCKETS - 1))


def _bias_rows(rel_bias):
    rel = (-np.arange(2 * WINDOW)) % (2 * WINDOW)
    idx = np.where(rel < WINDOW, _rel_bucket_static(rel), REL_BUCKETS)
    table = jnp.concatenate([rel_bias.astype(F32), jnp.full((1, ATTN_HEADS), -jnp.inf, F32)], axis=0)
    return table[idx].T


def _merge_kernel(ys_ref, ya_ref, g_ref, x_ref, wbs_ref, wba_ref, wmix_ref, lg_ref, lb_ref, h_ref):
    chunk = 256
    rb = 2 * ROW_BLOCK
    for r in range(0, ys_ref.shape[0], rb):
        rows = slice(r, r + rb)
        ys, ya = ys_ref[rows, :], ya_ref[rows, :]
        merged = []
        for c in range(0, D_MODEL, chunk):
            a = _dot(ys, wbs_ref[:, c:c + chunk])
            b = _dot(ya, wba_ref[:, c:c + chunk])
            merged.append((g_ref[rows, c:c + chunk].astype(F32) * a
                           + g_ref[rows, D_MODEL + c:D_MODEL + c + chunk].astype(F32) * b).astype(BF16))
        mix = _dot(jnp.concatenate(merged, axis=1), wmix_ref[...])
        h_ref[rows, :] = _layer_norm(DEEPNORM_ALPHA * x_ref[rows, :] + mix, lg_ref[...], lb_ref[...])


def _merge(y_ssm, y_attn, gates, x2d, w_bs, w_ba, w_mix, ln_g, ln_b, tm):
    s = x2d.shape[0]
    row = lambda i: (i, 0)
    const = lambda i: (0, 0)
    full = lambda a: pl.BlockSpec(a.shape, const)
    return pl.pallas_call(
        _merge_kernel,
        grid=(s // tm,),
        in_specs=[pl.BlockSpec((tm, SSM_D_INNER), row), pl.BlockSpec((tm, Q_COLS), row),
                  pl.BlockSpec((tm, GATE_COLS), row), pl.BlockSpec((tm, D_MODEL), row),
                  full(w_bs), full(w_ba), full(w_mix), full(ln_g), full(ln_b)],
        out_specs=pl.BlockSpec((tm, D_MODEL), row),
        out_shape=jax.ShapeDtypeStruct((s, D_MODEL), F32),
        compiler_params=pltpu.CompilerParams(dimension_semantics=("arbitrary",),
                                             vmem_limit_bytes=VMEM_LIMIT),
        name="merge",
    )(y_ssm, y_attn, gates, x2d, w_bs, w_ba, w_mix, ln_g, ln_b)


def _ffn_kernel(h_ref, wup_ref, cw_ref, cb_ref, wdn_ref, lg_ref, lb_ref, o_ref, halo_ref, act_ref):
    tm = h_ref.shape[0]

    @pl.when(pl.program_id(0) == 0)
    def _():
        halo_ref[...] = jnp.zeros_like(halo_ref)

    chunk = 256
    rb = 2 * ROW_BLOCK
    row_blocks = [slice(r, r + rb) for r in range(0, tm, rb)]
    hbs = [h_ref[rows, :].astype(BF16) for rows in row_blocks]

    def conv_cols(hb, cols, prev):
        u = _dot(hb, wup_ref[:, cols])
        return _causal_conv(u, prev, cw_ref, cb_ref, cols, FFN_CONV), u[rb - HALO_ROWS:, :]

    for c in range(0, D_FF, chunk):
        gcols, vcols = slice(c, c + chunk), slice(D_FF + c, D_FF + c + chunk)
        gprev, vprev = halo_ref[:, gcols], halo_ref[:, vcols]
        for hb, rows in zip(hbs, row_blocks):
            gate, gprev = conv_cols(hb, gcols, gprev)
            val, vprev = conv_cols(hb, vcols, vprev)
            act_ref[rows, c:c + chunk] = (_silu(gate) * val).astype(BF16)
        halo_ref[:, gcols] = gprev
        halo_ref[:, vcols] = vprev

    for rows in row_blocks:
        out = _dot(act_ref[rows, :], wdn_ref[...])
        o_ref[rows, :] = _layer_norm(DEEPNORM_ALPHA * h_ref[rows, :] + out, lg_ref[...], lb_ref[...])


def _ffn(h1, w_up, conv_w, conv_b, w_down, ln_g, ln_b, tm):
    s = h1.shape[0]
    row = lambda i: (i, 0)
    const = lambda i: (0, 0)
    full = lambda a: pl.BlockSpec(a.shape, const)
    return pl.pallas_call(
        _ffn_kernel,
        grid=(s // tm,),
        in_specs=[pl.BlockSpec((tm, D_MODEL), row),
                  pl.BlockSpec(w_up.shape, const, pipeline_mode=pl.Buffered(1)),
                  full(conv_w), full(conv_b),
                  pl.BlockSpec(w_down.shape, const, pipeline_mode=pl.Buffered(1)),
                  full(ln_g), full(ln_b)],
        out_specs=pl.BlockSpec((tm, D_MODEL), row),
        out_shape=jax.ShapeDtypeStruct((s, D_MODEL), F32),
        scratch_shapes=[pltpu.VMEM((HALO_ROWS, 2 * D_FF), F32),
                        pltpu.VMEM((tm, D_FF), BF16)],
        compiler_params=pltpu.CompilerParams(dimension_semantics=("arbitrary",),
                                             vmem_limit_bytes=VMEM_LIMIT),
        name="ffn",
    )(h1, w_up, conv_w, conv_b, w_down, ln_g, ln_b)


PREP_BLOCK = 512
DT_LO = SSM_D_INNER + SSM_CONV_DIM
DT_DST = PROJ_COLS - DT_PAD


def _permute_kernel(a_ref, b_ref, d_ref, o_ref):
    j = pl.program_id(0)
    first_shifted = DT_LO // PREP_BLOCK
    last = (PROJ_COLS - 1) // PREP_BLOCK

    @pl.when(j < first_shifted)
    def _():
        o_ref[...] = a_ref[...].T.astype(BF16)

    @pl.when((j >= first_shifted) & (j < last))
    def _():
        rows = jnp.concatenate([a_ref[SSM_HEADS:, :], b_ref[:SSM_HEADS, :]], axis=0)
        o_ref[...] = rows.T.astype(BF16)

    @pl.when(j == last)
    def _():
        n_tail = DT_DST - last * PREP_BLOCK
        rows = jnp.concatenate([a_ref[SSM_HEADS:SSM_HEADS + n_tail, :], d_ref[:SSM_HEADS, :],
                                jnp.zeros((DT_PAD - SSM_HEADS, D_MODEL), F32)], axis=0)
        o_ref[:, :n_tail + DT_PAD] = rows.T.astype(BF16)


def _permute_w_in(w_t):
    n_src = DT_LO + SSM_HEADS + QKV_COLS + GATE_COLS
    assert DT_LO % PREP_BLOCK == 0 and w_t.shape == (n_src, D_MODEL)
    first_shifted = DT_LO // PREP_BLOCK
    last_src = (n_src - 1) // PREP_BLOCK
    blk = lambda f: pl.BlockSpec((PREP_BLOCK, D_MODEL), f)
    return pl.pallas_call(
        _permute_kernel,
        grid=(pl.cdiv(PROJ_COLS, PREP_BLOCK),),
        in_specs=[blk(lambda j: (jnp.minimum(j, last_src), 0)),
                  blk(lambda j: (jnp.clip(j + 1, first_shifted, last_src), 0)),
                  blk(lambda j: (first_shifted, 0))],
        out_specs=pl.BlockSpec((D_MODEL, PREP_BLOCK), lambda j: (0, j)),
        out_shape=jax.ShapeDtypeStruct((D_MODEL, PROJ_COLS), BF16),
        compiler_params=pltpu.CompilerParams(dimension_semantics=("arbitrary",),
                                             vmem_limit_bytes=VMEM_LIMIT),
        name="permute_w_in",
    )(w_t, w_t, w_t)


def _row(v, width=None):
    v = v.astype(F32).reshape(1, -1)
    if width is not None and v.shape[1] < width:
        v = jnp.pad(v, ((0, 0), (0, width - v.shape[1])))
    return v


def kernel(x, rel_bias, w_in, b_gate, ssm_conv_w, ssm_conv_b, ssm_dt_bias, ssm_a_log, ssm_d, ssm_norm_w,
           attn_sinks, w_branch_ssm, w_branch_attn, w_mix_out, ln1_g, ln1_b, w_up, ffn_conv_w, ffn_conv_b,
           w_down, ln2_g, ln2_b):
    b, s, d = x.shape
    assert (b, d) == (1, D_MODEL) and s % 512 == 0 and w_in.shape[0] == DEPTH
    t = _tiles()
    h = x.reshape(s, d)
    bias_rows = _bias_rows(rel_bias)
    for l in range(DEPTH):
        zs, xc, qkv, gates, dt = _inproj(h, _permute_w_in(w_in[l].T), ssm_conv_w[l].astype(F32),
                                         _row(ssm_conv_b[l]), _row(ssm_dt_bias[l], DT_PAD),
                                         _row(b_gate[l]), t["inproj"])
        y_ssm, y_attn = _mixer(xc, zs, dt, qkv, _row(ssm_a_log[l], DT_PAD),
                               _row(jnp.repeat(ssm_d[l], SSM_HEADDIM)), _row(ssm_norm_w[l]),
                               bias_rows, attn_sinks[l].astype(F32), t["mixer"])
        h1 = _merge(y_ssm, y_attn, gates, h, w_branch_ssm[l].astype(BF16), w_branch_attn[l].astype(BF16),
                    w_mix_out[l].astype(BF16), _row(ln1_g[l]), _row(ln1_b[l]), t["merge"])
        h = _ffn(h1, w_up[l].astype(BF16), ffn_conv_w[l].astype(F32), _row(ffn_conv_b[l]),
                 w_down[l].astype(BF16), _row(ln2_g[l]), _row(ln2_b[l]), t["ffn"])
    return h.reshape(b, s, d)
```

```python
import math

import numpy as np
import jax
import jax.numpy as jnp
from jax import lax
from jax.experimental import pallas as pl
from jax.experimental.pallas import tpu as pltpu

F32 = jnp.float32
BF16 = jnp.bfloat16

D_MODEL = 1024
SSM_D_INNER = 2048
SSM_HEADDIM = 64
SSM_HEADS = 32
SSM_GROUPS = 4
SSM_HEADS_PER_GROUP = 8
SSM_STATE = 128
SSM_CONV = 4
SSM_CHUNK = 128
SSM_CONV_DIM = SSM_D_INNER + 2 * SSM_GROUPS * SSM_STATE
GROUP_WIDTH = SSM_HEADS_PER_GROUP * SSM_HEADDIM
ATTN_HEADS = 16
ATTN_KV_HEADS = 2
ATTN_HEADDIM = 64
WINDOW = 128
REL_BUCKETS = 32
REL_MAX_DIST = 128
Q_COLS = ATTN_HEADS * ATTN_HEADDIM
KV_COLS = ATTN_KV_HEADS * ATTN_HEADDIM
QKV_COLS = Q_COLS + 2 * KV_COLS
GATE_COLS = 2 * D_MODEL
D_FF = 2816
FFN_CONV = 3
DEPTH = 1
DEEPNORM_ALPHA = (2.0 * DEPTH) ** 0.25
LN_EPS = 1e-5
RMS_EPS = 1e-5

LANES = 128
SUBLANES = 8
DT_PAD = LANES
HALO_ROWS = SUBLANES
SSD_LOOKAHEAD = 1
ATTN_LOOKAHEAD = 1
ROW_BLOCK = 128
VMEM_LIMIT = 56 * 1024 * 1024

PROJ_SEGMENTS = (SSM_D_INNER, SSM_CONV_DIM, QKV_COLS, GATE_COLS, DT_PAD)
PROJ_COLS = sum(PROJ_SEGMENTS)

NEG_BIG = -1e30
LOG2E = 1.4426950408889634


def _tiles():
    assert SSM_CHUNK == WINDOW
    return dict(inproj=512, mixer=4 * SSM_CHUNK, merge=512, ffn=512)


def _dot(a, b):
    return jnp.dot(a, b, preferred_element_type=F32)


def _dot_nt(a, b):
    return lax.dot_general(a, b, (((1,), (1,)), ((), ())), preferred_element_type=F32)


def _sigmoid(x):
    return 1.0 / (1.0 + jnp.exp2(x * (-LOG2E)))


def _silu(x):
    return x * _sigmoid(x)


def _layer_norm(r, g, b):
    mu = jnp.mean(r, axis=-1, keepdims=True)
    rc = r - mu
    var = jnp.mean(rc * rc, axis=-1, keepdims=True)
    return rc * lax.rsqrt(var + LN_EPS) * g + b


def _shift_rows(x, prev, j):
    row = lax.broadcasted_iota(jnp.int32, prev.shape, 0)
    sh = pltpu.roll(x, j, 0)
    top = jnp.where(row < j, pltpu.roll(prev, j, 0), sh[:HALO_ROWS])
    return jnp.concatenate([top, sh[HALO_ROWS:]], axis=0)


def _causal_conv(x, prev, w_ref, b_ref, cols, taps):
    acc = b_ref[:, cols] + w_ref[taps - 1:taps, cols] * x
    for j in range(1, taps):
        acc = acc + w_ref[taps - 1 - j:taps - j, cols] * _shift_rows(x, prev, j)
    return acc


def _inproj_kernel(x_ref, w_ref, cw_ref, cb_ref, dtb_ref, bg_ref,
                   zs_ref, xc_ref, qkv_ref, g_ref, dt_ref, halo_ref):
    tm = x_ref.shape[0]

    @pl.when(pl.program_id(0) == 0)
    def _():
        halo_ref[...] = jnp.zeros_like(halo_ref)

    chunk = 256
    row_blocks = [slice(r, r + ROW_BLOCK) for r in range(0, tm, ROW_BLOCK)]
    xbs = [x_ref[rows, :].astype(BF16) for rows in row_blocks]
    offs = [sum(PROJ_SEGMENTS[:i]) for i in range(len(PROJ_SEGMENTS))]

    def proj(i, seg, c, w):
        return _dot(xbs[i], w_ref[:, offs[seg] + c:offs[seg] + c + w])

    for c in range(0, SSM_D_INNER, chunk):
        for i, rows in enumerate(row_blocks):
            zs_ref[rows, c:c + chunk] = _silu(proj(i, 0, c, chunk)).astype(BF16)
    for c in range(0, SSM_CONV_DIM, chunk):
        cols = slice(c, c + chunk)
        prev = halo_ref[:, cols]
        for i, rows in enumerate(row_blocks):
            u = proj(i, 1, c, chunk)
            xc_ref[rows, cols] = _silu(_causal_conv(u, prev, cw_ref, cb_ref, cols, SSM_CONV)).astype(BF16)
            prev = u[ROW_BLOCK - HALO_ROWS:, :]
        halo_ref[:, cols] = prev
    for c in range(0, QKV_COLS, chunk):
        w = min(chunk, QKV_COLS - c)
        for i, rows in enumerate(row_blocks):
            qkv_ref[rows, c:c + w] = proj(i, 2, c, w).astype(BF16)
    for c in range(0, GATE_COLS, chunk):
        for i, rows in enumerate(row_blocks):
            g_ref[rows, c:c + chunk] = _sigmoid(proj(i, 3, c, chunk) + bg_ref[:, c:c + chunk]).astype(BF16)
    lane_ok = lax.broadcasted_iota(jnp.int32, (ROW_BLOCK, DT_PAD), 1) < SSM_HEADS
    for i, rows in enumerate(row_blocks):
        dtr = proj(i, 4, 0, DT_PAD) + dtb_ref[...]
        softplus = jnp.maximum(dtr, 0.0) + jnp.log(1.0 + jnp.exp(-jnp.abs(dtr)))
        dt_ref[rows, :] = jnp.where(lane_ok, softplus, 0.0)


def _inproj(x2d, w_perm, conv_w, conv_b, dt_bias, b_gate, tm):
    s = x2d.shape[0]
    row = lambda i: (i, 0)
    const = lambda i: (0, 0)
    full = lambda a: pl.BlockSpec(a.shape, const)
    outs = [jax.ShapeDtypeStruct((s, n), dt) for n, dt in
            zip(PROJ_SEGMENTS, (BF16, BF16, BF16, BF16, F32))]
    return pl.pallas_call(
        _inproj_kernel,
        grid=(s // tm,),
        in_specs=[pl.BlockSpec((tm, D_MODEL), row),
                  pl.BlockSpec((D_MODEL, PROJ_COLS), const, pipeline_mode=pl.Buffered(1)),
                  full(conv_w), full(conv_b), full(dt_bias), full(b_gate)],
        out_specs=[pl.BlockSpec((tm, n), row) for n in PROJ_SEGMENTS],
        out_shape=outs,
        scratch_shapes=[pltpu.VMEM((HALO_ROWS, SSM_CONV_DIM), F32)],
        compiler_params=pltpu.CompilerParams(dimension_semantics=("arbitrary",),
                                             vmem_limit_bytes=VMEM_LIMIT),
        name="inproj",
    )(x2d, w_perm, conv_w, conv_b, dt_bias, b_gate)


def _split3(x):
    hi = x.astype(BF16).astype(F32)
    r = x - hi
    mid = r.astype(BF16).astype(F32)
    lo = (r - mid).astype(BF16).astype(F32)
    return hi, mid, lo


def _pack3(x):
    hi, mid, lo = _split3(x)
    return (hi + pltpu.roll(mid, SSM_HEADS, 1) + pltpu.roll(lo, 2 * SSM_HEADS, 1)).astype(BF16)


def _ssd_body(conv_ref, zs_ref, dt_ref, alog_ref, dexp_ref, nw_ref, e3_ref, ec3_ref, y_ref, state_ref):
    L = SSM_CHUNK
    lane_ok = lax.broadcasted_iota(jnp.int32, (L, DT_PAD), 1) < SSM_HEADS
    dt = dt_ref[...]
    a_dt = dt * (-LOG2E * jnp.exp(alog_ref[...]))

    ri = lax.broadcasted_iota(jnp.int32, (L, L), 0)
    ci = lax.broadcasted_iota(jnp.int32, (L, L), 1)
    causal = ri >= ci
    tril = jnp.where(causal, 1.0, 0.0).astype(BF16)
    hi, mid, lo = _split3(a_dt)
    a_cs = _dot(tril, hi.astype(BF16)) + _dot(tril, mid.astype(BF16)) + _dot(tril, lo.astype(BF16))

    a_last = a_cs[L - 1:L, :]
    decay_out_b = jnp.where(lane_ok, jnp.exp2(a_cs), 0.0).astype(BF16)
    w_state_b = (dt * jnp.exp2(a_last - a_cs)).astype(BF16)
    cd_e = jnp.exp2(_dot(_pack3(a_cs[L - SUBLANES:, :]), e3_ref[...])[SUBLANES - 1:, :])

    a_sub_t = (a_cs - jnp.log2(dt)).T
    acs3 = _pack3(a_cs)
    lane_lo = lax.broadcasted_iota(jnp.int32, (L, LANES), 1) < SSM_HEADDIM

    def group_start(g):
        gsl = slice(g * GROUP_WIDTH, (g + 1) * GROUP_WIDTH)
        xs_g = conv_ref[:, gsl].astype(F32)
        b_off = SSM_D_INNER + g * SSM_STATE
        c_off = SSM_D_INNER + SSM_GROUPS * SSM_STATE + g * SSM_STATE
        b_g = conv_ref[:, b_off:b_off + SSM_STATE]
        c_g = conv_ref[:, c_off:c_off + SSM_STATE]
        cb = _dot_nt(c_g, b_g)
        xdtd_b = (xs_g * _dot(w_state_b, e3_ref[:, gsl])).astype(BF16)
        st = state_ref[g]
        y_off = _dot(c_g, st.astype(BF16)) * _dot(decay_out_b, e3_ref[:, gsl])
        state_ref[g] = st * cd_e[:, gsl] + _dot(b_g.astype(F32).T.astype(BF16), xdtd_b)
        return cb, y_off + xs_g * dexp_ref[:, gsl]

    def acol_dot(p):
        return _dot(acs3, ec3_ref[:, 2 * p * L:(2 * p + 2) * L])

    pairs_per_group = SSM_HEADS_PER_GROUP // 2
    n_pairs = SSM_HEADS // 2
    started = {0: group_start(0)}
    acols = {p: acol_dot(p) for p in range(SSD_LOOKAHEAD)}
    pieces = []
    for p in range(n_pairs):
        g, pr = divmod(p, pairs_per_group)
        if p + SSD_LOOKAHEAD < n_pairs:
            acols[p + SSD_LOOKAHEAD] = acol_dot(p + SSD_LOOKAHEAD)
        if pr == pairs_per_group - 2 and g + 1 < SSM_GROUPS:
            started[g + 1] = group_start(g + 1)
        cb, y_rest = started[g]
        acol = acols.pop(p)
        ms = []
        for k in range(2):
            seg = acol[:, k * L:(k + 1) * L] - a_sub_t[2 * p + k:2 * p + k + 1, :]
            ms.append((cb * jnp.exp2(jnp.where(causal, seg, NEG_BIG))).astype(BF16))
        xp = conv_ref[:, p * LANES:(p + 1) * LANES]
        zero = jnp.zeros_like(xp)
        rhs = jnp.concatenate([jnp.where(lane_lo, xp, zero), jnp.where(lane_lo, zero, xp)], axis=0)
        pieces.append(_dot(jnp.concatenate(ms, axis=1), rhs))
        if pr == pairs_per_group - 1:
            gsl = slice(g * GROUP_WIDTH, (g + 1) * GROUP_WIDTH)
            yz = (jnp.concatenate(pieces, axis=1) + y_rest) * zs_ref[:, gsl].astype(F32)
            ms_ = jnp.mean(yz * yz, axis=-1, keepdims=True)
            y_ref[:, gsl] = (yz * lax.rsqrt(ms_ + RMS_EPS) * nw_ref[:, gsl]).astype(BF16)
            pieces = []
            del started[g]
        if p % 2 == 1:
            yield


def _expansion_matrices():
    k = np.arange(LANES)[:, None]
    valid = k < 3 * SSM_HEADS
    head = k % SSM_HEADS
    e3 = (valid & (head == (np.arange(SSM_D_INNER)[None, :] // SSM_HEADDIM))).astype(np.float32)
    ec3 = (valid & (head == (np.arange(SSM_HEADS * SSM_CHUNK)[None, :] // SSM_CHUNK))).astype(np.float32)
    return jnp.asarray(e3, BF16), jnp.asarray(ec3, BF16)


def _build_bias_mask(ur_ref, bm_ref):
    W = WINDOW
    first = lax.broadcasted_iota(jnp.int32, (W, 2 * W), 1) >= W
    for h in range(ATTN_HEADS):
        band_h = LOG2E * pltpu.roll(jnp.broadcast_to(ur_ref[h:h + 1, :], (W, 2 * W)), W, 1,
                                    stride=1, stride_axis=0)
        bm_ref[1, h] = band_h
        bm_ref[0, h] = jnp.where(first, band_h, -jnp.inf)


def _attn_body(sink_ref, q_ref, kvc_ref, kvp_ref, o_ref, bm_ref, slab):
    W = WINDOW
    lane_lo2 = lax.broadcasted_iota(jnp.int32, (2 * W, LANES), 1) < ATTN_HEADDIM
    lane_lo = lax.broadcasted_iota(jnp.int32, (W, LANES), 1) < ATTN_HEADDIM

    def band(lo_col):
        return jnp.concatenate([kvp_ref[:, lo_col:lo_col + KV_COLS].astype(F32),
                                kvc_ref[:, lo_col:lo_col + KV_COLS].astype(F32)], axis=0)

    kband = band(0) * (ATTN_HEADDIM ** -0.5 * LOG2E)
    vband = band(KV_COLS)
    ksw = pltpu.roll(kband, ATTN_HEADDIM, 1)
    vsw = pltpu.roll(vband, ATTN_HEADDIM, 1)
    k_var = ((jnp.where(lane_lo2, kband, 0.0).astype(BF16), jnp.where(lane_lo2, 0.0, ksw).astype(BF16)),
             (jnp.where(lane_lo2, ksw, 0.0).astype(BF16), jnp.where(lane_lo2, 0.0, kband).astype(BF16)))
    v_bd = (jnp.concatenate([jnp.where(lane_lo2, vband, 0.0), jnp.where(lane_lo2, 0.0, vsw)], axis=0).astype(BF16),
            jnp.concatenate([jnp.where(lane_lo2, vsw, 0.0), jnp.where(lane_lo2, 0.0, vband)], axis=0).astype(BF16))

    def kv_head(i):
        return (2 * i) // (ATTN_HEADS // ATTN_KV_HEADS)

    def logits(i):
        qp = q_ref[:, i * LANES:(i + 1) * LANES]
        return [_dot_nt(qp, k_var[kv_head(i)][j]) for j in range(2)]

    n_pairs = ATTN_HEADS // 2
    pending = {i: logits(i) for i in range(ATTN_LOOKAHEAD)}
    for i in range(n_pairs):
        c = kv_head(i)
        if i + ATTN_LOOKAHEAD < n_pairs:
            pending[i + ATTN_LOOKAHEAD] = logits(i + ATTN_LOOKAHEAD)
        s_cur = pending.pop(i)
        ps, rs = [], []
        for j in range(2):
            h = 2 * i + j
            sink = sink_ref[h] * LOG2E
            s = s_cur[j] + bm_ref[slab, h]
            m = jnp.maximum(jnp.max(s, axis=-1, keepdims=True), sink)
            p = jnp.exp2(s - m)
            den = jnp.sum(p, axis=-1, keepdims=True) + jnp.exp2(sink - m)
            ps.append(p.astype(BF16))
            rs.append(1.0 / den)
        pv = _dot(jnp.concatenate(ps, axis=1), v_bd[c])
        o_ref[:, i * LANES:(i + 1) * LANES] = (pv * jnp.where(lane_lo, rs[0], rs[1])).astype(BF16)
        yield


def _mixer_kernel(sink_ref, conv_ref, zs_ref, dt_ref, alog_ref, dexp_ref, nw_ref, e3_ref, ec3_ref,
                  q_ref, kvc_ref, kvp_ref, ur_ref, ys_ref, ya_ref, state_ref, bm_ref):
    @pl.when(pl.program_id(0) == 0)
    def _():
        state_ref[...] = jnp.zeros_like(state_ref)
        _build_bias_mask(ur_ref, bm_ref)

    L = SSM_CHUNK
    live = []
    for blk in range(conv_ref.shape[0] // L):
        rows = pl.ds(blk * L, L)
        live.append(_ssd_body(conv_ref.at[rows], zs_ref.at[rows], dt_ref.at[rows], alog_ref, dexp_ref, nw_ref,
                              e3_ref, ec3_ref, ys_ref.at[rows], state_ref))
        prev = kvp_ref if blk == 0 else kvc_ref.at[pl.ds((blk - 1) * L, L)]
        slab = jnp.minimum(pl.program_id(0), 1) if blk == 0 else 1
        live.append(_attn_body(sink_ref, q_ref.at[rows], kvc_ref.at[rows], prev, ya_ref.at[rows], bm_ref, slab))
    while live:
        for body in list(live):
            if next(body, "done") == "done":
                live.remove(body)


def _mixer(xc, zs, dt, qkv, a_log, d_exp, norm_w, bias_rows, sinks, tm):
    s = xc.shape[0]
    e3, ec3 = _expansion_matrices()
    row = lambda i: (i, 0)
    const = lambda i: (0, 0)
    full = lambda a: pl.BlockSpec(a.shape, const)
    kv_blk = Q_COLS // (2 * KV_COLS)
    return pl.pallas_call(
        _mixer_kernel,
        grid=(s // tm,),
        in_specs=[pl.BlockSpec(memory_space=pltpu.SMEM),
                  pl.BlockSpec((tm, SSM_CONV_DIM), row),
                  pl.BlockSpec((tm, SSM_D_INNER), row),
                  pl.BlockSpec((tm, DT_PAD), row),
                  full(a_log), full(d_exp), full(norm_w), full(e3), full(ec3),
                  pl.BlockSpec((tm, Q_COLS), row),
                  pl.BlockSpec((tm, 2 * KV_COLS), lambda i: (i, kv_blk)),
                  pl.BlockSpec((WINDOW, 2 * KV_COLS),
                               lambda i: (jnp.maximum(i * (tm // WINDOW) - 1, 0), kv_blk)),
                  full(bias_rows)],
        out_specs=[pl.BlockSpec((tm, SSM_D_INNER), row), pl.BlockSpec((tm, Q_COLS), row)],
        out_shape=[jax.ShapeDtypeStruct((s, SSM_D_INNER), BF16), jax.ShapeDtypeStruct((s, Q_COLS), BF16)],
        scratch_shapes=[pltpu.VMEM((SSM_GROUPS, SSM_STATE, GROUP_WIDTH), F32),
                        pltpu.VMEM((2, ATTN_HEADS, WINDOW, 2 * WINDOW), F32)],
        compiler_params=pltpu.CompilerParams(dimension_semantics=("arbitrary",),
                                             vmem_limit_bytes=VMEM_LIMIT),
        name="mixer",
    )(sinks, xc, zs, dt, a_log, d_exp, norm_w, e3, ec3, qkv, qkv, qkv, bias_rows)


def _rel_bucket_static(n):
    max_exact = REL_BUCKETS // 2
    nf = np.maximum(n, 1).astype(np.float32)
    large = max_exact + (np.log(nf / max_exact) / math.log(REL_MAX_DIST / max_exact)
                         * (REL_BUCKETS - max_exact)).astype(np.int32)
    return np.where(n < max_exact, n, np.minimum(large, REL_BUCKETS - 1))


def _bias_rows(rel_bias):
    rel = (-np.arange(2 * WINDOW)) % (2 * WINDOW)
    idx = np.where(rel < WINDOW, _rel_bucket_static(rel), REL_BUCKETS)
    table = jnp.concatenate([rel_bias.astype(F32), jnp.full((1, ATTN_HEADS), -jnp.inf, F32)], axis=0)
    return table[idx].T


def _merge_kernel(ys_ref, ya_ref, g_ref, x_ref, wbs_ref, wba_ref, wmix_ref, lg_ref, lb_ref, h_ref):
    chunk = 256
    rb = 2 * ROW_BLOCK
    for r in range(0, ys_ref.shape[0], rb):
        rows = slice(r, r + rb)
        ys, ya = ys_ref[rows, :], ya_ref[rows, :]
        merged = []
        for c in range(0, D_MODEL, chunk):
            a = _dot(ys, wbs_ref[:, c:c + chunk])
            b = _dot(ya, wba_ref[:, c:c + chunk])
            merged.append((g_ref[rows, c:c + chunk].astype(F32) * a
                           + g_ref[rows, D_MODEL + c:D_MODEL + c + chunk].astype(F32) * b).astype(BF16))
        mix = _dot(jnp.concatenate(merged, axis=1), wmix_ref[...])
        h_ref[rows, :] = _layer_norm(DEEPNORM_ALPHA * x_ref[rows, :] + mix, lg_ref[...], lb_ref[...])


def _merge(y_ssm, y_attn, gates, x2d, w_bs, w_ba, w_mix, ln_g, ln_b, tm):
    s = x2d.shape[0]
    row = lambda i: (i, 0)
    const = lambda i: (0, 0)
    full = lambda a: pl.BlockSpec(a.shape, const)
    return pl.pallas_call(
        _merge_kernel,
        grid=(s // tm,),
        in_specs=[pl.BlockSpec((tm, SSM_D_INNER), row), pl.BlockSpec((tm, Q_COLS), row),
                  pl.BlockSpec((tm, GATE_COLS), row), pl.BlockSpec((tm, D_MODEL), row),
                  full(w_bs), full(w_ba), full(w_mix), full(ln_g), full(ln_b)],
        out_specs=pl.BlockSpec((tm, D_MODEL), row),
        out_shape=jax.ShapeDtypeStruct((s, D_MODEL), F32),
        compiler_params=pltpu.CompilerParams(dimension_semantics=("arbitrary",),
                                             vmem_limit_bytes=VMEM_LIMIT),
        name="merge",
    )(y_ssm, y_attn, gates, x2d, w_bs, w_ba, w_mix, ln_g, ln_b)


def _ffn_kernel(h_ref, wup_ref, cw_ref, cb_ref, wdn_ref, lg_ref, lb_ref, o_ref, halo_ref, act_ref):
    tm = h_ref.shape[0]

    @pl.when(pl.program_id(0) == 0)
    def _():
        halo_ref[...] = jnp.zeros_like(halo_ref)

    chunk = 256
    rb = 2 * ROW_BLOCK
    row_blocks = [slice(r, r + rb) for r in range(0, tm, rb)]
    hbs = [h_ref[rows, :].astype(BF16) for rows in row_blocks]

    def conv_cols(hb, cols, prev):
        u = _dot(hb, wup_ref[:, cols])
        return _causal_conv(u, prev, cw_ref, cb_ref, cols, FFN_CONV), u[rb - HALO_ROWS:, :]

    for c in range(0, D_FF, chunk):
        gcols, vcols = slice(c, c + chunk), slice(D_FF + c, D_FF + c + chunk)
        gprev, vprev = halo_ref[:, gcols], halo_ref[:, vcols]
        for hb, rows in zip(hbs, row_blocks):
            gate, gprev = conv_cols(hb, gcols, gprev)
            val, vprev = conv_cols(hb, vcols, vprev)
            act_ref[rows, c:c + chunk] = (_silu(gate) * val).astype(BF16)
        halo_ref[:, gcols] = gprev
        halo_ref[:, vcols] = vprev

    for rows in row_blocks:
        out = _dot(act_ref[rows, :], wdn_ref[...])
        o_ref[rows, :] = _layer_norm(DEEPNORM_ALPHA * h_ref[rows, :] + out, lg_ref[...], lb_ref[...])


def _ffn(h1, w_up, conv_w, conv_b, w_down, ln_g, ln_b, tm):
    s = h1.shape[0]
    row = lambda i: (i, 0)
    const = lambda i: (0, 0)
    full = lambda a: pl.BlockSpec(a.shape, const)
    return pl.pallas_call(
        _ffn_kernel,
        grid=(s // tm,),
        in_specs=[pl.BlockSpec((tm, D_MODEL), row),
                  pl.BlockSpec(w_up.shape, const, pipeline_mode=pl.Buffered(1)),
                  full(conv_w), full(conv_b),
                  pl.BlockSpec(w_down.shape, const, pipeline_mode=pl.Buffered(1)),
                  full(ln_g), full(ln_b)],
        out_specs=pl.BlockSpec((tm, D_MODEL), row),
        out_shape=jax.ShapeDtypeStruct((s, D_MODEL), F32),
        scratch_shapes=[pltpu.VMEM((HALO_ROWS, 2 * D_FF), F32),
                        pltpu.VMEM((tm, D_FF), BF16)],
        compiler_params=pltpu.CompilerParams(dimension_semantics=("arbitrary",),
                                             vmem_limit_bytes=VMEM_LIMIT),
        name="ffn",
    )(h1, w_up, conv_w, conv_b, w_down, ln_g, ln_b)


PREP_BLOCK = 512
DT_LO = SSM_D_INNER + SSM_CONV_DIM
DT_DST = PROJ_COLS - DT_PAD


def _permute_kernel(a_ref, b_ref, d_ref, o_ref):
    j = pl.program_id(0)
    first_shifted = DT_LO // PREP_BLOCK
    last = (PROJ_COLS - 1) // PREP_BLOCK

    @pl.when(j < first_shifted)
    def _():
        o_ref[...] = a_ref[...].T.astype(BF16)

    @pl.when((j >= first_shifted) & (j < last))
    def _():
        rows = jnp.concatenate([a_ref[SSM_HEADS:, :], b_ref[:SSM_HEADS, :]], axis=0)
        o_ref[...] = rows.T.astype(BF16)

    @pl.when(j == last)
    def _():
        n_tail = DT_DST - last * PREP_BLOCK
        rows = jnp.concatenate([a_ref[SSM_HEADS:SSM_HEADS + n_tail, :], d_ref[:SSM_HEADS, :],
                                jnp.zeros((DT_PAD - SSM_HEADS, D_MODEL), F32)], axis=0)
        o_ref[:, :n_tail + DT_PAD] = rows.T.astype(BF16)


def _permute_w_in(w_t):
    n_src = DT_LO + SSM_HEADS + QKV_COLS + GATE_COLS
    assert DT_LO % PREP_BLOCK == 0 and w_t.shape == (n_src, D_MODEL)
    first_shifted = DT_LO // PREP_BLOCK
    last_src = (n_src - 1) // PREP_BLOCK
    blk = lambda f: pl.BlockSpec((PREP_BLOCK, D_MODEL), f)
    return pl.pallas_call(
        _permute_kernel,
        grid=(pl.cdiv(PROJ_COLS, PREP_BLOCK),),
        in_specs=[blk(lambda j: (jnp.minimum(j, last_src), 0)),
                  blk(lambda j: (jnp.clip(j + 1, first_shifted, last_src), 0)),
                  blk(lambda j: (first_shifted, 0))],
        out_specs=pl.BlockSpec((D_MODEL, PREP_BLOCK), lambda j: (0, j)),
        out_shape=jax.ShapeDtypeStruct((D_MODEL, PROJ_COLS), BF16),
        compiler_params=pltpu.CompilerParams(dimension_semantics=("arbitrary",),
                                             vmem_limit_bytes=VMEM_LIMIT),
        name="permute_w_in",
    )(w_t, w_t, w_t)


def _row(v, width=None):
    v = v.astype(F32).reshape(1, -1)
    if width is not None and v.shape[1] < width:
        v = jnp.pad(v, ((0, 0), (0, width - v.shape[1])))
    return v


def kernel(x, rel_bias, w_in, b_gate, ssm_conv_w, ssm_conv_b, ssm_dt_bias, ssm_a_log, ssm_d, ssm_norm_w,
           attn_sinks, w_branch_ssm, w_branch_attn, w_mix_out, ln1_g, ln1_b, w_up, ffn_conv_w, ffn_conv_b,
           w_down, ln2_g, ln2_b):
    b, s, d = x.shape
    assert (b, d) == (1, D_MODEL) and s % 512 == 0 and w_in.shape[0] == DEPTH
    t = _tiles()
    h = x.reshape(s, d)
    bias_rows = _bias_rows(rel_bias)
    for l in range(DEPTH):
        zs, xc, qkv, gates, dt = _inproj(h, _permute_w_in(w_in[l].T), ssm_conv_w[l].astype(F32),
                                         _row(ssm_conv_b[l]), _row(ssm_dt_bias[l], DT_PAD),
                                         _row(b_gate[l]), t["inproj"])
        y_ssm, y_attn = _mixer(xc, zs, dt, qkv, _row(ssm_a_log[l], DT_PAD),
                               _row(jnp.repeat(ssm_d[l], SSM_HEADDIM)), _row(ssm_norm_w[l]),
                               bias_rows, attn_sinks[l].astype(F32), t["mixer"])
        h1 = _merge(y_ssm, y_attn, gates, h, w_branch_ssm[l].astype(BF16), w_branch_attn[l].astype(BF16),
                    w_mix_out[l].astype(BF16), _row(ln1_g[l]), _row(ln1_b[l]), t["merge"])
        h = _ffn(h1, w_up[l].astype(BF16), ffn_conv_w[l].astype(F32), _row(ffn_conv_b[l]),
                 w_down[l].astype(BF16), _row(ln2_g[l]), _row(ln2_b[l]), t["ffn"])
    return h.reshape(b, s, d)
```

```python
import math

import numpy as np
import jax
import jax.numpy as jnp
from jax import lax
from jax.experimental import pallas as pl
from jax.experimental.pallas import tpu as pltpu

F32 = jnp.float32
BF16 = jnp.bfloat16

D_MODEL = 1024
SSM_D_INNER = 2048
SSM_HEADDIM = 64
SSM_HEADS = 32
SSM_GROUPS = 4
SSM_HEADS_PER_GROUP = 8
SSM_STATE = 128
SSM_CONV = 4
SSM_CHUNK = 128
SSM_CONV_DIM = SSM_D_INNER + 2 * SSM_GROUPS * SSM_STATE
GROUP_WIDTH = SSM_HEADS_PER_GROUP * SSM_HEADDIM
ATTN_HEADS = 16
ATTN_KV_HEADS = 2
ATTN_HEADDIM = 64
WINDOW = 128
REL_BUCKETS = 32
REL_MAX_DIST = 128
Q_COLS = ATTN_HEADS * ATTN_HEADDIM
KV_COLS = ATTN_KV_HEADS * ATTN_HEADDIM
QKV_COLS = Q_COLS + 2 * KV_COLS
GATE_COLS = 2 * D_MODEL
D_FF = 2816
FFN_CONV = 3
DEPTH = 1
DEEPNORM_ALPHA = (2.0 * DEPTH) ** 0.25
LN_EPS = 1e-5
RMS_EPS = 1e-5

LANES = 128
SUBLANES = 8
DT_PAD = LANES
HALO_ROWS = SUBLANES
SSD_LOOKAHEAD = 1
ATTN_LOOKAHEAD = 1
ROW_BLOCK = 128
VMEM_LIMIT = 56 * 1024 * 1024

PROJ_SEGMENTS = (SSM_D_INNER, SSM_CONV_DIM, QKV_COLS, GATE_COLS, DT_PAD)
PROJ_COLS = sum(PROJ_SEGMENTS)

NEG_BIG = -1e30
LOG2E = 1.4426950408889634


def _tiles():
    assert SSM_CHUNK == WINDOW
    return dict(inproj=512, mixer=2 * SSM_CHUNK, merge=512, ffn=1024)


def _dot(a, b):
    return jnp.dot(a, b, preferred_element_type=F32)


def _dot_nt(a, b):
    return lax.dot_general(a, b, (((1,), (1,)), ((), ())), preferred_element_type=F32)


def _sigmoid(x):
    return 1.0 / (1.0 + jnp.exp2(x * (-LOG2E)))


def _silu(x):
    h = 0.5 * x
    return h + h * jnp.tanh(h)


def _layer_norm(r, g, b):
    mu = jnp.mean(r, axis=-1, keepdims=True)
    rc = r - mu
    var = jnp.mean(rc * rc, axis=-1, keepdims=True)
    return rc * lax.rsqrt(var + LN_EPS) * g + b


def _shift_rows(x, prev, j):
    row = lax.broadcasted_iota(jnp.int32, prev.shape, 0)
    sh = pltpu.roll(x, j, 0)
    top = jnp.where(row < j, pltpu.roll(prev, j, 0), sh[:HALO_ROWS])
    return jnp.concatenate([top, sh[HALO_ROWS:]], axis=0)


def _causal_conv(x, prev, w_ref, b_ref, cols, taps):
    acc = b_ref[:, cols] + w_ref[taps - 1:taps, cols] * x
    for j in range(1, taps):
        acc = acc + w_ref[taps - 1 - j:taps - j, cols] * _shift_rows(x, prev, j)
    return acc


def _inproj_kernel(x_ref, w_ref, cw_ref, cb_ref, dtb_ref, bg_ref,
                   zs_ref, xc_ref, qkv_ref, g_ref, dt_ref, halo_ref):
    tm = x_ref.shape[0]

    @pl.when(pl.program_id(0) == 0)
    def _():
        halo_ref[...] = jnp.zeros_like(halo_ref)

    chunk = 256
    row_blocks = [slice(r, r + ROW_BLOCK) for r in range(0, tm, ROW_BLOCK)]
    xbs = [x_ref[rows, :].astype(BF16) for rows in row_blocks]
    offs = [sum(PROJ_SEGMENTS[:i]) for i in range(len(PROJ_SEGMENTS))]

    def proj(i, seg, c, w):
        return _dot(xbs[i], w_ref[:, offs[seg] + c:offs[seg] + c + w])

    wide_blocks = [slice(r, r + 2 * ROW_BLOCK) for r in range(0, tm, 2 * ROW_BLOCK)]
    xbw = [jnp.concatenate(xbs[2 * i:2 * i + 2], axis=0) for i in range(len(wide_blocks))]

    def proj_wide(i, seg, c, w):
        return _dot(xbw[i], w_ref[:, offs[seg] + c:offs[seg] + c + w])

    for c in range(0, SSM_D_INNER, chunk):
        for i, rows in enumerate(wide_blocks):
            zs_ref[rows, c:c + chunk] = _silu(proj_wide(i, 0, c, chunk)).astype(BF16)
    for c in range(0, SSM_CONV_DIM, chunk):
        cols = slice(c, c + chunk)
        prev = halo_ref[:, cols]
        for i, rows in enumerate(row_blocks):
            u = proj(i, 1, c, chunk)
            xc_ref[rows, cols] = _silu(_causal_conv(u, prev, cw_ref, cb_ref, cols, SSM_CONV)).astype(BF16)
            prev = u[ROW_BLOCK - HALO_ROWS:, :]
        halo_ref[:, cols] = prev
    for c in range(0, QKV_COLS, chunk):
        w = min(chunk, QKV_COLS - c)
        for i, rows in enumerate(wide_blocks):
            qkv_ref[rows, c:c + w] = proj_wide(i, 2, c, w).astype(BF16)
    for c in range(0, GATE_COLS, chunk):
        for i, rows in enumerate(wide_blocks):
            g_ref[rows, c:c + chunk] = _sigmoid(proj_wide(i, 3, c, chunk) + bg_ref[:, c:c + chunk]).astype(BF16)
    lane_ok = lax.broadcasted_iota(jnp.int32, (ROW_BLOCK, DT_PAD), 1) < SSM_HEADS
    for i, rows in enumerate(row_blocks):
        dtr = proj(i, 4, 0, DT_PAD) + dtb_ref[...]
        softplus = jnp.maximum(dtr, 0.0) + jnp.log(1.0 + jnp.exp(-jnp.abs(dtr)))
        dt_ref[rows, :] = jnp.where(lane_ok, softplus, 0.0)


def _inproj(x2d, w_perm, conv_w, conv_b, dt_bias, b_gate, tm):
    s = x2d.shape[0]
    row = lambda i: (i, 0)
    const = lambda i: (0, 0)
    full = lambda a: pl.BlockSpec(a.shape, const)
    outs = [jax.ShapeDtypeStruct((s, n), dt) for n, dt in
            zip(PROJ_SEGMENTS, (BF16, BF16, BF16, BF16, F32))]
    return pl.pallas_call(
        _inproj_kernel,
        grid=(s // tm,),
        in_specs=[pl.BlockSpec((tm, D_MODEL), row),
                  pl.BlockSpec((D_MODEL, PROJ_COLS), const, pipeline_mode=pl.Buffered(1)),
                  full(conv_w), full(conv_b), full(dt_bias), full(b_gate)],
        out_specs=[pl.BlockSpec((tm, n), row) for n in PROJ_SEGMENTS],
        out_shape=outs,
        scratch_shapes=[pltpu.VMEM((HALO_ROWS, SSM_CONV_DIM), F32)],
        compiler_params=pltpu.CompilerParams(dimension_semantics=("arbitrary",),
                                             vmem_limit_bytes=VMEM_LIMIT),
        name="inproj",
    )(x2d, w_perm, conv_w, conv_b, dt_bias, b_gate)


def _split3(x):
    hi = x.astype(BF16).astype(F32)
    r = x - hi
    mid = r.astype(BF16).astype(F32)
    lo = (r - mid).astype(BF16).astype(F32)
    return hi, mid, lo


def _pack3(x):
    hi, mid, lo = _split3(x)
    return (hi + pltpu.roll(mid, SSM_HEADS, 1) + pltpu.roll(lo, 2 * SSM_HEADS, 1)).astype(BF16)


def _ssd_body(conv_ref, zs_ref, dt_ref, alog_ref, dexp_ref, nw_ref, e3_ref, ec3_ref, y_ref, state_ref):
    L = SSM_CHUNK
    lane_ok = lax.broadcasted_iota(jnp.int32, (L, DT_PAD), 1) < SSM_HEADS
    dt = dt_ref[...]
    a_dt = dt * (-LOG2E * jnp.exp(alog_ref[...]))

    ri = lax.broadcasted_iota(jnp.int32, (L, L), 0)
    ci = lax.broadcasted_iota(jnp.int32, (L, L), 1)
    causal = ri >= ci
    tril = jnp.where(causal, 1.0, 0.0).astype(BF16)
    hi, mid, lo = _split3(a_dt)
    a_cs = _dot(tril, hi.astype(BF16)) + _dot(tril, mid.astype(BF16)) + _dot(tril, lo.astype(BF16))

    a_last = a_cs[L - 1:L, :]
    decay_out_b = jnp.where(lane_ok, jnp.exp2(a_cs), 0.0).astype(BF16)
    w_state_b = (dt * jnp.exp2(a_last - a_cs)).astype(BF16)
    cd_e = jnp.exp2(_dot(_pack3(a_cs[L - SUBLANES:, :]), e3_ref[...])[SUBLANES - 1:, :])

    a_sub_t = (a_cs - jnp.log2(dt)).T
    acs3 = _pack3(a_cs)
    lane_lo = lax.broadcasted_iota(jnp.int32, (L, LANES), 1) < SSM_HEADDIM

    def group_start(g):
        gsl = slice(g * GROUP_WIDTH, (g + 1) * GROUP_WIDTH)
        xs_g = conv_ref[:, gsl].astype(F32)
        b_off = SSM_D_INNER + g * SSM_STATE
        c_off = SSM_D_INNER + SSM_GROUPS * SSM_STATE + g * SSM_STATE
        b_g = conv_ref[:, b_off:b_off + SSM_STATE]
        c_g = conv_ref[:, c_off:c_off + SSM_STATE]
        cb = _dot_nt(c_g, b_g)
        xdtd_b = (xs_g * _dot(w_state_b, e3_ref[:, gsl])).astype(BF16)
        st = state_ref[g]
        y_off = _dot(c_g, st.astype(BF16)) * _dot(decay_out_b, e3_ref[:, gsl])
        state_ref[g] = st * cd_e[:, gsl] + _dot(b_g.astype(F32).T.astype(BF16), xdtd_b)
        return cb, y_off + xs_g * dexp_ref[:, gsl]

    def acol_dot(p):
        return _dot(acs3, ec3_ref[:, 2 * p * L:(2 * p + 2) * L])

    pairs_per_group = SSM_HEADS_PER_GROUP // 2
    n_pairs = SSM_HEADS // 2
    started = {0: group_start(0)}
    acols = {p: acol_dot(p) for p in range(SSD_LOOKAHEAD)}
    pieces = []
    for p in range(n_pairs):
        g, pr = divmod(p, pairs_per_group)
        if p + SSD_LOOKAHEAD < n_pairs:
            acols[p + SSD_LOOKAHEAD] = acol_dot(p + SSD_LOOKAHEAD)
        if pr == pairs_per_group - 2 and g + 1 < SSM_GROUPS:
            started[g + 1] = group_start(g + 1)
        cb, y_rest = started[g]
        acol = acols.pop(p)
        ms = []
        for k in range(2):
            seg = acol[:, k * L:(k + 1) * L] - a_sub_t[2 * p + k:2 * p + k + 1, :]
            ms.append((cb * jnp.exp2(jnp.where(causal, seg, NEG_BIG))).astype(BF16))
        xp = conv_ref[:, p * LANES:(p + 1) * LANES]
        zero = jnp.zeros_like(xp)
        rhs = jnp.concatenate([jnp.where(lane_lo, xp, zero), jnp.where(lane_lo, zero, xp)], axis=0)
        pieces.append(_dot(jnp.concatenate(ms, axis=1), rhs))
        if pr == pairs_per_group - 1:
            gsl = slice(g * GROUP_WIDTH, (g + 1) * GROUP_WIDTH)
            yz = (jnp.concatenate(pieces, axis=1) + y_rest) * zs_ref[:, gsl].astype(F32)
            ms_ = jnp.mean(yz * yz, axis=-1, keepdims=True)
            y_ref[:, gsl] = (yz * lax.rsqrt(ms_ + RMS_EPS) * nw_ref[:, gsl]).astype(BF16)
            pieces = []
            del started[g]
        if p % 2 == 1:
            yield


def _expansion_matrices():
    k = np.arange(LANES)[:, None]
    valid = k < 3 * SSM_HEADS
    head = k % SSM_HEADS
    e3 = (valid & (head == (np.arange(SSM_D_INNER)[None, :] // SSM_HEADDIM))).astype(np.float32)
    ec3 = (valid & (head == (np.arange(SSM_HEADS * SSM_CHUNK)[None, :] // SSM_CHUNK))).astype(np.float32)
    return jnp.asarray(e3, BF16), jnp.asarray(ec3, BF16)


def _build_bias_mask(ur_ref, bm_ref):
    W = WINDOW
    first = lax.broadcasted_iota(jnp.int32, (W, 2 * W), 1) >= W
    for h in range(ATTN_HEADS):
        band_h = LOG2E * pltpu.roll(jnp.broadcast_to(ur_ref[h:h + 1, :], (W, 2 * W)), W, 1,
                                    stride=1, stride_axis=0)
        bm_ref[1, h] = band_h
        bm_ref[0, h] = jnp.where(first, band_h, -jnp.inf)


def _attn_body(sink_ref, q_ref, kvc_ref, kvp_ref, o_ref, bm_ref, slab):
    W = WINDOW
    lane_lo2 = lax.broadcasted_iota(jnp.int32, (2 * W, LANES), 1) < ATTN_HEADDIM
    lane_lo = lax.broadcasted_iota(jnp.int32, (W, LANES), 1) < ATTN_HEADDIM

    def band(lo_col):
        return jnp.concatenate([kvp_ref[:, lo_col:lo_col + KV_COLS].astype(F32),
                                kvc_ref[:, lo_col:lo_col + KV_COLS].astype(F32)], axis=0)

    kband = band(0) * (ATTN_HEADDIM ** -0.5 * LOG2E)
    vband = band(KV_COLS)
    ksw = pltpu.roll(kband, ATTN_HEADDIM, 1)
    vsw = pltpu.roll(vband, ATTN_HEADDIM, 1)
    k_var = ((jnp.where(lane_lo2, kband, 0.0).astype(BF16), jnp.where(lane_lo2, 0.0, ksw).astype(BF16)),
             (jnp.where(lane_lo2, ksw, 0.0).astype(BF16), jnp.where(lane_lo2, 0.0, kband).astype(BF16)))
    v_bd = (jnp.concatenate([jnp.where(lane_lo2, vband, 0.0), jnp.where(lane_lo2, 0.0, vsw)], axis=0).astype(BF16),
            jnp.concatenate([jnp.where(lane_lo2, vsw, 0.0), jnp.where(lane_lo2, 0.0, vband)], axis=0).astype(BF16))

    def kv_head(i):
        return (2 * i) // (ATTN_HEADS // ATTN_KV_HEADS)

    def logits(i):
        qp = q_ref[:, i * LANES:(i + 1) * LANES]
        return [_dot_nt(qp, k_var[kv_head(i)][j]) for j in range(2)]

    n_pairs = ATTN_HEADS // 2
    pending = {i: logits(i) for i in range(ATTN_LOOKAHEAD)}
    for i in range(n_pairs):
        c = kv_head(i)
        if i + ATTN_LOOKAHEAD < n_pairs:
            pending[i + ATTN_LOOKAHEAD] = logits(i + ATTN_LOOKAHEAD)
        s_cur = pending.pop(i)
        ps, rs = [], []
        for j in range(2):
            h = 2 * i + j
            sink = sink_ref[h] * LOG2E
            s = s_cur[j] + bm_ref[slab, h]
            m = jnp.maximum(jnp.max(s, axis=-1, keepdims=True), sink)
            p = jnp.exp2(s - m)
            den = jnp.sum(p, axis=-1, keepdims=True) + jnp.exp2(sink - m)
            ps.append(p.astype(BF16))
            rs.append(1.0 / den)
        pv = _dot(jnp.concatenate(ps, axis=1), v_bd[c])
        o_ref[:, i * LANES:(i + 1) * LANES] = (pv * jnp.where(lane_lo, rs[0], rs[1])).astype(BF16)
        yield


def _mixer_kernel(sink_ref, conv_ref, zs_ref, dt_ref, alog_ref, dexp_ref, nw_ref, e3_ref, ec3_ref,
                  q_ref, kvc_ref, kvp_ref, ur_ref, ys_ref, ya_ref, state_ref, bm_ref):
    @pl.when(pl.program_id(0) == 0)
    def _():
        state_ref[...] = jnp.zeros_like(state_ref)
        _build_bias_mask(ur_ref, bm_ref)

    L = SSM_CHUNK
    live = []
    for blk in range(conv_ref.shape[0] // L):
        rows = pl.ds(blk * L, L)
        live.append(_ssd_body(conv_ref.at[rows], zs_ref.at[rows], dt_ref.at[rows], alog_ref, dexp_ref, nw_ref,
                              e3_ref, ec3_ref, ys_ref.at[rows], state_ref))
        prev = kvp_ref if blk == 0 else kvc_ref.at[pl.ds((blk - 1) * L, L)]
        slab = jnp.minimum(pl.program_id(0), 1) if blk == 0 else 1
        live.append(_attn_body(sink_ref, q_ref.at[rows], kvc_ref.at[rows], prev, ya_ref.at[rows], bm_ref, slab))
    while live:
        for body in list(live):
            if next(body, "done") == "done":
                live.remove(body)


def _mixer(xc, zs, dt, qkv, a_log, d_exp, norm_w, bias_rows, sinks, tm):
    s = xc.shape[0]
    e3, ec3 = _expansion_matrices()
    row = lambda i: (i, 0)
    const = lambda i: (0, 0)
    full = lambda a: pl.BlockSpec(a.shape, const)
    kv_blk = Q_COLS // (2 * KV_COLS)
    return pl.pallas_call(
        _mixer_kernel,
        grid=(s // tm,),
        in_specs=[pl.BlockSpec(memory_space=pltpu.SMEM),
                  pl.BlockSpec((tm, SSM_CONV_DIM), row),
                  pl.BlockSpec((tm, SSM_D_INNER), row),
                  pl.BlockSpec((tm, DT_PAD), row),
                  full(a_log), full(d_exp), full(norm_w), full(e3), full(ec3),
                  pl.BlockSpec((tm, Q_COLS), row),
                  pl.BlockSpec((tm, 2 * KV_COLS), lambda i: (i, kv_blk)),
                  pl.BlockSpec((WINDOW, 2 * KV_COLS),
                               lambda i: (jnp.maximum(i * (tm // WINDOW) - 1, 0), kv_blk)),
                  full(bias_rows)],
        out_specs=[pl.BlockSpec((tm, SSM_D_INNER), row), pl.BlockSpec((tm, Q_COLS), row)],
        out_shape=[jax.ShapeDtypeStruct((s, SSM_D_INNER), BF16), jax.ShapeDtypeStruct((s, Q_COLS), BF16)],
        scratch_shapes=[pltpu.VMEM((SSM_GROUPS, SSM_STATE, GROUP_WIDTH), F32),
                        pltpu.VMEM((2, ATTN_HEADS, WINDOW, 2 * WINDOW), F32)],
        compiler_params=pltpu.CompilerParams(dimension_semantics=("arbitrary",),
                                             vmem_limit_bytes=VMEM_LIMIT),
        name="mixer",
    )(sinks, xc, zs, dt, a_log, d_exp, norm_w, e3, ec3, qkv, qkv, qkv, bias_rows)


def _rel_bucket_static(n):
    max_exact = REL_BUCKETS // 2
    nf = np.maximum(n, 1).astype(np.float32)
    large = max_exact + (np.log(nf / max_exact) / math.log(REL_MAX_DIST / max_exact)
                         * (REL_BUCKETS - max_exact)).astype(np.int32)
    return np.where(n < max_exact, n, np.minimum(large, REL_BUCKETS - 1))


def _bias_rows(rel_bias):
    rel = (-np.arange(2 * WINDOW)) % (2 * WINDOW)
    idx = np.where(rel < WINDOW, _rel_bucket_static(rel), REL_BUCKETS)
    table = jnp.concatenate([rel_bias.astype(F32), jnp.full((1, ATTN_HEADS), -jnp.inf, F32)], axis=0)
    return table[idx].T


def _merge_kernel(ys_ref, ya_ref, g_ref, x_ref, wbs_ref, wba_ref, wmix_ref, lg_ref, lb_ref, h_ref):
    chunk = 256
    rb = 2 * ROW_BLOCK
    for r in range(0, ys_ref.shape[0], rb):
        rows = slice(r, r + rb)
        ys, ya = ys_ref[rows, :], ya_ref[rows, :]
        merged = []
        for c in range(0, D_MODEL, chunk):
            a = _dot(ys, wbs_ref[:, c:c + chunk])
            b = _dot(ya, wba_ref[:, c:c + chunk])
            merged.append((g_ref[rows, c:c + chunk].astype(F32) * a
                           + g_ref[rows, D_MODEL + c:D_MODEL + c + chunk].astype(F32) * b).astype(BF16))
        mix = _dot(jnp.concatenate(merged, axis=1), wmix_ref[...])
        h_ref[rows, :] = _layer_norm(DEEPNORM_ALPHA * x_ref[rows, :] + mix, lg_ref[...], lb_ref[...])


def _merge(y_ssm, y_attn, gates, x2d, w_bs, w_ba, w_mix, ln_g, ln_b, tm):
    s = x2d.shape[0]
    row = lambda i: (i, 0)
    const = lambda i: (0, 0)
    full = lambda a: pl.BlockSpec(a.shape, const)
    return pl.pallas_call(
        _merge_kernel,
        grid=(s // tm,),
        in_specs=[pl.BlockSpec((tm, SSM_D_INNER), row), pl.BlockSpec((tm, Q_COLS), row),
                  pl.BlockSpec((tm, GATE_COLS), row), pl.BlockSpec((tm, D_MODEL), row),
                  full(w_bs), full(w_ba), full(w_mix), full(ln_g), full(ln_b)],
        out_specs=pl.BlockSpec((tm, D_MODEL), row),
        out_shape=jax.ShapeDtypeStruct((s, D_MODEL), F32),
        compiler_params=pltpu.CompilerParams(dimension_semantics=("arbitrary",),
                                             vmem_limit_bytes=VMEM_LIMIT),
        name="merge",
    )(y_ssm, y_attn, gates, x2d, w_bs, w_ba, w_mix, ln_g, ln_b)


def _ffn_kernel(h_ref, wup_ref, cw_ref, cb_ref, wdn_ref, lg_ref, lb_ref, o_ref, halo_ref, act_ref):
    tm = h_ref.shape[0]

    @pl.when(pl.program_id(0) == 0)
    def _():
        halo_ref[...] = jnp.zeros_like(halo_ref)

    chunk = 256
    rb = 2 * ROW_BLOCK
    row_blocks = [slice(r, r + rb) for r in range(0, tm, rb)]
    hbs = [h_ref[rows, :].astype(BF16) for rows in row_blocks]

    def conv_cols(hb, cols, prev):
        u = _dot(hb, wup_ref[:, cols])
        return _causal_conv(u, prev, cw_ref, cb_ref, cols, FFN_CONV), u[rb - HALO_ROWS:, :]

    for c in range(0, D_FF, chunk):
        gcols, vcols = slice(c, c + chunk), slice(D_FF + c, D_FF + c + chunk)
        gprev, vprev = halo_ref[:, gcols], halo_ref[:, vcols]
        for hb, rows in zip(hbs, row_blocks):
            gate, gprev = conv_cols(hb, gcols, gprev)
            val, vprev = conv_cols(hb, vcols, vprev)
            act_ref[rows, c:c + chunk] = (_silu(gate) * val).astype(BF16)
        halo_ref[:, gcols] = gprev
        halo_ref[:, vcols] = vprev

    for rows in row_blocks:
        out = _dot(act_ref[rows, :], wdn_ref[...])
        o_ref[rows, :] = _layer_norm(DEEPNORM_ALPHA * h_ref[rows, :] + out, lg_ref[...], lb_ref[...])


def _ffn(h1, w_up, conv_w, conv_b, w_down, ln_g, ln_b, tm):
    s = h1.shape[0]
    row = lambda i: (i, 0)
    const = lambda i: (0, 0)
    full = lambda a: pl.BlockSpec(a.shape, const)
    return pl.pallas_call(
        _ffn_kernel,
        grid=(s // tm,),
        in_specs=[pl.BlockSpec((tm, D_MODEL), row),
                  pl.BlockSpec(w_up.shape, const, pipeline_mode=pl.Buffered(1)),
                  full(conv_w), full(conv_b),
                  pl.BlockSpec(w_down.shape, const, pipeline_mode=pl.Buffered(1)),
                  full(ln_g), full(ln_b)],
        out_specs=pl.BlockSpec((tm, D_MODEL), row),
        out_shape=jax.ShapeDtypeStruct((s, D_MODEL), F32),
        scratch_shapes=[pltpu.VMEM((HALO_ROWS, 2 * D_FF), F32),
                        pltpu.VMEM((tm, D_FF), BF16)],
        compiler_params=pltpu.CompilerParams(dimension_semantics=("arbitrary",),
                                             vmem_limit_bytes=VMEM_LIMIT),
        name="ffn",
    )(h1, w_up, conv_w, conv_b, w_down, ln_g, ln_b)


PREP_BLOCK = 512
DT_LO = SSM_D_INNER + SSM_CONV_DIM
DT_DST = PROJ_COLS - DT_PAD


def _permute_kernel(a_ref, b_ref, d_ref, o_ref):
    j = pl.program_id(0)
    first_shifted = DT_LO // PREP_BLOCK
    last = (PROJ_COLS - 1) // PREP_BLOCK

    @pl.when(j < first_shifted)
    def _():
        o_ref[...] = a_ref[...].T.astype(BF16)

    @pl.when((j >= first_shifted) & (j < last))
    def _():
        rows = jnp.concatenate([a_ref[SSM_HEADS:, :], b_ref[:SSM_HEADS, :]], axis=0)
        o_ref[...] = rows.T.astype(BF16)

    @pl.when(j == last)
    def _():
        n_tail = DT_DST - last * PREP_BLOCK
        rows = jnp.concatenate([a_ref[SSM_HEADS:SSM_HEADS + n_tail, :], d_ref[:SSM_HEADS, :],
                                jnp.zeros((DT_PAD - SSM_HEADS, D_MODEL), F32)], axis=0)
        o_ref[:, :n_tail + DT_PAD] = rows.T.astype(BF16)


def _permute_w_in(w_t):
    n_src = DT_LO + SSM_HEADS + QKV_COLS + GATE_COLS
    assert DT_LO % PREP_BLOCK == 0 and w_t.shape == (n_src, D_MODEL)
    first_shifted = DT_LO // PREP_BLOCK
    last_src = (n_src - 1) // PREP_BLOCK
    blk = lambda f: pl.BlockSpec((PREP_BLOCK, D_MODEL), f)
    return pl.pallas_call(
        _permute_kernel,
        grid=(pl.cdiv(PROJ_COLS, PREP_BLOCK),),
        in_specs=[blk(lambda j: (jnp.minimum(j, last_src), 0)),
                  blk(lambda j: (jnp.clip(j + 1, first_shifted, last_src), 0)),
                  blk(lambda j: (first_shifted, 0))],
        out_specs=pl.BlockSpec((D_MODEL, PREP_BLOCK), lambda j: (0, j)),
        out_shape=jax.ShapeDtypeStruct((D_MODEL, PROJ_COLS), BF16),
        compiler_params=pltpu.CompilerParams(dimension_semantics=("arbitrary",),
                                             vmem_limit_bytes=VMEM_LIMIT),
        name="permute_w_in",
    )(w_t, w_t, w_t)


def _row(v, width=None):
    v = v.astype(F32).reshape(1, -1)
    if width is not None and v.shape[1] < width:
        v = jnp.pad(v, ((0, 0), (0, width - v.shape[1])))
    return v


def kernel(x, rel_bias, w_in, b_gate, ssm_conv_w, ssm_conv_b, ssm_dt_bias, ssm_a_log, ssm_d, ssm_norm_w,
           attn_sinks, w_branch_ssm, w_branch_attn, w_mix_out, ln1_g, ln1_b, w_up, ffn_conv_w, ffn_conv_b,
           w_down, ln2_g, ln2_b):
    b, s, d = x.shape
    assert (b, d) == (1, D_MODEL) and s % 512 == 0 and w_in.shape[0] == DEPTH
    t = _tiles()
    h = x.reshape(s, d)
    bias_rows = _bias_rows(rel_bias)
    for l in range(DEPTH):
        zs, xc, qkv, gates, dt = _inproj(h, _permute_w_in(w_in[l].T), ssm_conv_w[l].astype(F32),
                                         _row(ssm_conv_b[l]), _row(ssm_dt_bias[l], DT_PAD),
                                         _row(b_gate[l]), t["inproj"])
        y_ssm, y_attn = _mixer(xc, zs, dt, qkv, _row(ssm_a_log[l], DT_PAD),
                               _row(jnp.repeat(ssm_d[l], SSM_HEADDIM)), _row(ssm_norm_w[l]),
                               bias_rows, attn_sinks[l].astype(F32), t["mixer"])
        h1 = _merge(y_ssm, y_attn, gates, h, w_branch_ssm[l].astype(BF16), w_branch_attn[l].astype(BF16),
                    w_mix_out[l].astype(BF16), _row(ln1_g[l]), _row(ln1_b[l]), t["merge"])
        h = _ffn(h1, w_up[l].astype(BF16), ffn_conv_w[l].astype(F32), _row(ffn_conv_b[l]),
                 w_down[l].astype(BF16), _row(ln2_g[l]), _row(ln2_b[l]), t["ffn"])
    return h.reshape(b, s, d)
```

```python
import math

import numpy as np
import jax
import jax.numpy as jnp
from jax import lax
from jax.experimental import pallas as pl
from jax.experimental.pallas import tpu as pltpu

F32 = jnp.float32
BF16 = jnp.bfloat16

D_MODEL = 1024
SSM_D_INNER = 2048
SSM_HEADDIM = 64
SSM_HEADS = 32
SSM_GROUPS = 4
SSM_HEADS_PER_GROUP = 8
SSM_STATE = 128
SSM_CONV = 4
SSM_CHUNK = 128
SSM_CONV_DIM = SSM_D_INNER + 2 * SSM_GROUPS * SSM_STATE
GROUP_WIDTH = SSM_HEADS_PER_GROUP * SSM_HEADDIM
ATTN_HEADS = 16
ATTN_KV_HEADS = 2
ATTN_HEADDIM = 64
WINDOW = 128
REL_BUCKETS = 32
REL_MAX_DIST = 128
Q_COLS = ATTN_HEADS * ATTN_HEADDIM
KV_COLS = ATTN_KV_HEADS * ATTN_HEADDIM
QKV_COLS = Q_COLS + 2 * KV_COLS
GATE_COLS = 2 * D_MODEL
D_FF = 2816
FFN_CONV = 3
DEPTH = 1
DEEPNORM_ALPHA = (2.0 * DEPTH) ** 0.25
LN_EPS = 1e-5
RMS_EPS = 1e-5

LANES = 128
SUBLANES = 8
DT_PAD = LANES
HALO_ROWS = SUBLANES
SSD_LOOKAHEAD = 1
ATTN_LOOKAHEAD = 1
ROW_BLOCK = 128
VMEM_LIMIT = 56 * 1024 * 1024

PROJ_SEGMENTS = (SSM_D_INNER, SSM_CONV_DIM, QKV_COLS, GATE_COLS, DT_PAD)
PROJ_COLS = sum(PROJ_SEGMENTS)

NEG_BIG = -1e30
LOG2E = 1.4426950408889634


def _tiles():
    assert SSM_CHUNK == WINDOW
    return dict(inproj=512, mixer=2 * SSM_CHUNK, merge=512, ffn=1024)


def _dot(a, b):
    return jnp.dot(a, b, preferred_element_type=F32)


def _sigmoid(x):
    return 1.0 / (1.0 + jnp.exp2(x * (-LOG2E)))


def _silu(x):
    h = 0.5 * x
    return h + h * jnp.tanh(h)


def _layer_norm(r, g, b):
    mu = jnp.mean(r, axis=-1, keepdims=True)
    rc = r - mu
    var = jnp.mean(rc * rc, axis=-1, keepdims=True)
    return rc * lax.rsqrt(var + LN_EPS) * g + b


def _shift_rows(x, prev, j):
    row = lax.broadcasted_iota(jnp.int32, prev.shape, 0)
    sh = pltpu.roll(x, j, 0)
    top = jnp.where(row < j, pltpu.roll(prev, j, 0), sh[:HALO_ROWS])
    return jnp.concatenate([top, sh[HALO_ROWS:]], axis=0)


def _causal_conv(x, prev, w_ref, b_ref, cols, taps):
    acc = b_ref[:, cols] + w_ref[taps - 1:taps, cols] * x
    for j in range(1, taps):
        acc = acc + w_ref[taps - 1 - j:taps - j, cols] * _shift_rows(x, prev, j)
    return acc


def _inproj_kernel(x_ref, w_ref, cw_ref, cb_ref, dtb_ref, bg_ref,
                   zs_ref, xc_ref, qkv_ref, g_ref, dt_ref, halo_ref):
    tm = x_ref.shape[0]

    @pl.when(pl.program_id(0) == 0)
    def _():
        halo_ref[...] = jnp.zeros_like(halo_ref)

    chunk = 256
    row_blocks = [slice(r, r + ROW_BLOCK) for r in range(0, tm, ROW_BLOCK)]
    xbs = [x_ref[rows, :].astype(BF16) for rows in row_blocks]
    offs = [sum(PROJ_SEGMENTS[:i]) for i in range(len(PROJ_SEGMENTS))]

    def proj(i, seg, c, w):
        return _dot(xbs[i], w_ref[:, offs[seg] + c:offs[seg] + c + w])

    wide_blocks = [slice(r, r + 2 * ROW_BLOCK) for r in range(0, tm, 2 * ROW_BLOCK)]
    xbw = [jnp.concatenate(xbs[2 * i:2 * i + 2], axis=0) for i in range(len(wide_blocks))]

    def proj_wide(i, seg, c, w):
        return _dot(xbw[i], w_ref[:, offs[seg] + c:offs[seg] + c + w])

    for c in range(0, SSM_D_INNER, chunk):
        for i, rows in enumerate(wide_blocks):
            zs_ref[rows, c:c + chunk] = _silu(proj_wide(i, 0, c, chunk)).astype(BF16)
    for c in range(0, SSM_CONV_DIM, chunk):
        cols = slice(c, c + chunk)
        prev = halo_ref[:, cols]
        for i, rows in enumerate(row_blocks):
            u = proj(i, 1, c, chunk)
            xc_ref[rows, cols] = _silu(_causal_conv(u, prev, cw_ref, cb_ref, cols, SSM_CONV)).astype(BF16)
            prev = u[ROW_BLOCK - HALO_ROWS:, :]
        halo_ref[:, cols] = prev
    for c in range(0, QKV_COLS, chunk):
        w = min(chunk, QKV_COLS - c)
        for i, rows in enumerate(wide_blocks):
            qkv_ref[rows, c:c + w] = proj_wide(i, 2, c, w).astype(BF16)
    for c in range(0, GATE_COLS, chunk):
        for i, rows in enumerate(wide_blocks):
            g_ref[rows, c:c + chunk] = _sigmoid(proj_wide(i, 3, c, chunk) + bg_ref[:, c:c + chunk]).astype(BF16)
    lane_ok = lax.broadcasted_iota(jnp.int32, (ROW_BLOCK, DT_PAD), 1) < SSM_HEADS
    for i, rows in enumerate(row_blocks):
        dtr = proj(i, 4, 0, DT_PAD) + dtb_ref[...]
        softplus = jnp.maximum(dtr, 0.0) + jnp.log(1.0 + jnp.exp(-jnp.abs(dtr)))
        dt_ref[rows, :] = jnp.where(lane_ok, softplus, 0.0)


def _inproj(x2d, w_perm, conv_w, conv_b, dt_bias, b_gate, tm):
    s = x2d.shape[0]
    row = lambda i: (i, 0)
    const = lambda i: (0, 0)
    full = lambda a: pl.BlockSpec(a.shape, const)
    outs = [jax.ShapeDtypeStruct((s, n), dt) for n, dt in
            zip(PROJ_SEGMENTS, (BF16, BF16, BF16, BF16, F32))]
    return pl.pallas_call(
        _inproj_kernel,
        grid=(s // tm,),
        in_specs=[pl.BlockSpec((tm, D_MODEL), row),
                  pl.BlockSpec((D_MODEL, PROJ_COLS), const, pipeline_mode=pl.Buffered(1)),
                  full(conv_w), full(conv_b), full(dt_bias), full(b_gate)],
        out_specs=[pl.BlockSpec((tm, n), row) for n in PROJ_SEGMENTS],
        out_shape=outs,
        scratch_shapes=[pltpu.VMEM((HALO_ROWS, SSM_CONV_DIM), F32)],
        compiler_params=pltpu.CompilerParams(dimension_semantics=("arbitrary",),
                                             vmem_limit_bytes=VMEM_LIMIT),
        name="inproj",
    )(x2d, w_perm, conv_w, conv_b, dt_bias, b_gate)


def _split3(x):
    hi = x.astype(BF16).astype(F32)
    r = x - hi
    mid = r.astype(BF16).astype(F32)
    lo = (r - mid).astype(BF16).astype(F32)
    return hi, mid, lo


def _pack3(x):
    hi, mid, lo = _split3(x)
    return (hi + pltpu.roll(mid, SSM_HEADS, 1) + pltpu.roll(lo, 2 * SSM_HEADS, 1)).astype(BF16)


def _ssd_body(conv_ref, zs_ref, dt_ref, alog_ref, dexp_ref, nw_ref, e3_ref, ec3_ref, y_ref, state_ref):
    L = SSM_CHUNK
    lane_ok = lax.broadcasted_iota(jnp.int32, (L, DT_PAD), 1) < SSM_HEADS
    dt = dt_ref[...]
    a_dt = dt * (-LOG2E * jnp.exp(alog_ref[...]))

    ri = lax.broadcasted_iota(jnp.int32, (L, L), 0)
    ci = lax.broadcasted_iota(jnp.int32, (L, L), 1)
    causal = ri >= ci
    tril = jnp.where(causal, 1.0, 0.0).astype(BF16)
    hi, mid, lo = _split3(a_dt)
    a_cs = _dot(tril, hi.astype(BF16)) + _dot(tril, mid.astype(BF16)) + _dot(tril, lo.astype(BF16))

    a_last = a_cs[L - 1:L, :]
    decay_out_b = jnp.where(lane_ok, jnp.exp2(a_cs), 0.0).astype(BF16)
    w_state_b = (dt * jnp.exp2(a_last - a_cs)).astype(BF16)
    cd_e = jnp.exp2(_dot(_pack3(a_cs[L - SUBLANES:, :]), e3_ref[...])[SUBLANES - 1:, :])

    a_sub_t = (a_cs - jnp.log2(dt)).T
    acs3 = _pack3(a_cs)
    lane_lo = lax.broadcasted_iota(jnp.int32, (L, LANES), 1) < SSM_HEADDIM

    def group_start(g):
        gsl = slice(g * GROUP_WIDTH, (g + 1) * GROUP_WIDTH)
        xs_g = conv_ref[:, gsl].astype(F32)
        b_off = SSM_D_INNER + g * SSM_STATE
        c_off = SSM_D_INNER + SSM_GROUPS * SSM_STATE + g * SSM_STATE
        b_t = conv_ref[:, b_off:b_off + SSM_STATE].astype(F32).T.astype(BF16)
        c_g = conv_ref[:, c_off:c_off + SSM_STATE]
        cb = _dot(c_g, b_t)
        xdtd_b = (xs_g * _dot(w_state_b, e3_ref[:, gsl])).astype(BF16)
        st = state_ref[g]
        y_off = _dot(c_g, st.astype(BF16)) * _dot(decay_out_b, e3_ref[:, gsl])
        state_ref[g] = st * cd_e[:, gsl] + _dot(b_t, xdtd_b)
        return cb, y_off + xs_g * dexp_ref[:, gsl]

    def acol_dot(p):
        return _dot(acs3, ec3_ref[:, 2 * p * L:(2 * p + 2) * L])

    pairs_per_group = SSM_HEADS_PER_GROUP // 2
    n_pairs = SSM_HEADS // 2
    started = {0: group_start(0)}
    acols = {p: acol_dot(p) for p in range(SSD_LOOKAHEAD)}
    pieces = []
    for p in range(n_pairs):
        g, pr = divmod(p, pairs_per_group)
        if p + SSD_LOOKAHEAD < n_pairs:
            acols[p + SSD_LOOKAHEAD] = acol_dot(p + SSD_LOOKAHEAD)
        if pr == pairs_per_group - 2 and g + 1 < SSM_GROUPS:
            started[g + 1] = group_start(g + 1)
        cb, y_rest = started[g]
        acol = acols.pop(p)
        ms = []
        for k in range(2):
            seg = acol[:, k * L:(k + 1) * L] - a_sub_t[2 * p + k:2 * p + k + 1, :]
            ms.append((cb * jnp.exp2(jnp.where(causal, seg, NEG_BIG))).astype(BF16))
        xp = conv_ref[:, p * LANES:(p + 1) * LANES]
        zero = jnp.zeros_like(xp)
        rhs = jnp.concatenate([jnp.where(lane_lo, xp, zero), jnp.where(lane_lo, zero, xp)], axis=0)
        pieces.append(_dot(jnp.concatenate(ms, axis=1), rhs))
        if pr == pairs_per_group - 1:
            gsl = slice(g * GROUP_WIDTH, (g + 1) * GROUP_WIDTH)
            yz = (jnp.concatenate(pieces, axis=1) + y_rest) * zs_ref[:, gsl].astype(F32)
            ms_ = jnp.mean(yz * yz, axis=-1, keepdims=True)
            y_ref[:, gsl] = (yz * lax.rsqrt(ms_ + RMS_EPS) * nw_ref[:, gsl]).astype(BF16)
            pieces = []
            del started[g]
        if p % 2 == 1:
            yield


def _expansion_matrices():
    k = np.arange(LANES)[:, None]
    valid = k < 3 * SSM_HEADS
    head = k % SSM_HEADS
    e3 = (valid & (head == (np.arange(SSM_D_INNER)[None, :] // SSM_HEADDIM))).astype(np.float32)
    ec3 = (valid & (head == (np.arange(SSM_HEADS * SSM_CHUNK)[None, :] // SSM_CHUNK))).astype(np.float32)
    return jnp.asarray(e3, BF16), jnp.asarray(ec3, BF16)


def _build_bias_mask(ur_ref, bm_ref):
    W = WINDOW
    first = lax.broadcasted_iota(jnp.int32, (W, 2 * W), 1) >= W
    for h in range(ATTN_HEADS):
        band_h = LOG2E * pltpu.roll(jnp.broadcast_to(ur_ref[h:h + 1, :], (W, 2 * W)), W, 1,
                                    stride=1, stride_axis=0)
        bm_ref[1, h] = band_h
        bm_ref[0, h] = jnp.where(first, band_h, -jnp.inf)


def _attn_body(sink_ref, q_ref, kvc_ref, kvp_ref, o_ref, bm_ref, slab):
    W = WINDOW
    lane_lo2 = lax.broadcasted_iota(jnp.int32, (2 * W, LANES), 1) < ATTN_HEADDIM
    lane_lo = lax.broadcasted_iota(jnp.int32, (W, LANES), 1) < ATTN_HEADDIM

    def band(lo_col):
        return jnp.concatenate([kvp_ref[:, lo_col:lo_col + KV_COLS].astype(F32),
                                kvc_ref[:, lo_col:lo_col + KV_COLS].astype(F32)], axis=0)

    kband = band(0) * (ATTN_HEADDIM ** -0.5 * LOG2E)
    vband = band(KV_COLS)
    vsw = pltpu.roll(vband, ATTN_HEADDIM, 1)
    k_t = kband.T
    k_t_sw = pltpu.roll(k_t, ATTN_HEADDIM, 0)
    row_lo = lax.broadcasted_iota(jnp.int32, (LANES, 2 * W), 0) < ATTN_HEADDIM
    k_var = ((jnp.where(row_lo, k_t, 0.0).astype(BF16), jnp.where(row_lo, 0.0, k_t_sw).astype(BF16)),
             (jnp.where(row_lo, k_t_sw, 0.0).astype(BF16), jnp.where(row_lo, 0.0, k_t).astype(BF16)))
    ones_bd = jnp.concatenate([jnp.where(lane_lo2, 1.0, 0.0), jnp.where(lane_lo2, 0.0, 1.0)], axis=0)
    v_bd = tuple(jnp.concatenate([jnp.concatenate([jnp.where(lane_lo2, top, 0.0), jnp.where(lane_lo2, 0.0, bot)],
                                                  axis=0), ones_bd], axis=1).astype(BF16)
                 for top, bot in ((vband, vsw), (vsw, vband)))

    def kv_head(i):
        return (2 * i) // (ATTN_HEADS // ATTN_KV_HEADS)

    def logits(i):
        qp = q_ref[:, i * LANES:(i + 1) * LANES]
        return [_dot(qp, k_var[kv_head(i)][j]) for j in range(2)]

    n_pairs = ATTN_HEADS // 2
    pending = {i: logits(i) for i in range(ATTN_LOOKAHEAD)}
    for i in range(n_pairs):
        c = kv_head(i)
        if i + ATTN_LOOKAHEAD < n_pairs:
            pending[i + ATTN_LOOKAHEAD] = logits(i + ATTN_LOOKAHEAD)
        s_cur = pending.pop(i)
        ps, sink_terms = [], []
        for j in range(2):
            h = 2 * i + j
            sink = sink_ref[h] * LOG2E
            s = s_cur[j] + bm_ref[slab, h]
            m = jnp.maximum(jnp.max(s, axis=-1, keepdims=True), sink)
            ps.append(jnp.exp2(s - m).astype(BF16))
            sink_terms.append(jnp.exp2(sink - m))
        pv = _dot(jnp.concatenate(ps, axis=1), v_bd[c])
        den = pv[:, LANES:] + jnp.where(lane_lo, sink_terms[0], sink_terms[1])
        o_ref[:, i * LANES:(i + 1) * LANES] = (pv[:, :LANES] / den).astype(BF16)
        yield


def _mixer_kernel(sink_ref, conv_ref, zs_ref, dt_ref, alog_ref, dexp_ref, nw_ref, e3_ref, ec3_ref,
                  q_ref, kvc_ref, kvp_ref, ur_ref, ys_ref, ya_ref, state_ref, bm_ref):
    @pl.when(pl.program_id(0) == 0)
    def _():
        state_ref[...] = jnp.zeros_like(state_ref)
        _build_bias_mask(ur_ref, bm_ref)

    L = SSM_CHUNK
    live = []
    for blk in range(conv_ref.shape[0] // L):
        rows = pl.ds(blk * L, L)
        live.append(_ssd_body(conv_ref.at[rows], zs_ref.at[rows], dt_ref.at[rows], alog_ref, dexp_ref, nw_ref,
                              e3_ref, ec3_ref, ys_ref.at[rows], state_ref))
        prev = kvp_ref if blk == 0 else kvc_ref.at[pl.ds((blk - 1) * L, L)]
        slab = jnp.minimum(pl.program_id(0), 1) if blk == 0 else 1
        live.append(_attn_body(sink_ref, q_ref.at[rows], kvc_ref.at[rows], prev, ya_ref.at[rows], bm_ref, slab))
    while live:
        for body in list(live):
            if next(body, "done") == "done":
                live.remove(body)


def _mixer(xc, zs, dt, qkv, a_log, d_exp, norm_w, bias_rows, sinks, tm):
    s = xc.shape[0]
    e3, ec3 = _expansion_matrices()
    row = lambda i: (i, 0)
    const = lambda i: (0, 0)
    full = lambda a: pl.BlockSpec(a.shape, const)
    kv_blk = Q_COLS // (2 * KV_COLS)
    return pl.pallas_call(
        _mixer_kernel,
        grid=(s // tm,),
        in_specs=[pl.BlockSpec(memory_space=pltpu.SMEM),
                  pl.BlockSpec((tm, SSM_CONV_DIM), row),
                  pl.BlockSpec((tm, SSM_D_INNER), row),
                  pl.BlockSpec((tm, DT_PAD), row),
                  full(a_log), full(d_exp), full(norm_w), full(e3), full(ec3),
                  pl.BlockSpec((tm, Q_COLS), row),
                  pl.BlockSpec((tm, 2 * KV_COLS), lambda i: (i, kv_blk)),
                  pl.BlockSpec((WINDOW, 2 * KV_COLS),
                               lambda i: (jnp.maximum(i * (tm // WINDOW) - 1, 0), kv_blk)),
                  full(bias_rows)],
        out_specs=[pl.BlockSpec((tm, SSM_D_INNER), row), pl.BlockSpec((tm, Q_COLS), row)],
        out_shape=[jax.ShapeDtypeStruct((s, SSM_D_INNER), BF16), jax.ShapeDtypeStruct((s, Q_COLS), BF16)],
        scratch_shapes=[pltpu.VMEM((SSM_GROUPS, SSM_STATE, GROUP_WIDTH), F32),
                        pltpu.VMEM((2, ATTN_HEADS, WINDOW, 2 * WINDOW), F32)],
        compiler_params=pltpu.CompilerParams(dimension_semantics=("arbitrary",),
                                             vmem_limit_bytes=VMEM_LIMIT),
        name="mixer",
    )(sinks, xc, zs, dt, a_log, d_exp, norm_w, e3, ec3, qkv, qkv, qkv, bias_rows)


def _rel_bucket_static(n):
    max_exact = REL_BUCKETS // 2
    nf = np.maximum(n, 1).astype(np.float32)
    large = max_exact + (np.log(nf / max_exact) / math.log(REL_MAX_DIST / max_exact)
                         * (REL_BUCKETS - max_exact)).astype(np.int32)
    return np.where(n < max_exact, n, np.minimum(large, REL_BUCKETS - 1))


def _bias_rows(rel_bias):
    rel = (-np.arange(2 * WINDOW)) % (2 * WINDOW)
    idx = np.where(rel < WINDOW, _rel_bucket_static(rel), REL_BUCKETS)
    table = jnp.concatenate([rel_bias.astype(F32), jnp.full((1, ATTN_HEADS), -jnp.inf, F32)], axis=0)
    return table[idx].T


def _merge_kernel(ys_ref, ya_ref, g_ref, x_ref, wbs_ref, wba_ref, wmix_ref, lg_ref, lb_ref, h_ref):
    chunk = 256
    rb = 2 * ROW_BLOCK
    for r in range(0, ys_ref.shape[0], rb):
        rows = slice(r, r + rb)
        ys, ya = ys_ref[rows, :], ya_ref[rows, :]
        merged = []
        for c in range(0, D_MODEL, chunk):
            a = _dot(ys, wbs_ref[:, c:c + chunk])
            b = _dot(ya, wba_ref[:, c:c + chunk])
            merged.append((g_ref[rows, c:c + chunk].astype(F32) * a
                           + g_ref[rows, D_MODEL + c:D_MODEL + c + chunk].astype(F32) * b).astype(BF16))
        mix = _dot(jnp.concatenate(merged, axis=1), wmix_ref[...])
        h_ref[rows, :] = _layer_norm(DEEPNORM_ALPHA * x_ref[rows, :] + mix, lg_ref[...], lb_ref[...])


def _merge(y_ssm, y_attn, gates, x2d, w_bs, w_ba, w_mix, ln_g, ln_b, tm):
    s = x2d.shape[0]
    row = lambda i: (i, 0)
    const = lambda i: (0, 0)
    full = lambda a: pl.BlockSpec(a.shape, const)
    return pl.pallas_call(
        _merge_kernel,
        grid=(s // tm,),
        in_specs=[pl.BlockSpec((tm, SSM_D_INNER), row), pl.BlockSpec((tm, Q_COLS), row),
                  pl.BlockSpec((tm, GATE_COLS), row), pl.BlockSpec((tm, D_MODEL), row),
                  full(w_bs), full(w_ba), full(w_mix), full(ln_g), full(ln_b)],
        out_specs=pl.BlockSpec((tm, D_MODEL), row),
        out_shape=jax.ShapeDtypeStruct((s, D_MODEL), F32),
        compiler_params=pltpu.CompilerParams(dimension_semantics=("arbitrary",),
                                             vmem_limit_bytes=VMEM_LIMIT),
        name="merge",
    )(y_ssm, y_attn, gates, x2d, w_bs, w_ba, w_mix, ln_g, ln_b)


def _ffn_kernel(h_ref, wup_ref, cw_ref, cb_ref, wdn_ref, lg_ref, lb_ref, o_ref, halo_ref, act_ref):
    tm = h_ref.shape[0]

    @pl.when(pl.program_id(0) == 0)
    def _():
        halo_ref[...] = jnp.zeros_like(halo_ref)

    chunk = 256
    rb = 2 * ROW_BLOCK
    row_blocks = [slice(r, r + rb) for r in range(0, tm, rb)]
    hbs = [h_ref[rows, :].astype(BF16) for rows in row_blocks]

    def conv_cols(hb, cols, prev):
        u = _dot(hb, wup_ref[:, cols])
        return _causal_conv(u, prev, cw_ref, cb_ref, cols, FFN_CONV), u[rb - HALO_ROWS:, :]

    for c in range(0, D_FF, chunk):
        gcols, vcols = slice(c, c + chunk), slice(D_FF + c, D_FF + c + chunk)
        gprev, vprev = halo_ref[:, gcols], halo_ref[:, vcols]
        for hb, rows in zip(hbs, row_blocks):
            gate, gprev = conv_cols(hb, gcols, gprev)
            val, vprev = conv_cols(hb, vcols, vprev)
            act_ref[rows, c:c + chunk] = (_silu(gate) * val).astype(BF16)
        halo_ref[:, gcols] = gprev
        halo_ref[:, vcols] = vprev

    for rows in row_blocks:
        out = _dot(act_ref[rows, :], wdn_ref[...])
        o_ref[rows, :] = _layer_norm(DEEPNORM_ALPHA * h_ref[rows, :] + out, lg_ref[...], lb_ref[...])


def _ffn(h1, w_up, conv_w, conv_b, w_down, ln_g, ln_b, tm):
    s = h1.shape[0]
    row = lambda i: (i, 0)
    const = lambda i: (0, 0)
    full = lambda a: pl.BlockSpec(a.shape, const)
    return pl.pallas_call(
        _ffn_kernel,
        grid=(s // tm,),
        in_specs=[pl.BlockSpec((tm, D_MODEL), row),
                  pl.BlockSpec(w_up.shape, const, pipeline_mode=pl.Buffered(1)),
                  full(conv_w), full(conv_b),
                  pl.BlockSpec(w_down.shape, const, pipeline_mode=pl.Buffered(1)),
                  full(ln_g), full(ln_b)],
        out_specs=pl.BlockSpec((tm, D_MODEL), row),
        out_shape=jax.ShapeDtypeStruct((s, D_MODEL), F32),
        scratch_shapes=[pltpu.VMEM((HALO_ROWS, 2 * D_FF), F32),
                        pltpu.VMEM((tm, D_FF), BF16)],
        compiler_params=pltpu.CompilerParams(dimension_semantics=("arbitrary",),
                                             vmem_limit_bytes=VMEM_LIMIT),
        name="ffn",
    )(h1, w_up, conv_w, conv_b, w_down, ln_g, ln_b)


PREP_BLOCK = 512
DT_LO = SSM_D_INNER + SSM_CONV_DIM
DT_DST = PROJ_COLS - DT_PAD


def _permute_kernel(a_ref, b_ref, d_ref, o_ref):
    j = pl.program_id(0)
    first_shifted = DT_LO // PREP_BLOCK
    last = (PROJ_COLS - 1) // PREP_BLOCK

    @pl.when(j < first_shifted)
    def _():
        o_ref[...] = a_ref[...].T.astype(BF16)

    @pl.when((j >= first_shifted) & (j < last))
    def _():
        rows = jnp.concatenate([a_ref[SSM_HEADS:, :], b_ref[:SSM_HEADS, :]], axis=0)
        o_ref[...] = rows.T.astype(BF16)

    @pl.when(j == last)
    def _():
        n_tail = DT_DST - last * PREP_BLOCK
        rows = jnp.concatenate([a_ref[SSM_HEADS:SSM_HEADS + n_tail, :], d_ref[:SSM_HEADS, :],
                                jnp.zeros((DT_PAD - SSM_HEADS, D_MODEL), F32)], axis=0)
        o_ref[:, :n_tail + DT_PAD] = rows.T.astype(BF16)


def _permute_w_in(w_t):
    n_src = DT_LO + SSM_HEADS + QKV_COLS + GATE_COLS
    assert DT_LO % PREP_BLOCK == 0 and w_t.shape == (n_src, D_MODEL)
    first_shifted = DT_LO // PREP_BLOCK
    last_src = (n_src - 1) // PREP_BLOCK
    blk = lambda f: pl.BlockSpec((PREP_BLOCK, D_MODEL), f)
    return pl.pallas_call(
        _permute_kernel,
        grid=(pl.cdiv(PROJ_COLS, PREP_BLOCK),),
        in_specs=[blk(lambda j: (jnp.minimum(j, last_src), 0)),
                  blk(lambda j: (jnp.clip(j + 1, first_shifted, last_src), 0)),
                  blk(lambda j: (first_shifted, 0))],
        out_specs=pl.BlockSpec((D_MODEL, PREP_BLOCK), lambda j: (0, j)),
        out_shape=jax.ShapeDtypeStruct((D_MODEL, PROJ_COLS), BF16),
        compiler_params=pltpu.CompilerParams(dimension_semantics=("arbitrary",),
                                             vmem_limit_bytes=VMEM_LIMIT),
        name="permute_w_in",
    )(w_t, w_t, w_t)


def _row(v, width=None):
    v = v.astype(F32).reshape(1, -1)
    if width is not None and v.shape[1] < width:
        v = jnp.pad(v, ((0, 0), (0, width - v.shape[1])))
    return v


def kernel(x, rel_bias, w_in, b_gate, ssm_conv_w, ssm_conv_b, ssm_dt_bias, ssm_a_log, ssm_d, ssm_norm_w,
           attn_sinks, w_branch_ssm, w_branch_attn, w_mix_out, ln1_g, ln1_b, w_up, ffn_conv_w, ffn_conv_b,
           w_down, ln2_g, ln2_b):
    b, s, d = x.shape
    assert (b, d) == (1, D_MODEL) and s % 512 == 0 and w_in.shape[0] == DEPTH
    t = _tiles()
    h = x.reshape(s, d)
    bias_rows = _bias_rows(rel_bias)
    for l in range(DEPTH):
        zs, xc, qkv, gates, dt = _inproj(h, _permute_w_in(w_in[l].T), ssm_conv_w[l].astype(F32),
                                         _row(ssm_conv_b[l]), _row(ssm_dt_bias[l], DT_PAD),
                                         _row(b_gate[l]), t["inproj"])
        y_ssm, y_attn = _mixer(xc, zs, dt, qkv, _row(ssm_a_log[l], DT_PAD),
                               _row(jnp.repeat(ssm_d[l], SSM_HEADDIM)), _row(ssm_norm_w[l]),
                               bias_rows, attn_sinks[l].astype(F32), t["mixer"])
        h1 = _merge(y_ssm, y_attn, gates, h, w_branch_ssm[l].astype(BF16), w_branch_attn[l].astype(BF16),
                    w_mix_out[l].astype(BF16), _row(ln1_g[l]), _row(ln1_b[l]), t["merge"])
        h = _ffn(h1, w_up[l].astype(BF16), ffn_conv_w[l].astype(F32), _row(ffn_conv_b[l]),
                 w_down[l].astype(BF16), _row(ln2_g[l]), _row(ln2_b[l]), t["ffn"])
    return h.reshape(b, s, d)
```

```python
import math

import numpy as np
import jax
import jax.numpy as jnp
from jax import lax
from jax.experimental import pallas as pl
from jax.experimental.pallas import tpu as pltpu

F32 = jnp.float32
BF16 = jnp.bfloat16

D_MODEL = 1024
SSM_D_INNER = 2048
SSM_HEADDIM = 64
SSM_HEADS = 32
SSM_GROUPS = 4
SSM_HEADS_PER_GROUP = 8
SSM_STATE = 128
SSM_CONV = 4
SSM_CHUNK = 128
SSM_CONV_DIM = SSM_D_INNER + 2 * SSM_GROUPS * SSM_STATE
GROUP_WIDTH = SSM_HEADS_PER_GROUP * SSM_HEADDIM
ATTN_HEADS = 16
ATTN_KV_HEADS = 2
ATTN_HEADDIM = 64
WINDOW = 128
REL_BUCKETS = 32
REL_MAX_DIST = 128
Q_COLS = ATTN_HEADS * ATTN_HEADDIM
KV_COLS = ATTN_KV_HEADS * ATTN_HEADDIM
QKV_COLS = Q_COLS + 2 * KV_COLS
GATE_COLS = 2 * D_MODEL
D_FF = 2816
FFN_CONV = 3
DEPTH = 1
DEEPNORM_ALPHA = (2.0 * DEPTH) ** 0.25
LN_EPS = 1e-5
RMS_EPS = 1e-5

LANES = 128
SUBLANES = 8
DT_PAD = LANES
HALO_ROWS = SUBLANES
SSD_LOOKAHEAD = 1
ATTN_LOOKAHEAD = 1
MXU_WIDTH = 256
ROW_BLOCK = MXU_WIDTH // 2
VMEM_LIMIT = 56 * 1024 * 1024

PROJ_SEGMENTS = (SSM_D_INNER, SSM_CONV_DIM, QKV_COLS, GATE_COLS, DT_PAD)
PROJ_COLS = sum(PROJ_SEGMENTS)

NEG_BIG = -1e30
LOG2E = 1.4426950408889634


def _tiles():
    assert SSM_CHUNK == WINDOW
    return dict(inproj=512, mixer=2 * SSM_CHUNK, merge=512, ffn=1024)


def _dot(a, b):
    return jnp.dot(a, b, preferred_element_type=F32)


def _sigmoid(x):
    return 1.0 / (1.0 + jnp.exp2(x * (-LOG2E)))


def _silu(x):
    h = 0.5 * x
    return h + h * jnp.tanh(h)


def _layer_norm(r, g, b):
    mu = jnp.mean(r, axis=-1, keepdims=True)
    rc = r - mu
    var = jnp.mean(rc * rc, axis=-1, keepdims=True)
    return rc * lax.rsqrt(var + LN_EPS) * g + b


def _shift_rows(x, prev, j):
    row = lax.broadcasted_iota(jnp.int32, prev.shape, 0)
    sh = pltpu.roll(x, j, 0)
    top = jnp.where(row < j, pltpu.roll(prev, j, 0), sh[:HALO_ROWS])
    return jnp.concatenate([top, sh[HALO_ROWS:]], axis=0)


def _causal_conv(x, prev, w_ref, b_ref, cols, taps):
    acc = b_ref[:, cols] + w_ref[taps - 1:taps, cols] * x
    for j in range(1, taps):
        acc = acc + w_ref[taps - 1 - j:taps - j, cols] * _shift_rows(x, prev, j)
    return acc


def _inproj_kernel(x_ref, w_ref, cw_ref, cb_ref, dtb_ref, bg_ref,
                   zs_ref, xc_ref, qkv_ref, g_ref, dt_ref, halo_ref):
    tm = x_ref.shape[0]

    @pl.when(pl.program_id(0) == 0)
    def _():
        halo_ref[...] = jnp.zeros_like(halo_ref)

    chunk = MXU_WIDTH
    row_blocks = [slice(r, r + ROW_BLOCK) for r in range(0, tm, ROW_BLOCK)]
    xbs = [x_ref[rows, :].astype(BF16) for rows in row_blocks]
    offs = [sum(PROJ_SEGMENTS[:i]) for i in range(len(PROJ_SEGMENTS))]

    def proj(i, seg, c, w):
        return _dot(xbs[i], w_ref[:, offs[seg] + c:offs[seg] + c + w])

    wide_blocks = [slice(r, r + 2 * ROW_BLOCK) for r in range(0, tm, 2 * ROW_BLOCK)]
    xbw = [jnp.concatenate(xbs[2 * i:2 * i + 2], axis=0) for i in range(len(wide_blocks))]

    def proj_wide(i, seg, c, w):
        return _dot(xbw[i], w_ref[:, offs[seg] + c:offs[seg] + c + w])

    for c in range(0, SSM_D_INNER, chunk):
        for i, rows in enumerate(wide_blocks):
            zs_ref[rows, c:c + chunk] = _silu(proj_wide(i, 0, c, chunk)).astype(BF16)
    for c in range(0, SSM_CONV_DIM, chunk):
        cols = slice(c, c + chunk)
        prev = halo_ref[:, cols]
        for i, rows in enumerate(row_blocks):
            u = proj(i, 1, c, chunk)
            xc_ref[rows, cols] = _silu(_causal_conv(u, prev, cw_ref, cb_ref, cols, SSM_CONV)).astype(BF16)
            prev = u[ROW_BLOCK - HALO_ROWS:, :]
        halo_ref[:, cols] = prev
    for c in range(0, QKV_COLS, chunk):
        w = min(chunk, QKV_COLS - c)
        for i, rows in enumerate(wide_blocks):
            qkv_ref[rows, c:c + w] = proj_wide(i, 2, c, w).astype(BF16)
    for c in range(0, GATE_COLS, chunk):
        for i, rows in enumerate(wide_blocks):
            g_ref[rows, c:c + chunk] = _sigmoid(proj_wide(i, 3, c, chunk) + bg_ref[:, c:c + chunk]).astype(BF16)
    lane_ok = lax.broadcasted_iota(jnp.int32, (ROW_BLOCK, DT_PAD), 1) < SSM_HEADS
    for i, rows in enumerate(row_blocks):
        dtr = proj(i, 4, 0, DT_PAD) + dtb_ref[...]
        softplus = jnp.maximum(dtr, 0.0) + jnp.log(1.0 + jnp.exp(-jnp.abs(dtr)))
        dt_ref[rows, :] = jnp.where(lane_ok, softplus, 0.0)


def _inproj(x2d, w_perm, conv_w, conv_b, dt_bias, b_gate, tm):
    s = x2d.shape[0]
    row = lambda i: (i, 0)
    const = lambda i: (0, 0)
    full = lambda a: pl.BlockSpec(a.shape, const)
    outs = [jax.ShapeDtypeStruct((s, n), dt) for n, dt in
            zip(PROJ_SEGMENTS, (BF16, BF16, BF16, BF16, F32))]
    return pl.pallas_call(
        _inproj_kernel,
        grid=(s // tm,),
        in_specs=[pl.BlockSpec((tm, D_MODEL), row),
                  pl.BlockSpec((D_MODEL, PROJ_COLS), const, pipeline_mode=pl.Buffered(1)),
                  full(conv_w), full(conv_b), full(dt_bias), full(b_gate)],
        out_specs=[pl.BlockSpec((tm, n), row) for n in PROJ_SEGMENTS],
        out_shape=outs,
        scratch_shapes=[pltpu.VMEM((HALO_ROWS, SSM_CONV_DIM), F32)],
        compiler_params=pltpu.CompilerParams(dimension_semantics=("arbitrary",),
                                             vmem_limit_bytes=VMEM_LIMIT),
        name="inproj",
    )(x2d, w_perm, conv_w, conv_b, dt_bias, b_gate)


def _split3(x):
    hi = x.astype(BF16).astype(F32)
    r = x - hi
    mid = r.astype(BF16).astype(F32)
    lo = (r - mid).astype(BF16).astype(F32)
    return hi, mid, lo


def _pack3(x):
    hi, mid, lo = _split3(x)
    return (hi + pltpu.roll(mid, SSM_HEADS, 1) + pltpu.roll(lo, 2 * SSM_HEADS, 1)).astype(BF16)


def _ssd_body(conv_ref, zs_ref, dt_ref, alog_ref, dexp_ref, nw_ref, e3_ref, y_ref, state_ref):
    L = SSM_CHUNK
    lane_ok = lax.broadcasted_iota(jnp.int32, (L, DT_PAD), 1) < SSM_HEADS
    dt = dt_ref[...]
    a_dt = dt * (-LOG2E * jnp.exp(alog_ref[...]))

    ri = lax.broadcasted_iota(jnp.int32, (L, L), 0)
    ci = lax.broadcasted_iota(jnp.int32, (L, L), 1)
    causal = ri >= ci
    tril = jnp.where(causal, 1.0, 0.0).astype(BF16)
    hi, mid, lo = _split3(a_dt)
    a_cs = _dot(tril, hi.astype(BF16)) + _dot(tril, mid.astype(BF16)) + _dot(tril, lo.astype(BF16))

    a_last = a_cs[L - 1:L, :]
    decay_out_b = jnp.where(lane_ok, jnp.exp2(a_cs), 0.0).astype(BF16)
    w_state_b = (dt * jnp.exp2(a_last - a_cs)).astype(BF16)
    cd_e = jnp.exp2(_dot(_pack3(a_cs[L - SUBLANES:, :]), e3_ref[...])[SUBLANES - 1:, :])

    a_sub_t = (a_cs - jnp.log2(dt)).T
    lane_lo = lax.broadcasted_iota(jnp.int32, (L, LANES), 1) < SSM_HEADDIM

    def group_start(g):
        gsl = slice(g * GROUP_WIDTH, (g + 1) * GROUP_WIDTH)
        xs_g = conv_ref[:, gsl].astype(F32)
        b_off = SSM_D_INNER + g * SSM_STATE
        c_off = SSM_D_INNER + SSM_GROUPS * SSM_STATE + g * SSM_STATE
        b_t = conv_ref[:, b_off:b_off + SSM_STATE].astype(F32).T.astype(BF16)
        c_g = conv_ref[:, c_off:c_off + SSM_STATE]
        cb = _dot(c_g, b_t)
        xdtd_b = (xs_g * _dot(w_state_b, e3_ref[:, gsl])).astype(BF16)
        st = state_ref[g]
        y_off = _dot(c_g, st.astype(BF16)) * _dot(decay_out_b, e3_ref[:, gsl])
        state_ref[g] = st * cd_e[:, gsl] + _dot(b_t, xdtd_b)
        return cb, y_off + xs_g * dexp_ref[:, gsl]

    def acol_dot(p):
        return jnp.concatenate([jnp.broadcast_to(a_cs[:, 2 * p + k:2 * p + k + 1], (L, L)) for k in range(2)],
                               axis=1)

    pairs_per_group = SSM_HEADS_PER_GROUP // 2
    n_pairs = SSM_HEADS // 2
    started = {0: group_start(0)}
    acols = {p: acol_dot(p) for p in range(SSD_LOOKAHEAD)}
    pieces = []
    for p in range(n_pairs):
        g, pr = divmod(p, pairs_per_group)
        if p + SSD_LOOKAHEAD < n_pairs:
            acols[p + SSD_LOOKAHEAD] = acol_dot(p + SSD_LOOKAHEAD)
        if pr == pairs_per_group - 2 and g + 1 < SSM_GROUPS:
            started[g + 1] = group_start(g + 1)
        cb, y_rest = started[g]
        acol = acols.pop(p)
        ms = []
        for k in range(2):
            seg = acol[:, k * L:(k + 1) * L] - a_sub_t[2 * p + k:2 * p + k + 1, :]
            ms.append((cb * jnp.exp2(jnp.where(causal, seg, NEG_BIG))).astype(BF16))
        xp = conv_ref[:, p * LANES:(p + 1) * LANES]
        zero = jnp.zeros_like(xp)
        rhs = jnp.concatenate([jnp.where(lane_lo, xp, zero), jnp.where(lane_lo, zero, xp)], axis=0)
        pieces.append(_dot(jnp.concatenate(ms, axis=1), rhs))
        if pr == pairs_per_group - 1:
            gsl = slice(g * GROUP_WIDTH, (g + 1) * GROUP_WIDTH)
            yz = (jnp.concatenate(pieces, axis=1) + y_rest) * zs_ref[:, gsl].astype(F32)
            ms_ = jnp.mean(yz * yz, axis=-1, keepdims=True)
            y_ref[:, gsl] = (yz * lax.rsqrt(ms_ + RMS_EPS) * nw_ref[:, gsl]).astype(BF16)
            pieces = []
            del started[g]
        if p % 2 == 1:
            yield


def _expansion_matrix():
    k = np.arange(LANES)[:, None]
    valid = k < 3 * SSM_HEADS
    head = k % SSM_HEADS
    e3 = (valid & (head == (np.arange(SSM_D_INNER)[None, :] // SSM_HEADDIM))).astype(np.float32)
    return jnp.asarray(e3, BF16)


def _build_bias_mask(ur_ref, bm_ref):
    W = WINDOW
    first = lax.broadcasted_iota(jnp.int32, (W, 2 * W), 1) >= W
    for h in range(ATTN_HEADS):
        band_h = LOG2E * pltpu.roll(jnp.broadcast_to(ur_ref[h:h + 1, :], (W, 2 * W)), W, 1,
                                    stride=1, stride_axis=0)
        bm_ref[1, h] = band_h
        bm_ref[0, h] = jnp.where(first, band_h, -jnp.inf)


def _attn_body(sink_ref, q_ref, kvc_ref, kvp_ref, o_ref, bm_ref, slab):
    W = WINDOW
    lane_lo2 = lax.broadcasted_iota(jnp.int32, (2 * W, LANES), 1) < ATTN_HEADDIM
    lane_lo = lax.broadcasted_iota(jnp.int32, (W, LANES), 1) < ATTN_HEADDIM

    def band(lo_col):
        return jnp.concatenate([kvp_ref[:, lo_col:lo_col + KV_COLS].astype(F32),
                                kvc_ref[:, lo_col:lo_col + KV_COLS].astype(F32)], axis=0)

    kband = band(0) * (ATTN_HEADDIM ** -0.5 * LOG2E)
    vband = band(KV_COLS)
    vsw = pltpu.roll(vband, ATTN_HEADDIM, 1)
    k_t = kband.T
    k_t_sw = pltpu.roll(k_t, ATTN_HEADDIM, 0)
    row_lo = lax.broadcasted_iota(jnp.int32, (LANES, 2 * W), 0) < ATTN_HEADDIM
    k_var = ((jnp.where(row_lo, k_t, 0.0).astype(BF16), jnp.where(row_lo, 0.0, k_t_sw).astype(BF16)),
             (jnp.where(row_lo, k_t_sw, 0.0).astype(BF16), jnp.where(row_lo, 0.0, k_t).astype(BF16)))
    ones_bd = jnp.concatenate([jnp.where(lane_lo2, 1.0, 0.0), jnp.where(lane_lo2, 0.0, 1.0)], axis=0)
    v_bd = tuple(jnp.concatenate([jnp.concatenate([jnp.where(lane_lo2, top, 0.0), jnp.where(lane_lo2, 0.0, bot)],
                                                  axis=0), ones_bd], axis=1).astype(BF16)
                 for top, bot in ((vband, vsw), (vsw, vband)))

    def kv_head(i):
        return (2 * i) // (ATTN_HEADS // ATTN_KV_HEADS)

    def logits(i):
        qp = q_ref[:, i * LANES:(i + 1) * LANES]
        return [_dot(qp, k_var[kv_head(i)][j]) for j in range(2)]

    n_pairs = ATTN_HEADS // 2
    pending = {i: logits(i) for i in range(ATTN_LOOKAHEAD)}
    for i in range(n_pairs):
        c = kv_head(i)
        if i + ATTN_LOOKAHEAD < n_pairs:
            pending[i + ATTN_LOOKAHEAD] = logits(i + ATTN_LOOKAHEAD)
        s_cur = pending.pop(i)
        ps, sink_terms = [], []
        for j in range(2):
            h = 2 * i + j
            sink = sink_ref[h] * LOG2E
            s = s_cur[j] + bm_ref[slab, h]
            m = jnp.maximum(jnp.max(s, axis=-1, keepdims=True), sink)
            ps.append(jnp.exp2(s - m).astype(BF16))
            sink_terms.append(jnp.exp2(sink - m))
        pv = _dot(jnp.concatenate(ps, axis=1), v_bd[c])
        den = pv[:, LANES:] + jnp.where(lane_lo, sink_terms[0], sink_terms[1])
        o_ref[:, i * LANES:(i + 1) * LANES] = (pv[:, :LANES] / den).astype(BF16)
        yield


def _mixer_kernel(sink_ref, conv_ref, zs_ref, dt_ref, alog_ref, dexp_ref, nw_ref, e3_ref,
                  q_ref, kvc_ref, kvp_ref, ur_ref, ys_ref, ya_ref, state_ref, bm_ref):
    @pl.when(pl.program_id(0) == 0)
    def _():
        state_ref[...] = jnp.zeros_like(state_ref)
        _build_bias_mask(ur_ref, bm_ref)

    L = SSM_CHUNK
    live = []
    for blk in range(conv_ref.shape[0] // L):
        rows = pl.ds(blk * L, L)
        live.append(_ssd_body(conv_ref.at[rows], zs_ref.at[rows], dt_ref.at[rows], alog_ref, dexp_ref, nw_ref,
                              e3_ref, ys_ref.at[rows], state_ref))
        prev = kvp_ref if blk == 0 else kvc_ref.at[pl.ds((blk - 1) * L, L)]
        slab = jnp.minimum(pl.program_id(0), 1) if blk == 0 else 1
        live.append(_attn_body(sink_ref, q_ref.at[rows], kvc_ref.at[rows], prev, ya_ref.at[rows], bm_ref, slab))
    while live:
        for body in list(live):
            if next(body, "done") == "done":
                live.remove(body)


def _mixer(xc, zs, dt, qkv, a_log, d_exp, norm_w, bias_rows, sinks, tm):
    s = xc.shape[0]
    e3 = _expansion_matrix()
    row = lambda i: (i, 0)
    const = lambda i: (0, 0)
    full = lambda a: pl.BlockSpec(a.shape, const)
    kv_blk = Q_COLS // (2 * KV_COLS)
    return pl.pallas_call(
        _mixer_kernel,
        grid=(s // tm,),
        in_specs=[pl.BlockSpec(memory_space=pltpu.SMEM),
                  pl.BlockSpec((tm, SSM_CONV_DIM), row),
                  pl.BlockSpec((tm, SSM_D_INNER), row),
                  pl.BlockSpec((tm, DT_PAD), row),
                  full(a_log), full(d_exp), full(norm_w), full(e3),
                  pl.BlockSpec((tm, Q_COLS), row),
                  pl.BlockSpec((tm, 2 * KV_COLS), lambda i: (i, kv_blk)),
                  pl.BlockSpec((WINDOW, 2 * KV_COLS),
                               lambda i: (jnp.maximum(i * (tm // WINDOW) - 1, 0), kv_blk)),
                  full(bias_rows)],
        out_specs=[pl.BlockSpec((tm, SSM_D_INNER), row), pl.BlockSpec((tm, Q_COLS), row)],
        out_shape=[jax.ShapeDtypeStruct((s, SSM_D_INNER), BF16), jax.ShapeDtypeStruct((s, Q_COLS), BF16)],
        scratch_shapes=[pltpu.VMEM((SSM_GROUPS, SSM_STATE, GROUP_WIDTH), F32),
                        pltpu.VMEM((2, ATTN_HEADS, WINDOW, 2 * WINDOW), F32)],
        compiler_params=pltpu.CompilerParams(dimension_semantics=("arbitrary",),
                                             vmem_limit_bytes=VMEM_LIMIT),
        name="mixer",
    )(sinks, xc, zs, dt, a_log, d_exp, norm_w, e3, qkv, qkv, qkv, bias_rows)


def _rel_bucket_static(n):
    max_exact = REL_BUCKETS // 2
    nf = np.maximum(n, 1).astype(np.float32)
    large = max_exact + (np.log(nf / max_exact) / math.log(REL_MAX_DIST / max_exact)
                         * (REL_BUCKETS - max_exact)).astype(np.int32)
    return np.where(n < max_exact, n, np.minimum(large, REL_BUCKETS - 1))


def _bias_rows(rel_bias):
    rel = (-np.arange(2 * WINDOW)) % (2 * WINDOW)
    idx = np.where(rel < WINDOW, _rel_bucket_static(rel), REL_BUCKETS)
    table = jnp.concatenate([rel_bias.astype(F32), jnp.full((1, ATTN_HEADS), -jnp.inf, F32)], axis=0)
    return table[idx].T


def _merge_kernel(ys_ref, ya_ref, g_ref, x_ref, wbs_ref, wba_ref, wmix_ref, lg_ref, lb_ref, h_ref):
    chunk = MXU_WIDTH
    rb = 2 * ROW_BLOCK
    for r in range(0, ys_ref.shape[0], rb):
        rows = slice(r, r + rb)
        ys, ya = ys_ref[rows, :], ya_ref[rows, :]
        merged = []
        for c in range(0, D_MODEL, chunk):
            a = _dot(ys, wbs_ref[:, c:c + chunk])
            b = _dot(ya, wba_ref[:, c:c + chunk])
            merged.append((g_ref[rows, c:c + chunk].astype(F32) * a
                           + g_ref[rows, D_MODEL + c:D_MODEL + c + chunk].astype(F32) * b).astype(BF16))
        mix = _dot(jnp.concatenate(merged, axis=1), wmix_ref[...])
        h_ref[rows, :] = _layer_norm(DEEPNORM_ALPHA * x_ref[rows, :] + mix, lg_ref[...], lb_ref[...])


def _merge(y_ssm, y_attn, gates, x2d, w_bs, w_ba, w_mix, ln_g, ln_b, tm):
    s = x2d.shape[0]
    row = lambda i: (i, 0)
    const = lambda i: (0, 0)
    full = lambda a: pl.BlockSpec(a.shape, const)
    return pl.pallas_call(
        _merge_kernel,
        grid=(s // tm,),
        in_specs=[pl.BlockSpec((tm, SSM_D_INNER), row), pl.BlockSpec((tm, Q_COLS), row),
                  pl.BlockSpec((tm, GATE_COLS), row), pl.BlockSpec((tm, D_MODEL), row),
                  full(w_bs), full(w_ba), full(w_mix), full(ln_g), full(ln_b)],
        out_specs=pl.BlockSpec((tm, D_MODEL), row),
        out_shape=jax.ShapeDtypeStruct((s, D_MODEL), F32),
        compiler_params=pltpu.CompilerParams(dimension_semantics=("arbitrary",),
                                             vmem_limit_bytes=VMEM_LIMIT),
        name="merge",
    )(y_ssm, y_attn, gates, x2d, w_bs, w_ba, w_mix, ln_g, ln_b)


def _ffn_kernel(h_ref, wup_ref, cw_ref, cb_ref, wdn_ref, lg_ref, lb_ref, o_ref, halo_ref, act_ref):
    tm = h_ref.shape[0]

    @pl.when(pl.program_id(0) == 0)
    def _():
        halo_ref[...] = jnp.zeros_like(halo_ref)

    chunk = MXU_WIDTH
    rb = 2 * ROW_BLOCK
    row_blocks = [slice(r, r + rb) for r in range(0, tm, rb)]
    hbs = [h_ref[rows, :].astype(BF16) for rows in row_blocks]

    def conv_cols(hb, cols, prev):
        u = _dot(hb, wup_ref[:, cols])
        return _causal_conv(u, prev, cw_ref, cb_ref, cols, FFN_CONV), u[rb - HALO_ROWS:, :]

    for c in range(0, D_FF, chunk):
        gcols, vcols = slice(c, c + chunk), slice(D_FF + c, D_FF + c + chunk)
        gprev, vprev = halo_ref[:, gcols], halo_ref[:, vcols]
        for hb, rows in zip(hbs, row_blocks):
            gate, gprev = conv_cols(hb, gcols, gprev)
            val, vprev = conv_cols(hb, vcols, vprev)
            act_ref[rows, c:c + chunk] = (_silu(gate) * val).astype(BF16)
        halo_ref[:, gcols] = gprev
        halo_ref[:, vcols] = vprev

    for rows in row_blocks:
        out = _dot(act_ref[rows, :], wdn_ref[...])
        o_ref[rows, :] = _layer_norm(DEEPNORM_ALPHA * h_ref[rows, :] + out, lg_ref[...], lb_ref[...])


def _ffn(h1, w_up, conv_w, conv_b, w_down, ln_g, ln_b, tm):
    s = h1.shape[0]
    row = lambda i: (i, 0)
    const = lambda i: (0, 0)
    full = lambda a: pl.BlockSpec(a.shape, const)
    return pl.pallas_call(
        _ffn_kernel,
        grid=(s // tm,),
        in_specs=[pl.BlockSpec((tm, D_MODEL), row),
                  pl.BlockSpec(w_up.shape, const, pipeline_mode=pl.Buffered(1)),
                  full(conv_w), full(conv_b),
                  pl.BlockSpec(w_down.shape, const, pipeline_mode=pl.Buffered(1)),
                  full(ln_g), full(ln_b)],
        out_specs=pl.BlockSpec((tm, D_MODEL), row),
        out_shape=jax.ShapeDtypeStruct((s, D_MODEL), F32),
        scratch_shapes=[pltpu.VMEM((HALO_ROWS, 2 * D_FF), F32),
                        pltpu.VMEM((tm, D_FF), BF16)],
        compiler_params=pltpu.CompilerParams(dimension_semantics=("arbitrary",),
                                             vmem_limit_bytes=VMEM_LIMIT),
        name="ffn",
    )(h1, w_up, conv_w, conv_b, w_down, ln_g, ln_b)


PREP_BLOCK = 512
DT_LO = SSM_D_INNER + SSM_CONV_DIM
DT_DST = PROJ_COLS - DT_PAD


def _permute_kernel(a_ref, b_ref, d_ref, o_ref):
    j = pl.program_id(0)
    first_shifted = DT_LO // PREP_BLOCK
    last = (PROJ_COLS - 1) // PREP_BLOCK

    @pl.when(j < first_shifted)
    def _():
        o_ref[...] = a_ref[...].T.astype(BF16)

    @pl.when((j >= first_shifted) & (j < last))
    def _():
        rows = jnp.concatenate([a_ref[SSM_HEADS:, :], b_ref[:SSM_HEADS, :]], axis=0)
        o_ref[...] = rows.T.astype(BF16)

    @pl.when(j == last)
    def _():
        n_tail = DT_DST - last * PREP_BLOCK
        rows = jnp.concatenate([a_ref[SSM_HEADS:SSM_HEADS + n_tail, :], d_ref[:SSM_HEADS, :],
                                jnp.zeros((DT_PAD - SSM_HEADS, D_MODEL), F32)], axis=0)
        o_ref[:, :n_tail + DT_PAD] = rows.T.astype(BF16)


def _permute_w_in(w_t):
    n_src = DT_LO + SSM_HEADS + QKV_COLS + GATE_COLS
    assert DT_LO % PREP_BLOCK == 0 and w_t.shape == (n_src, D_MODEL)
    first_shifted = DT_LO // PREP_BLOCK
    last_src = (n_src - 1) // PREP_BLOCK
    blk = lambda f: pl.BlockSpec((PREP_BLOCK, D_MODEL), f)
    return pl.pallas_call(
        _permute_kernel,
        grid=(pl.cdiv(PROJ_COLS, PREP_BLOCK),),
        in_specs=[blk(lambda j: (jnp.minimum(j, last_src), 0)),
                  blk(lambda j: (jnp.clip(j + 1, first_shifted, last_src), 0)),
                  blk(lambda j: (first_shifted, 0))],
        out_specs=pl.BlockSpec((D_MODEL, PREP_BLOCK), lambda j: (0, j)),
        out_shape=jax.ShapeDtypeStruct((D_MODEL, PROJ_COLS), BF16),
        compiler_params=pltpu.CompilerParams(dimension_semantics=("arbitrary",),
                                             vmem_limit_bytes=VMEM_LIMIT),
        name="permute_w_in",
    )(w_t, w_t, w_t)


def _row(v, width=None):
    v = v.astype(F32).reshape(1, -1)
    if width is not None and v.shape[1] < width:
        v = jnp.pad(v, ((0, 0), (0, width - v.shape[1])))
    return v


def kernel(x, rel_bias, w_in, b_gate, ssm_conv_w, ssm_conv_b, ssm_dt_bias, ssm_a_log, ssm_d, ssm_norm_w,
           attn_sinks, w_branch_ssm, w_branch_attn, w_mix_out, ln1_g, ln1_b, w_up, ffn_conv_w, ffn_conv_b,
           w_down, ln2_g, ln2_b):
    b, s, d = x.shape
    assert (b, d) == (1, D_MODEL) and s % 512 == 0 and w_in.shape[0] == DEPTH
    t = _tiles()
    h = x.reshape(s, d)
    bias_rows = _bias_rows(rel_bias)
    for l in range(DEPTH):
        zs, xc, qkv, gates, dt = _inproj(h, _permute_w_in(w_in[l].T), ssm_conv_w[l].astype(F32),
                                         _row(ssm_conv_b[l]), _row(ssm_dt_bias[l], DT_PAD),
                                         _row(b_gate[l]), t["inproj"])
        y_ssm, y_attn = _mixer(xc, zs, dt, qkv, _row(ssm_a_log[l], DT_PAD),
                               _row(jnp.repeat(ssm_d[l], SSM_HEADDIM)), _row(ssm_norm_w[l]),
                               bias_rows, attn_sinks[l].astype(F32), t["mixer"])
        h1 = _merge(y_ssm, y_attn, gates, h, w_branch_ssm[l].astype(BF16), w_branch_attn[l].astype(BF16),
                    w_mix_out[l].astype(BF16), _row(ln1_g[l]), _row(ln1_b[l]), t["merge"])
        h = _ffn(h1, w_up[l].astype(BF16), ffn_conv_w[l].astype(F32), _row(ffn_conv_b[l]),
                 w_down[l].astype(BF16), _row(ln2_g[l]), _row(ln2_b[l]), t["ffn"])
    return h.reshape(b, s, d)
```

```python
import math

import numpy as np
import jax
import jax.numpy as jnp
from jax import lax
from jax.experimental import pallas as pl
from jax.experimental.pallas import tpu as pltpu

F32 = jnp.float32
BF16 = jnp.bfloat16

D_MODEL = 1024
SSM_D_INNER = 2048
SSM_HEADDIM = 64
SSM_HEADS = 32
SSM_GROUPS = 4
SSM_HEADS_PER_GROUP = 8
SSM_STATE = 128
SSM_CONV = 4
SSM_CHUNK = 128
SSM_CONV_DIM = SSM_D_INNER + 2 * SSM_GROUPS * SSM_STATE
GROUP_WIDTH = SSM_HEADS_PER_GROUP * SSM_HEADDIM
ATTN_HEADS = 16
ATTN_KV_HEADS = 2
ATTN_HEADDIM = 64
WINDOW = 128
REL_BUCKETS = 32
REL_MAX_DIST = 128
Q_COLS = ATTN_HEADS * ATTN_HEADDIM
KV_COLS = ATTN_KV_HEADS * ATTN_HEADDIM
QKV_COLS = Q_COLS + 2 * KV_COLS
GATE_COLS = 2 * D_MODEL
D_FF = 2816
FFN_CONV = 3
DEPTH = 1
DEEPNORM_ALPHA = (2.0 * DEPTH) ** 0.25
LN_EPS = 1e-5
RMS_EPS = 1e-5

LANES = 128
SUBLANES = 8
DT_PAD = LANES
HALO_ROWS = SUBLANES
SSD_LOOKAHEAD = 1
ATTN_LOOKAHEAD = 1
MXU_WIDTH = 256
ROW_BLOCK = MXU_WIDTH // 2
VMEM_LIMIT = 56 * 1024 * 1024

PROJ_SEGMENTS = (SSM_D_INNER, SSM_CONV_DIM, QKV_COLS, GATE_COLS, DT_PAD)
PROJ_COLS = sum(PROJ_SEGMENTS)

NEG_BIG = -1e30
LOG2E = 1.4426950408889634


def _tiles():
    assert SSM_CHUNK == WINDOW
    return dict(inproj=512, mixer=2 * SSM_CHUNK, merge=512, ffn=1024)


def _dot(a, b):
    return jnp.dot(a, b, preferred_element_type=F32)


def _sigmoid(x):
    return 1.0 / (1.0 + jnp.exp2(x * (-LOG2E)))


def _silu(x):
    h = 0.5 * x
    return h + h * jnp.tanh(h)


def _layer_norm(r, g, b):
    mu = jnp.mean(r, axis=-1, keepdims=True)
    rc = r - mu
    var = jnp.mean(rc * rc, axis=-1, keepdims=True)
    return rc * lax.rsqrt(var + LN_EPS) * g + b


def _shift_rows(x, prev, j):
    row = lax.broadcasted_iota(jnp.int32, prev.shape, 0)
    sh = pltpu.roll(x, j, 0)
    top = jnp.where(row < j, pltpu.roll(prev, j, 0), sh[:HALO_ROWS])
    return jnp.concatenate([top, sh[HALO_ROWS:]], axis=0)


def _causal_conv(x, prev, w_ref, b_ref, cols, taps):
    acc = b_ref[:, cols] + w_ref[taps - 1:taps, cols] * x
    for j in range(1, taps):
        acc = acc + w_ref[taps - 1 - j:taps - j, cols] * _shift_rows(x, prev, j)
    return acc


def _inproj_kernel(x_ref, w_ref, cw_ref, cb_ref, dtb_ref, bg_ref,
                   zs_ref, xc_ref, qkv_ref, g_ref, dt_ref, halo_ref):
    tm = x_ref.shape[0]

    @pl.when(pl.program_id(0) == 0)
    def _():
        halo_ref[...] = jnp.zeros_like(halo_ref)

    chunk = MXU_WIDTH
    row_blocks = [slice(r, r + ROW_BLOCK) for r in range(0, tm, ROW_BLOCK)]
    xbs = [x_ref[rows, :].astype(BF16) for rows in row_blocks]
    offs = [sum(PROJ_SEGMENTS[:i]) for i in range(len(PROJ_SEGMENTS))]

    def proj(i, seg, c, w):
        return _dot(xbs[i], w_ref[:, offs[seg] + c:offs[seg] + c + w])

    wide_blocks = [slice(r, r + 2 * ROW_BLOCK) for r in range(0, tm, 2 * ROW_BLOCK)]
    xbw = [jnp.concatenate(xbs[2 * i:2 * i + 2], axis=0) for i in range(len(wide_blocks))]

    def proj_wide(i, seg, c, w):
        return _dot(xbw[i], w_ref[:, offs[seg] + c:offs[seg] + c + w])

    for c in range(0, SSM_D_INNER, chunk):
        for i, rows in enumerate(wide_blocks):
            zs_ref[rows, c:c + chunk] = _silu(proj_wide(i, 0, c, chunk)).astype(BF16)
    for c in range(0, SSM_CONV_DIM, chunk):
        cols = slice(c, c + chunk)
        prev = halo_ref[:, cols]
        for i, rows in enumerate(row_blocks):
            u = proj(i, 1, c, chunk)
            xc_ref[rows, cols] = _silu(_causal_conv(u, prev, cw_ref, cb_ref, cols, SSM_CONV)).astype(BF16)
            prev = u[ROW_BLOCK - HALO_ROWS:, :]
        halo_ref[:, cols] = prev
    for c in range(0, QKV_COLS, chunk):
        w = min(chunk, QKV_COLS - c)
        for i, rows in enumerate(wide_blocks):
            qkv_ref[rows, c:c + w] = proj_wide(i, 2, c, w).astype(BF16)
    for c in range(0, GATE_COLS, chunk):
        for i, rows in enumerate(wide_blocks):
            g_ref[rows, c:c + chunk] = _sigmoid(proj_wide(i, 3, c, chunk) + bg_ref[:, c:c + chunk]).astype(BF16)
    lane_ok = lax.broadcasted_iota(jnp.int32, (ROW_BLOCK, DT_PAD), 1) < SSM_HEADS
    for i, rows in enumerate(row_blocks):
        dtr = proj(i, 4, 0, DT_PAD) + dtb_ref[...]
        softplus = jnp.maximum(dtr, 0.0) + jnp.log(1.0 + jnp.exp(-jnp.abs(dtr)))
        dt_ref[rows, :] = jnp.where(lane_ok, softplus, 0.0)


def _inproj(x2d, w_perm, conv_w, conv_b, dt_bias, b_gate, tm):
    s = x2d.shape[0]
    row = lambda i: (i, 0)
    const = lambda i: (0, 0)
    full = lambda a: pl.BlockSpec(a.shape, const)
    outs = [jax.ShapeDtypeStruct((s, n), dt) for n, dt in
            zip(PROJ_SEGMENTS, (BF16, BF16, BF16, BF16, F32))]
    return pl.pallas_call(
        _inproj_kernel,
        grid=(s // tm,),
        in_specs=[pl.BlockSpec((tm, D_MODEL), row),
                  pl.BlockSpec((D_MODEL, PROJ_COLS), const, pipeline_mode=pl.Buffered(1)),
                  full(conv_w), full(conv_b), full(dt_bias), full(b_gate)],
        out_specs=[pl.BlockSpec((tm, n), row) for n in PROJ_SEGMENTS],
        out_shape=outs,
        scratch_shapes=[pltpu.VMEM((HALO_ROWS, SSM_CONV_DIM), F32)],
        compiler_params=pltpu.CompilerParams(dimension_semantics=("arbitrary",),
                                             vmem_limit_bytes=VMEM_LIMIT),
        name="inproj",
    )(x2d, w_perm, conv_w, conv_b, dt_bias, b_gate)


def _split3(x):
    hi = x.astype(BF16).astype(F32)
    r = x - hi
    mid = r.astype(BF16).astype(F32)
    lo = (r - mid).astype(BF16).astype(F32)
    return hi, mid, lo


def _pack3(x):
    hi, mid, lo = _split3(x)
    return (hi + pltpu.roll(mid, SSM_HEADS, 1) + pltpu.roll(lo, 2 * SSM_HEADS, 1)).astype(BF16)


def _ssd_body(conv_ref, zs_ref, dt_ref, alog_ref, dexp_ref, nw_ref, e3_ref, y_ref, state_ref):
    L = SSM_CHUNK
    lane_ok = lax.broadcasted_iota(jnp.int32, (L, DT_PAD), 1) < SSM_HEADS
    dt = dt_ref[...]
    a_dt = dt * (-LOG2E * jnp.exp(alog_ref[...]))

    ri = lax.broadcasted_iota(jnp.int32, (L, L), 0)
    ci = lax.broadcasted_iota(jnp.int32, (L, L), 1)
    causal = ri >= ci
    tril = jnp.where(causal, 1.0, 0.0).astype(BF16)
    hi, mid, lo = _split3(a_dt)
    a_cs = _dot(tril, hi.astype(BF16)) + _dot(tril, mid.astype(BF16)) + _dot(tril, lo.astype(BF16))

    a_last = a_cs[L - 1:L, :]
    decay_out_b = jnp.where(lane_ok, jnp.exp2(a_cs), 0.0).astype(BF16)
    w_state_b = (dt * jnp.exp2(a_last - a_cs)).astype(BF16)
    cd_e = jnp.exp2(_dot(_pack3(a_cs[L - SUBLANES:, :]), e3_ref[...])[SUBLANES - 1:, :])

    a_sub_t = (a_cs - jnp.log2(dt)).T
    lane_lo = lax.broadcasted_iota(jnp.int32, (L, LANES), 1) < SSM_HEADDIM

    def group_start(g):
        gsl = slice(g * GROUP_WIDTH, (g + 1) * GROUP_WIDTH)
        xs_g = conv_ref[:, gsl].astype(F32)
        b_off = SSM_D_INNER + g * SSM_STATE
        c_off = SSM_D_INNER + SSM_GROUPS * SSM_STATE + g * SSM_STATE
        b_t = conv_ref[:, b_off:b_off + SSM_STATE].astype(F32).T.astype(BF16)
        c_g = conv_ref[:, c_off:c_off + SSM_STATE]
        cb = _dot(c_g, b_t)
        xdtd_b = (xs_g * _dot(w_state_b, e3_ref[:, gsl])).astype(BF16)
        st = state_ref[g]
        y_off = _dot(c_g, st.astype(BF16)) * _dot(decay_out_b, e3_ref[:, gsl])
        state_ref[g] = st * cd_e[:, gsl] + _dot(b_t, xdtd_b)
        return cb, y_off + xs_g * dexp_ref[:, gsl]

    def acol_dot(p):
        return jnp.concatenate([jnp.broadcast_to(a_cs[:, 2 * p + k:2 * p + k + 1], (L, L)) for k in range(2)],
                               axis=1)

    pairs_per_group = SSM_HEADS_PER_GROUP // 2
    n_pairs = SSM_HEADS // 2
    started = {0: group_start(0)}
    acols = {p: acol_dot(p) for p in range(SSD_LOOKAHEAD)}
    pieces = []
    for p in range(n_pairs):
        g, pr = divmod(p, pairs_per_group)
        if p + SSD_LOOKAHEAD < n_pairs:
            acols[p + SSD_LOOKAHEAD] = acol_dot(p + SSD_LOOKAHEAD)
        if pr == pairs_per_group - 2 and g + 1 < SSM_GROUPS:
            started[g + 1] = group_start(g + 1)
        cb, y_rest = started[g]
        acol = acols.pop(p)
        ms = []
        for k in range(2):
            seg = acol[:, k * L:(k + 1) * L] - a_sub_t[2 * p + k:2 * p + k + 1, :]
            ms.append((cb * jnp.exp2(jnp.where(causal, seg, NEG_BIG))).astype(BF16))
        xp = conv_ref[:, p * LANES:(p + 1) * LANES]
        zero = jnp.zeros_like(xp)
        rhs = jnp.concatenate([jnp.where(lane_lo, xp, zero), jnp.where(lane_lo, zero, xp)], axis=0)
        pieces.append(_dot(jnp.concatenate(ms, axis=1), rhs))
        if pr == pairs_per_group - 1:
            gsl = slice(g * GROUP_WIDTH, (g + 1) * GROUP_WIDTH)
            yz = (jnp.concatenate(pieces, axis=1) + y_rest) * zs_ref[:, gsl].astype(F32)
            ms_ = jnp.mean(yz * yz, axis=-1, keepdims=True)
            y_ref[:, gsl] = (yz * lax.rsqrt(ms_ + RMS_EPS) * nw_ref[:, gsl]).astype(BF16)
            pieces = []
            del started[g]
        yield


def _expansion_matrix():
    k = np.arange(LANES)[:, None]
    valid = k < 3 * SSM_HEADS
    head = k % SSM_HEADS
    e3 = (valid & (head == (np.arange(SSM_D_INNER)[None, :] // SSM_HEADDIM))).astype(np.float32)
    return jnp.asarray(e3, BF16)


def _build_bias_mask(ur_ref, bm_ref):
    W = WINDOW
    first = lax.broadcasted_iota(jnp.int32, (W, 2 * W), 1) >= W
    for h in range(ATTN_HEADS):
        band_h = LOG2E * pltpu.roll(jnp.broadcast_to(ur_ref[h:h + 1, :], (W, 2 * W)), W, 1,
                                    stride=1, stride_axis=0)
        bm_ref[1, h] = band_h
        bm_ref[0, h] = jnp.where(first, band_h, -jnp.inf)


def _attn_body(sink_ref, q_ref, kvc_ref, kvp_ref, o_ref, bm_ref, slab):
    W = WINDOW
    lane_lo2 = lax.broadcasted_iota(jnp.int32, (2 * W, LANES), 1) < ATTN_HEADDIM
    lane_lo = lax.broadcasted_iota(jnp.int32, (W, LANES), 1) < ATTN_HEADDIM

    def band(lo_col):
        return jnp.concatenate([kvp_ref[:, lo_col:lo_col + KV_COLS].astype(F32),
                                kvc_ref[:, lo_col:lo_col + KV_COLS].astype(F32)], axis=0)

    kband = band(0) * (ATTN_HEADDIM ** -0.5 * LOG2E)
    vband = band(KV_COLS)
    vsw = pltpu.roll(vband, ATTN_HEADDIM, 1)
    k_t = kband.T
    k_t_sw = pltpu.roll(k_t, ATTN_HEADDIM, 0)
    row_lo = lax.broadcasted_iota(jnp.int32, (LANES, 2 * W), 0) < ATTN_HEADDIM
    k_var = ((jnp.where(row_lo, k_t, 0.0).astype(BF16), jnp.where(row_lo, 0.0, k_t_sw).astype(BF16)),
             (jnp.where(row_lo, k_t_sw, 0.0).astype(BF16), jnp.where(row_lo, 0.0, k_t).astype(BF16)))
    ones_bd = jnp.concatenate([jnp.where(lane_lo2, 1.0, 0.0), jnp.where(lane_lo2, 0.0, 1.0)], axis=0)
    v_bd = tuple(jnp.concatenate([jnp.concatenate([jnp.where(lane_lo2, top, 0.0), jnp.where(lane_lo2, 0.0, bot)],
                                                  axis=0), ones_bd], axis=1).astype(BF16)
                 for top, bot in ((vband, vsw), (vsw, vband)))

    def kv_head(i):
        return (2 * i) // (ATTN_HEADS // ATTN_KV_HEADS)

    def logits(i):
        qp = q_ref[:, i * LANES:(i + 1) * LANES]
        return [_dot(qp, k_var[kv_head(i)][j]) for j in range(2)]

    n_pairs = ATTN_HEADS // 2
    pending = {i: logits(i) for i in range(ATTN_LOOKAHEAD)}
    for i in range(n_pairs):
        c = kv_head(i)
        if i + ATTN_LOOKAHEAD < n_pairs:
            pending[i + ATTN_LOOKAHEAD] = logits(i + ATTN_LOOKAHEAD)
        s_cur = pending.pop(i)
        ps, sink_terms = [], []
        for j in range(2):
            h = 2 * i + j
            sink = sink_ref[h] * LOG2E
            s = s_cur[j] + bm_ref[slab, h]
            m = jnp.maximum(jnp.max(s, axis=-1, keepdims=True), sink)
            ps.append(jnp.exp2(s - m).astype(BF16))
            sink_terms.append(jnp.exp2(sink - m))
        pv = _dot(jnp.concatenate(ps, axis=1), v_bd[c])
        den = pv[:, LANES:] + jnp.where(lane_lo, sink_terms[0], sink_terms[1])
        o_ref[:, i * LANES:(i + 1) * LANES] = (pv[:, :LANES] / den).astype(BF16)
        yield


def _mixer_kernel(sink_ref, conv_ref, zs_ref, dt_ref, alog_ref, dexp_ref, nw_ref, e3_ref,
                  q_ref, kvc_ref, kvp_ref, ur_ref, ys_ref, ya_ref, state_ref, bm_ref):
    @pl.when(pl.program_id(0) == 0)
    def _():
        state_ref[...] = jnp.zeros_like(state_ref)
        _build_bias_mask(ur_ref, bm_ref)

    L = SSM_CHUNK
    live = []
    for blk in range(conv_ref.shape[0] // L):
        rows = pl.ds(blk * L, L)
        live.append(_ssd_body(conv_ref.at[rows], zs_ref.at[rows], dt_ref.at[rows], alog_ref, dexp_ref, nw_ref,
                              e3_ref, ys_ref.at[rows], state_ref))
        prev = kvp_ref if blk == 0 else kvc_ref.at[pl.ds((blk - 1) * L, L)]
        slab = jnp.minimum(pl.program_id(0), 1) if blk == 0 else 1
        live.append(_attn_body(sink_ref, q_ref.at[rows], kvc_ref.at[rows], prev, ya_ref.at[rows], bm_ref, slab))
    while live:
        for body in list(live):
            if next(body, "done") == "done":
                live.remove(body)


def _mixer(xc, zs, dt, qkv, a_log, d_exp, norm_w, bias_rows, sinks, tm):
    s = xc.shape[0]
    e3 = _expansion_matrix()
    row = lambda i: (i, 0)
    const = lambda i: (0, 0)
    full = lambda a: pl.BlockSpec(a.shape, const)
    kv_blk = Q_COLS // (2 * KV_COLS)
    return pl.pallas_call(
        _mixer_kernel,
        grid=(s // tm,),
        in_specs=[pl.BlockSpec(memory_space=pltpu.SMEM),
                  pl.BlockSpec((tm, SSM_CONV_DIM), row),
                  pl.BlockSpec((tm, SSM_D_INNER), row),
                  pl.BlockSpec((tm, DT_PAD), row),
                  full(a_log), full(d_exp), full(norm_w), full(e3),
                  pl.BlockSpec((tm, Q_COLS), row),
                  pl.BlockSpec((tm, 2 * KV_COLS), lambda i: (i, kv_blk)),
                  pl.BlockSpec((WINDOW, 2 * KV_COLS),
                               lambda i: (jnp.maximum(i * (tm // WINDOW) - 1, 0), kv_blk)),
                  full(bias_rows)],
        out_specs=[pl.BlockSpec((tm, SSM_D_INNER), row), pl.BlockSpec((tm, Q_COLS), row)],
        out_shape=[jax.ShapeDtypeStruct((s, SSM_D_INNER), BF16), jax.ShapeDtypeStruct((s, Q_COLS), BF16)],
        scratch_shapes=[pltpu.VMEM((SSM_GROUPS, SSM_STATE, GROUP_WIDTH), F32),
                        pltpu.VMEM((2, ATTN_HEADS, WINDOW, 2 * WINDOW), F32)],
        compiler_params=pltpu.CompilerParams(dimension_semantics=("arbitrary",),
                                             vmem_limit_bytes=VMEM_LIMIT),
        name="mixer",
    )(sinks, xc, zs, dt, a_log, d_exp, norm_w, e3, qkv, qkv, qkv, bias_rows)


def _rel_bucket_static(n):
    max_exact = REL_BUCKETS // 2
    nf = np.maximum(n, 1).astype(np.float32)
    large = max_exact + (np.log(nf / max_exact) / math.log(REL_MAX_DIST / max_exact)
                         * (REL_BUCKETS - max_exact)).astype(np.int32)
    return np.where(n < max_exact, n, np.minimum(large, REL_BUCKETS - 1))


def _bias_rows(rel_bias):
    rel = (-np.arange(2 * WINDOW)) % (2 * WINDOW)
    idx = np.where(rel < WINDOW, _rel_bucket_static(rel), REL_BUCKETS)
    table = jnp.concatenate([rel_bias.astype(F32), jnp.full((1, ATTN_HEADS), -jnp.inf, F32)], axis=0)
    return table[idx].T


def _merge_kernel(ys_ref, ya_ref, g_ref, x_ref, wbs_ref, wba_ref, wmix_ref, lg_ref, lb_ref, h_ref):
    chunk = MXU_WIDTH
    rb = 2 * ROW_BLOCK
    for r in range(0, ys_ref.shape[0], rb):
        rows = slice(r, r + rb)
        ys, ya = ys_ref[rows, :], ya_ref[rows, :]
        merged = []
        for c in range(0, D_MODEL, chunk):
            a = _dot(ys, wbs_ref[:, c:c + chunk])
            b = _dot(ya, wba_ref[:, c:c + chunk])
            merged.append((g_ref[rows, c:c + chunk].astype(F32) * a
                           + g_ref[rows, D_MODEL + c:D_MODEL + c + chunk].astype(F32) * b).astype(BF16))
        mix = _dot(jnp.concatenate(merged, axis=1), wmix_ref[...])
        h_ref[rows, :] = _layer_norm(DEEPNORM_ALPHA * x_ref[rows, :] + mix, lg_ref[...], lb_ref[...])


def _merge(y_ssm, y_attn, gates, x2d, w_bs, w_ba, w_mix, ln_g, ln_b, tm):
    s = x2d.shape[0]
    row = lambda i: (i, 0)
    const = lambda i: (0, 0)
    full = lambda a: pl.BlockSpec(a.shape, const)
    return pl.pallas_call(
        _merge_kernel,
        grid=(s // tm,),
        in_specs=[pl.BlockSpec((tm, SSM_D_INNER), row), pl.BlockSpec((tm, Q_COLS), row),
                  pl.BlockSpec((tm, GATE_COLS), row), pl.BlockSpec((tm, D_MODEL), row),
                  full(w_bs), full(w_ba), full(w_mix), full(ln_g), full(ln_b)],
        out_specs=pl.BlockSpec((tm, D_MODEL), row),
        out_shape=jax.ShapeDtypeStruct((s, D_MODEL), F32),
        compiler_params=pltpu.CompilerParams(dimension_semantics=("arbitrary",),
                                             vmem_limit_bytes=VMEM_LIMIT),
        name="merge",
    )(y_ssm, y_attn, gates, x2d, w_bs, w_ba, w_mix, ln_g, ln_b)


def _ffn_kernel(h_ref, wup_ref, cw_ref, cb_ref, wdn_ref, lg_ref, lb_ref, o_ref, halo_ref, act_ref):
    tm = h_ref.shape[0]

    @pl.when(pl.program_id(0) == 0)
    def _():
        halo_ref[...] = jnp.zeros_like(halo_ref)

    chunk = MXU_WIDTH
    rb = 2 * ROW_BLOCK
    row_blocks = [slice(r, r + rb) for r in range(0, tm, rb)]
    hbs = [h_ref[rows, :].astype(BF16) for rows in row_blocks]

    def conv_cols(hb, cols, prev):
        u = _dot(hb, wup_ref[:, cols])
        return _causal_conv(u, prev, cw_ref, cb_ref, cols, FFN_CONV), u[rb - HALO_ROWS:, :]

    for c in range(0, D_FF, chunk):
        gcols, vcols = slice(c, c + chunk), slice(D_FF + c, D_FF + c + chunk)
        gprev, vprev = halo_ref[:, gcols], halo_ref[:, vcols]
        for hb, rows in zip(hbs, row_blocks):
            gate, gprev = conv_cols(hb, gcols, gprev)
            val, vprev = conv_cols(hb, vcols, vprev)
            act_ref[rows, c:c + chunk] = (_silu(gate) * val).astype(BF16)
        halo_ref[:, gcols] = gprev
        halo_ref[:, vcols] = vprev

    for rows in row_blocks:
        out = _dot(act_ref[rows, :], wdn_ref[...])
        o_ref[rows, :] = _layer_norm(DEEPNORM_ALPHA * h_ref[rows, :] + out, lg_ref[...], lb_ref[...])


def _ffn(h1, w_up, conv_w, conv_b, w_down, ln_g, ln_b, tm):
    s = h1.shape[0]
    row = lambda i: (i, 0)
    const = lambda i: (0, 0)
    full = lambda a: pl.BlockSpec(a.shape, const)
    return pl.pallas_call(
        _ffn_kernel,
        grid=(s // tm,),
        in_specs=[pl.BlockSpec((tm, D_MODEL), row),
                  pl.BlockSpec(w_up.shape, const, pipeline_mode=pl.Buffered(1)),
                  full(conv_w), full(conv_b),
                  pl.BlockSpec(w_down.shape, const, pipeline_mode=pl.Buffered(1)),
                  full(ln_g), full(ln_b)],
        out_specs=pl.BlockSpec((tm, D_MODEL), row),
        out_shape=jax.ShapeDtypeStruct((s, D_MODEL), F32),
        scratch_shapes=[pltpu.VMEM((HALO_ROWS, 2 * D_FF), F32),
                        pltpu.VMEM((tm, D_FF), BF16)],
        compiler_params=pltpu.CompilerParams(dimension_semantics=("arbitrary",),
                                             vmem_limit_bytes=VMEM_LIMIT),
        name="ffn",
    )(h1, w_up, conv_w, conv_b, w_down, ln_g, ln_b)


PREP_BLOCK = 512
DT_LO = SSM_D_INNER + SSM_CONV_DIM
DT_DST = PROJ_COLS - DT_PAD


def _permute_kernel(a_ref, b_ref, d_ref, o_ref):
    j = pl.program_id(0)
    first_shifted = DT_LO // PREP_BLOCK
    last = (PROJ_COLS - 1) // PREP_BLOCK

    @pl.when(j < first_shifted)
    def _():
        o_ref[...] = a_ref[...].T.astype(BF16)

    @pl.when((j >= first_shifted) & (j < last))
    def _():
        rows = jnp.concatenate([a_ref[SSM_HEADS:, :], b_ref[:SSM_HEADS, :]], axis=0)
        o_ref[...] = rows.T.astype(BF16)

    @pl.when(j == last)
    def _():
        n_tail = DT_DST - last * PREP_BLOCK
        rows = jnp.concatenate([a_ref[SSM_HEADS:SSM_HEADS + n_tail, :], d_ref[:SSM_HEADS, :],
                                jnp.zeros((DT_PAD - SSM_HEADS, D_MODEL), F32)], axis=0)
        o_ref[:, :n_tail + DT_PAD] = rows.T.astype(BF16)


def _permute_w_in(w_t):
    n_src = DT_LO + SSM_HEADS + QKV_COLS + GATE_COLS
    assert DT_LO % PREP_BLOCK == 0 and w_t.shape == (n_src, D_MODEL)
    first_shifted = DT_LO // PREP_BLOCK
    last_src = (n_src - 1) // PREP_BLOCK
    blk = lambda f: pl.BlockSpec((PREP_BLOCK, D_MODEL), f)
    return pl.pallas_call(
        _permute_kernel,
        grid=(pl.cdiv(PROJ_COLS, PREP_BLOCK),),
        in_specs=[blk(lambda j: (jnp.minimum(j, last_src), 0)),
                  blk(lambda j: (jnp.clip(j + 1, first_shifted, last_src), 0)),
                  blk(lambda j: (first_shifted, 0))],
        out_specs=pl.BlockSpec((D_MODEL, PREP_BLOCK), lambda j: (0, j)),
        out_shape=jax.ShapeDtypeStruct((D_MODEL, PROJ_COLS), BF16),
        compiler_params=pltpu.CompilerParams(dimension_semantics=("arbitrary",),
                                             vmem_limit_bytes=VMEM_LIMIT),
        name="permute_w_in",
    )(w_t, w_t, w_t)


def _row(v, width=None):
    v = v.astype(F32).reshape(1, -1)
    if width is not None and v.shape[1] < width:
        v = jnp.pad(v, ((0, 0), (0, width - v.shape[1])))
    return v


def kernel(x, rel_bias, w_in, b_gate, ssm_conv_w, ssm_conv_b, ssm_dt_bias, ssm_a_log, ssm_d, ssm_norm_w,
           attn_sinks, w_branch_ssm, w_branch_attn, w_mix_out, ln1_g, ln1_b, w_up, ffn_conv_w, ffn_conv_b,
           w_down, ln2_g, ln2_b):
    b, s, d = x.shape
    assert (b, d) == (1, D_MODEL) and s % 512 == 0 and w_in.shape[0] == DEPTH
    t = _tiles()
    h = x.reshape(s, d)
    bias_rows = _bias_rows(rel_bias)
    for l in range(DEPTH):
        zs, xc, qkv, gates, dt = _inproj(h, _permute_w_in(w_in[l].T), ssm_conv_w[l].astype(F32),
                                         _row(ssm_conv_b[l]), _row(ssm_dt_bias[l], DT_PAD),
                                         _row(b_gate[l]), t["inproj"])
        y_ssm, y_attn = _mixer(xc, zs, dt, qkv, _row(ssm_a_log[l], DT_PAD),
                               _row(jnp.repeat(ssm_d[l], SSM_HEADDIM)), _row(ssm_norm_w[l]),
                               bias_rows, attn_sinks[l].astype(F32), t["mixer"])
        h1 = _merge(y_ssm, y_attn, gates, h, w_branch_ssm[l].astype(BF16), w_branch_attn[l].astype(BF16),
                    w_mix_out[l].astype(BF16), _row(ln1_g[l]), _row(ln1_b[l]), t["merge"])
        h = _ffn(h1, w_up[l].astype(BF16), ffn_conv_w[l].astype(F32), _row(ffn_conv_b[l]),
                 w_down[l].astype(BF16), _row(ln2_g[l]), _row(ln2_b[l]), t["ffn"])
    return h.reshape(b, s, d)
```

```python
import math

import numpy as np
import jax
import jax.numpy as jnp
from jax import lax
from jax.experimental import pallas as pl
from jax.experimental.pallas import tpu as pltpu

F32 = jnp.float32
BF16 = jnp.bfloat16

D_MODEL = 1024
SSM_D_INNER = 2048
SSM_HEADDIM = 64
SSM_HEADS = 32
SSM_GROUPS = 4
SSM_HEADS_PER_GROUP = 8
SSM_STATE = 128
SSM_CONV = 4
SSM_CHUNK = 128
SSM_CONV_DIM = SSM_D_INNER + 2 * SSM_GROUPS * SSM_STATE
GROUP_WIDTH = SSM_HEADS_PER_GROUP * SSM_HEADDIM
ATTN_HEADS = 16
ATTN_KV_HEADS = 2
ATTN_HEADDIM = 64
WINDOW = 128
REL_BUCKETS = 32
REL_MAX_DIST = 128
Q_COLS = ATTN_HEADS * ATTN_HEADDIM
KV_COLS = ATTN_KV_HEADS * ATTN_HEADDIM
QKV_COLS = Q_COLS + 2 * KV_COLS
GATE_COLS = 2 * D_MODEL
D_FF = 2816
FFN_CONV = 3
DEPTH = 1
DEEPNORM_ALPHA = (2.0 * DEPTH) ** 0.25
LN_EPS = 1e-5
RMS_EPS = 1e-5

LANES = 128
SUBLANES = 8
DT_PAD = LANES
HALO_ROWS = SUBLANES
SSD_LOOKAHEAD = 1
ATTN_LOOKAHEAD = 1
MXU_WIDTH = 256
ROW_BLOCK = MXU_WIDTH // 2
VMEM_LIMIT = 56 * 1024 * 1024

PROJ_SEGMENTS = (SSM_D_INNER, SSM_CONV_DIM, QKV_COLS, GATE_COLS, DT_PAD)
PROJ_COLS = sum(PROJ_SEGMENTS)

NEG_BIG = -1e30
LOG2E = 1.4426950408889634


def _tiles():
    assert SSM_CHUNK == WINDOW
    return dict(inproj=512, mixer=2 * SSM_CHUNK, merge=512, ffn=1024)


def _dot(a, b):
    return jnp.dot(a, b, preferred_element_type=F32)


def _sigmoid(x):
    return 1.0 / (1.0 + jnp.exp2(x * (-LOG2E)))


def _silu(x):
    h = 0.5 * x
    return h + h * jnp.tanh(h)


def _layer_norm(r, g, b):
    mu = jnp.mean(r, axis=-1, keepdims=True)
    rc = r - mu
    var = jnp.mean(rc * rc, axis=-1, keepdims=True)
    return rc * lax.rsqrt(var + LN_EPS) * g + b


def _shift_rows(x, prev, j):
    row = lax.broadcasted_iota(jnp.int32, prev.shape, 0)
    sh = pltpu.roll(x, j, 0)
    top = jnp.where(row < j, pltpu.roll(prev, j, 0), sh[:HALO_ROWS])
    return jnp.concatenate([top, sh[HALO_ROWS:]], axis=0)


def _causal_conv(x, prev, w_ref, b_ref, cols, taps):
    acc = b_ref[:, cols] + w_ref[taps - 1:taps, cols] * x
    for j in range(1, taps):
        acc = acc + w_ref[taps - 1 - j:taps - j, cols] * _shift_rows(x, prev, j)
    return acc


def _inproj_kernel(x_ref, w_ref, cw_ref, cb_ref, dtb_ref, bg_ref,
                   zs_ref, xc_ref, qkv_ref, g_ref, dt_ref, halo_ref):
    tm = x_ref.shape[0]

    @pl.when(pl.program_id(0) == 0)
    def _():
        halo_ref[...] = jnp.zeros_like(halo_ref)

    chunk = MXU_WIDTH
    row_blocks = [slice(r, r + ROW_BLOCK) for r in range(0, tm, ROW_BLOCK)]
    xbs = [x_ref[rows, :].astype(BF16) for rows in row_blocks]
    offs = [sum(PROJ_SEGMENTS[:i]) for i in range(len(PROJ_SEGMENTS))]

    def proj(i, seg, c, w):
        return _dot(xbs[i], w_ref[:, offs[seg] + c:offs[seg] + c + w])

    wide_blocks = [slice(r, r + 2 * ROW_BLOCK) for r in range(0, tm, 2 * ROW_BLOCK)]
    xbw = [jnp.concatenate(xbs[2 * i:2 * i + 2], axis=0) for i in range(len(wide_blocks))]

    def proj_wide(i, seg, c, w):
        return _dot(xbw[i], w_ref[:, offs[seg] + c:offs[seg] + c + w])

    for c in range(0, SSM_D_INNER, chunk):
        for i, rows in enumerate(wide_blocks):
            zs_ref[rows, c:c + chunk] = _silu(proj_wide(i, 0, c, chunk)).astype(BF16)
    for c in range(0, SSM_CONV_DIM, chunk):
        cols = slice(c, c + chunk)
        prev = halo_ref[:, cols]
        for i, rows in enumerate(row_blocks):
            u = proj(i, 1, c, chunk)
            xc_ref[rows, cols] = _silu(_causal_conv(u, prev, cw_ref, cb_ref, cols, SSM_CONV)).astype(BF16)
            prev = u[ROW_BLOCK - HALO_ROWS:, :]
        halo_ref[:, cols] = prev
    for c in range(0, QKV_COLS, chunk):
        w = min(chunk, QKV_COLS - c)
        for i, rows in enumerate(wide_blocks):
            qkv_ref[rows, c:c + w] = proj_wide(i, 2, c, w).astype(BF16)
    for c in range(0, GATE_COLS, chunk):
        for i, rows in enumerate(wide_blocks):
            g_ref[rows, c:c + chunk] = _sigmoid(proj_wide(i, 3, c, chunk) + bg_ref[:, c:c + chunk]).astype(BF16)
    lane_ok = lax.broadcasted_iota(jnp.int32, (ROW_BLOCK, DT_PAD), 1) < SSM_HEADS
    for i, rows in enumerate(row_blocks):
        dtr = proj(i, 4, 0, DT_PAD) + dtb_ref[...]
        softplus = jnp.maximum(dtr, 0.0) + jnp.log(1.0 + jnp.exp(-jnp.abs(dtr)))
        dt_ref[rows, :] = jnp.where(lane_ok, softplus, 0.0)


def _inproj(x2d, w_perm, conv_w, conv_b, dt_bias, b_gate, tm):
    s = x2d.shape[0]
    row = lambda i: (i, 0)
    const = lambda i: (0, 0)
    full = lambda a: pl.BlockSpec(a.shape, const)
    outs = [jax.ShapeDtypeStruct((s, n), dt) for n, dt in
            zip(PROJ_SEGMENTS, (BF16, BF16, BF16, BF16, F32))]
    return pl.pallas_call(
        _inproj_kernel,
        grid=(s // tm,),
        in_specs=[pl.BlockSpec((tm, D_MODEL), row),
                  pl.BlockSpec((D_MODEL, PROJ_COLS), const, pipeline_mode=pl.Buffered(1)),
                  full(conv_w), full(conv_b), full(dt_bias), full(b_gate)],
        out_specs=[pl.BlockSpec((tm, n), row) for n in PROJ_SEGMENTS],
        out_shape=outs,
        scratch_shapes=[pltpu.VMEM((HALO_ROWS, SSM_CONV_DIM), F32)],
        compiler_params=pltpu.CompilerParams(dimension_semantics=("arbitrary",),
                                             vmem_limit_bytes=VMEM_LIMIT),
        name="inproj",
    )(x2d, w_perm, conv_w, conv_b, dt_bias, b_gate)


def _split3(x):
    hi = x.astype(BF16).astype(F32)
    r = x - hi
    mid = r.astype(BF16).astype(F32)
    lo = (r - mid).astype(BF16).astype(F32)
    return hi, mid, lo


def _pack3(x):
    hi, mid, lo = _split3(x)
    return (hi + pltpu.roll(mid, SSM_HEADS, 1) + pltpu.roll(lo, 2 * SSM_HEADS, 1)).astype(BF16)


def _ssd_body(conv_ref, zs_ref, dt_ref, alog_ref, dexp_ref, nw_ref, e3_ref, y_ref, state_ref):
    L = SSM_CHUNK
    lane_ok = lax.broadcasted_iota(jnp.int32, (L, DT_PAD), 1) < SSM_HEADS
    dt = dt_ref[...]
    a_dt = dt * (-LOG2E * jnp.exp(alog_ref[...]))

    ri = lax.broadcasted_iota(jnp.int32, (L, L), 0)
    ci = lax.broadcasted_iota(jnp.int32, (L, L), 1)
    causal = ri >= ci
    tril = jnp.where(causal, 1.0, 0.0).astype(BF16)
    hi, mid, lo = _split3(a_dt)
    a_cs = _dot(tril, hi.astype(BF16)) + _dot(tril, mid.astype(BF16)) + _dot(tril, lo.astype(BF16))

    a_last = a_cs[L - 1:L, :]
    decay_out_b = jnp.where(lane_ok, jnp.exp2(a_cs), 0.0).astype(BF16)
    w_state_b = (dt * jnp.exp2(a_last - a_cs)).astype(BF16)
    cd_e = jnp.exp2(_dot(_pack3(a_cs[L - SUBLANES:, :]), e3_ref[...])[SUBLANES - 1:, :])

    a_sub_t = (a_cs - jnp.log2(dt)).T
    lane_lo = lax.broadcasted_iota(jnp.int32, (L, LANES), 1) < SSM_HEADDIM

    def group_start(g):
        gsl = slice(g * GROUP_WIDTH, (g + 1) * GROUP_WIDTH)
        xs_g = conv_ref[:, gsl].astype(F32)
        b_off = SSM_D_INNER + g * SSM_STATE
        c_off = SSM_D_INNER + SSM_GROUPS * SSM_STATE + g * SSM_STATE
        b_t = conv_ref[:, b_off:b_off + SSM_STATE].astype(F32).T.astype(BF16)
        c_g = conv_ref[:, c_off:c_off + SSM_STATE]
        cb = _dot(c_g, b_t)
        xdtd_b = (xs_g * _dot(w_state_b, e3_ref[:, gsl])).astype(BF16)
        st = state_ref[g]
        y_off = _dot(c_g, st.astype(BF16)) * _dot(decay_out_b, e3_ref[:, gsl])
        state_ref[g] = st * cd_e[:, gsl] + _dot(b_t, xdtd_b)
        return cb, y_off + xs_g * dexp_ref[:, gsl]

    def acol_dot(p):
        return jnp.concatenate([jnp.broadcast_to(a_cs[:, 2 * p + k:2 * p + k + 1], (L, L)) for k in range(2)],
                               axis=1)

    pairs_per_group = SSM_HEADS_PER_GROUP // 2
    n_pairs = SSM_HEADS // 2
    started = {0: group_start(0)}
    acols = {p: acol_dot(p) for p in range(SSD_LOOKAHEAD)}
    pieces = []
    for p in range(n_pairs):
        g, pr = divmod(p, pairs_per_group)
        if p + SSD_LOOKAHEAD < n_pairs:
            acols[p + SSD_LOOKAHEAD] = acol_dot(p + SSD_LOOKAHEAD)
        if pr == pairs_per_group - 2 and g + 1 < SSM_GROUPS:
            started[g + 1] = group_start(g + 1)
        cb, y_rest = started[g]
        acol = acols.pop(p)
        ms = []
        for k in range(2):
            seg = acol[:, k * L:(k + 1) * L] - a_sub_t[2 * p + k:2 * p + k + 1, :]
            ms.append((cb * jnp.exp2(jnp.where(causal, seg, NEG_BIG))).astype(BF16))
        xp = conv_ref[:, p * LANES:(p + 1) * LANES]
        zero = jnp.zeros_like(xp)
        rhs = jnp.concatenate([jnp.where(lane_lo, xp, zero), jnp.where(lane_lo, zero, xp)], axis=0)
        pieces.append(_dot(jnp.concatenate(ms, axis=1), rhs))
        if pr == pairs_per_group - 1:
            gsl = slice(g * GROUP_WIDTH, (g + 1) * GROUP_WIDTH)
            yz = (jnp.concatenate(pieces, axis=1) + y_rest) * zs_ref[:, gsl].astype(F32)
            ms_ = jnp.mean(yz * yz, axis=-1, keepdims=True)
            y_ref[:, gsl] = (yz * lax.rsqrt(ms_ + RMS_EPS) * nw_ref[:, gsl]).astype(BF16)
            pieces = []
            del started[g]
        yield


def _expansion_matrix():
    k = np.arange(LANES)[:, None]
    valid = k < 3 * SSM_HEADS
    head = k % SSM_HEADS
    e3 = (valid & (head == (np.arange(SSM_D_INNER)[None, :] // SSM_HEADDIM))).astype(np.float32)
    return jnp.asarray(e3, BF16)


def _build_bias_mask(ur_ref, bm_ref):
    W = WINDOW
    first = lax.broadcasted_iota(jnp.int32, (W, 2 * W), 1) >= W
    for h in range(ATTN_HEADS):
        band_h = LOG2E * pltpu.roll(jnp.broadcast_to(ur_ref[h:h + 1, :], (W, 2 * W)), W, 1,
                                    stride=1, stride_axis=0)
        bm_ref[1, h] = band_h
        bm_ref[0, h] = jnp.where(first, band_h, -jnp.inf)


def _attn_body(sink_ref, q_ref, kvc_ref, kvp_ref, o_ref, bm_ref, slab):
    W = WINDOW
    lane_lo2 = lax.broadcasted_iota(jnp.int32, (2 * W, LANES), 1) < ATTN_HEADDIM
    lane_lo = lax.broadcasted_iota(jnp.int32, (W, LANES), 1) < ATTN_HEADDIM

    def band(lo_col):
        return jnp.concatenate([kvp_ref[:, lo_col:lo_col + KV_COLS].astype(F32),
                                kvc_ref[:, lo_col:lo_col + KV_COLS].astype(F32)], axis=0)

    kband = band(0) * (ATTN_HEADDIM ** -0.5 * LOG2E)
    vband = band(KV_COLS)
    vsw = pltpu.roll(vband, ATTN_HEADDIM, 1)
    k_t = kband.T
    k_t_sw = pltpu.roll(k_t, ATTN_HEADDIM, 0)
    row_lo = lax.broadcasted_iota(jnp.int32, (LANES, 2 * W), 0) < ATTN_HEADDIM
    k_var = ((jnp.where(row_lo, k_t, 0.0).astype(BF16), jnp.where(row_lo, 0.0, k_t_sw).astype(BF16)),
             (jnp.where(row_lo, k_t_sw, 0.0).astype(BF16), jnp.where(row_lo, 0.0, k_t).astype(BF16)))
    ones_bd = jnp.concatenate([jnp.where(lane_lo2, 1.0, 0.0), jnp.where(lane_lo2, 0.0, 1.0)], axis=0)
    v_bd = tuple(jnp.concatenate([jnp.concatenate([jnp.where(lane_lo2, top, 0.0), jnp.where(lane_lo2, 0.0, bot)],
                                                  axis=0), ones_bd], axis=1).astype(BF16)
                 for top, bot in ((vband, vsw), (vsw, vband)))

    def kv_head(i):
        return (2 * i) // (ATTN_HEADS // ATTN_KV_HEADS)

    def logits(i):
        qp = q_ref[:, i * LANES:(i + 1) * LANES]
        return [_dot(qp, k_var[kv_head(i)][j]) for j in range(2)]

    n_pairs = ATTN_HEADS // 2
    pending = {i: logits(i) for i in range(ATTN_LOOKAHEAD)}
    for i in range(n_pairs):
        c = kv_head(i)
        if i + ATTN_LOOKAHEAD < n_pairs:
            pending[i + ATTN_LOOKAHEAD] = logits(i + ATTN_LOOKAHEAD)
        s_cur = pending.pop(i)
        ps, sink_terms = [], []
        for j in range(2):
            h = 2 * i + j
            sink = sink_ref[h] * LOG2E
            s = s_cur[j] + bm_ref[slab, h]
            m = jnp.maximum(jnp.max(s, axis=-1, keepdims=True), sink)
            ps.append(jnp.exp2(s - m).astype(BF16))
            sink_terms.append(jnp.exp2(sink - m))
            if j == 0:
                yield
        pv =_dot(jnp.concatenate(ps, axis=1), v_bd[c])
        den = pv[:, LANES:] + jnp.where(lane_lo, sink_terms[0], sink_terms[1])
        o_ref[:, i * LANES:(i + 1) * LANES] = (pv[:, :LANES] / den).astype(BF16)
        yield


def _mixer_kernel(sink_ref, conv_ref, zs_ref, dt_ref, alog_ref, dexp_ref, nw_ref, e3_ref,
                  q_ref, kvc_ref, kvp_ref, ur_ref, ys_ref, ya_ref, state_ref, bm_ref):
    @pl.when(pl.program_id(0) == 0)
    def _():
        state_ref[...] = jnp.zeros_like(state_ref)
        _build_bias_mask(ur_ref, bm_ref)

    L = SSM_CHUNK
    live = []
    for blk in range(conv_ref.shape[0] // L):
        rows = pl.ds(blk * L, L)
        live.append(_ssd_body(conv_ref.at[rows], zs_ref.at[rows], dt_ref.at[rows], alog_ref, dexp_ref, nw_ref,
                              e3_ref, ys_ref.at[rows], state_ref))
        prev = kvp_ref if blk == 0 else kvc_ref.at[pl.ds((blk - 1) * L, L)]
        slab = jnp.minimum(pl.program_id(0), 1) if blk == 0 else 1
        live.append(_attn_body(sink_ref, q_ref.at[rows], kvc_ref.at[rows], prev, ya_ref.at[rows], bm_ref, slab))
    while live:
        for body in list(live):
            if next(body, "done") == "done":
                live.remove(body)


def _mixer(xc, zs, dt, qkv, a_log, d_exp, norm_w, bias_rows, sinks, tm):
    s = xc.shape[0]
    e3 = _expansion_matrix()
    row = lambda i: (i, 0)
    const = lambda i: (0, 0)
    full = lambda a: pl.BlockSpec(a.shape, const)
    kv_blk = Q_COLS // (2 * KV_COLS)
    return pl.pallas_call(
        _mixer_kernel,
        grid=(s // tm,),
        in_specs=[pl.BlockSpec(memory_space=pltpu.SMEM),
                  pl.BlockSpec((tm, SSM_CONV_DIM), row),
                  pl.BlockSpec((tm, SSM_D_INNER), row),
                  pl.BlockSpec((tm, DT_PAD), row),
                  full(a_log), full(d_exp), full(norm_w), full(e3),
                  pl.BlockSpec((tm, Q_COLS), row),
                  pl.BlockSpec((tm, 2 * KV_COLS), lambda i: (i, kv_blk)),
                  pl.BlockSpec((WINDOW, 2 * KV_COLS),
                               lambda i: (jnp.maximum(i * (tm // WINDOW) - 1, 0), kv_blk)),
                  full(bias_rows)],
        out_specs=[pl.BlockSpec((tm, SSM_D_INNER), row), pl.BlockSpec((tm, Q_COLS), row)],
        out_shape=[jax.ShapeDtypeStruct((s, SSM_D_INNER), BF16), jax.ShapeDtypeStruct((s, Q_COLS), BF16)],
        scratch_shapes=[pltpu.VMEM((SSM_GROUPS, SSM_STATE, GROUP_WIDTH), F32),
                        pltpu.VMEM((2, ATTN_HEADS, WINDOW, 2 * WINDOW), F32)],
        compiler_params=pltpu.CompilerParams(dimension_semantics=("arbitrary",),
                                             vmem_limit_bytes=VMEM_LIMIT),
        name="mixer",
    )(sinks, xc, zs, dt, a_log, d_exp, norm_w, e3, qkv, qkv, qkv, bias_rows)


def _rel_bucket_static(n):
    max_exact = REL_BUCKETS // 2
    nf = np.maximum(n, 1).astype(np.float32)
    large = max_exact + (np.log(nf / max_exact) / math.log(REL_MAX_DIST / max_exact)
                         * (REL_BUCKETS - max_exact)).astype(np.int32)
    return np.where(n < max_exact, n, np.minimum(large, REL_BUCKETS - 1))


def _bias_rows(rel_bias):
    rel = (-np.arange(2 * WINDOW)) % (2 * WINDOW)
    idx = np.where(rel < WINDOW, _rel_bucket_static(rel), REL_BUCKETS)
    table = jnp.concatenate([rel_bias.astype(F32), jnp.full((1, ATTN_HEADS), -jnp.inf, F32)], axis=0)
    return table[idx].T


def _merge_kernel(ys_ref, ya_ref, g_ref, x_ref, wbs_ref, wba_ref, wmix_ref, lg_ref, lb_ref, h_ref):
    chunk = MXU_WIDTH
    rb = 2 * ROW_BLOCK
    for r in range(0, ys_ref.shape[0], rb):
        rows = slice(r, r + rb)
        ys, ya = ys_ref[rows, :], ya_ref[rows, :]
        merged = []
        for c in range(0, D_MODEL, chunk):
            a = _dot(ys, wbs_ref[:, c:c + chunk])
            b = _dot(ya, wba_ref[:, c:c + chunk])
            merged.append((g_ref[rows, c:c + chunk].astype(F32) * a
                           + g_ref[rows, D_MODEL + c:D_MODEL + c + chunk].astype(F32) * b).astype(BF16))
        mix = _dot(jnp.concatenate(merged, axis=1), wmix_ref[...])
        h_ref[rows, :] = _layer_norm(DEEPNORM_ALPHA * x_ref[rows, :] + mix, lg_ref[...], lb_ref[...])


def _merge(y_ssm, y_attn, gates, x2d, w_bs, w_ba, w_mix, ln_g, ln_b, tm):
    s = x2d.shape[0]
    row = lambda i: (i, 0)
    const = lambda i: (0, 0)
    full = lambda a: pl.BlockSpec(a.shape, const)
    return pl.pallas_call(
        _merge_kernel,
        grid=(s // tm,),
        in_specs=[pl.BlockSpec((tm, SSM_D_INNER), row), pl.BlockSpec((tm, Q_COLS), row),
                  pl.BlockSpec((tm, GATE_COLS), row), pl.BlockSpec((tm, D_MODEL), row),
                  full(w_bs), full(w_ba), full(w_mix), full(ln_g), full(ln_b)],
        out_specs=pl.BlockSpec((tm, D_MODEL), row),
        out_shape=jax.ShapeDtypeStruct((s, D_MODEL), F32),
        compiler_params=pltpu.CompilerParams(dimension_semantics=("arbitrary",),
                                             vmem_limit_bytes=VMEM_LIMIT),
        name="merge",
    )(y_ssm, y_attn, gates, x2d, w_bs, w_ba, w_mix, ln_g, ln_b)


def _ffn_kernel(h_ref, wup_ref, cw_ref, cb_ref, wdn_ref, lg_ref, lb_ref, o_ref, halo_ref, act_ref):
    tm = h_ref.shape[0]

    @pl.when(pl.program_id(0) == 0)
    def _():
        halo_ref[...] = jnp.zeros_like(halo_ref)

    chunk = MXU_WIDTH
    rb = 2 * ROW_BLOCK
    row_blocks = [slice(r, r + rb) for r in range(0, tm, rb)]
    hbs = [h_ref[rows, :].astype(BF16) for rows in row_blocks]

    def conv_cols(hb, cols, prev):
        u = _dot(hb, wup_ref[:, cols])
        return _causal_conv(u, prev, cw_ref, cb_ref, cols, FFN_CONV), u[rb - HALO_ROWS:, :]

    for c in range(0, D_FF, chunk):
        gcols, vcols = slice(c, c + chunk), slice(D_FF + c, D_FF + c + chunk)
        gprev, vprev = halo_ref[:, gcols], halo_ref[:, vcols]
        for hb, rows in zip(hbs, row_blocks):
            gate, gprev = conv_cols(hb, gcols, gprev)
            val, vprev = conv_cols(hb, vcols, vprev)
            act_ref[rows, c:c + chunk] = (_silu(gate) * val).astype(BF16)
        halo_ref[:, gcols] = gprev
        halo_ref[:, vcols] = vprev

    for rows in row_blocks:
        out = _dot(act_ref[rows, :], wdn_ref[...])
        o_ref[rows, :] = _layer_norm(DEEPNORM_ALPHA * h_ref[rows, :] + out, lg_ref[...], lb_ref[...])


def _ffn(h1, w_up, conv_w, conv_b, w_down, ln_g, ln_b, tm):
    s = h1.shape[0]
    row = lambda i: (i, 0)
    const = lambda i: (0, 0)
    full = lambda a: pl.BlockSpec(a.shape, const)
    return pl.pallas_call(
        _ffn_kernel,
        grid=(s // tm,),
        in_specs=[pl.BlockSpec((tm, D_MODEL), row),
                  pl.BlockSpec(w_up.shape, const, pipeline_mode=pl.Buffered(1)),
                  full(conv_w), full(conv_b),
                  pl.BlockSpec(w_down.shape, const, pipeline_mode=pl.Buffered(1)),
                  full(ln_g), full(ln_b)],
        out_specs=pl.BlockSpec((tm, D_MODEL), row),
        out_shape=jax.ShapeDtypeStruct((s, D_MODEL), F32),
        scratch_shapes=[pltpu.VMEM((HALO_ROWS, 2 * D_FF), F32),
                        pltpu.VMEM((tm, D_FF), BF16)],
        compiler_params=pltpu.CompilerParams(dimension_semantics=("arbitrary",),
                                             vmem_limit_bytes=VMEM_LIMIT),
        name="ffn",
    )(h1, w_up, conv_w, conv_b, w_down, ln_g, ln_b)


PREP_BLOCK = 512
DT_LO = SSM_D_INNER + SSM_CONV_DIM
DT_DST = PROJ_COLS - DT_PAD


def _permute_kernel(a_ref, b_ref, d_ref, o_ref):
    j = pl.program_id(0)
    first_shifted = DT_LO // PREP_BLOCK
    last = (PROJ_COLS - 1) // PREP_BLOCK

    @pl.when(j < first_shifted)
    def _():
        o_ref[...] = a_ref[...].T.astype(BF16)

    @pl.when((j >= first_shifted) & (j < last))
    def _():
        rows = jnp.concatenate([a_ref[SSM_HEADS:, :], b_ref[:SSM_HEADS, :]], axis=0)
        o_ref[...] = rows.T.astype(BF16)

    @pl.when(j == last)
    def _():
        n_tail = DT_DST - last * PREP_BLOCK
        rows = jnp.concatenate([a_ref[SSM_HEADS:SSM_HEADS + n_tail, :], d_ref[:SSM_HEADS, :],
                                jnp.zeros((DT_PAD - SSM_HEADS, D_MODEL), F32)], axis=0)
        o_ref[:, :n_tail + DT_PAD] = rows.T.astype(BF16)


def _permute_w_in(w_t):
    n_src = DT_LO + SSM_HEADS + QKV_COLS + GATE_COLS
    assert DT_LO % PREP_BLOCK == 0 and w_t.shape == (n_src, D_MODEL)
    first_shifted = DT_LO // PREP_BLOCK
    last_src = (n_src - 1) // PREP_BLOCK
    blk = lambda f: pl.BlockSpec((PREP_BLOCK, D_MODEL), f)
    return pl.pallas_call(
        _permute_kernel,
        grid=(pl.cdiv(PROJ_COLS, PREP_BLOCK),),
        in_specs=[blk(lambda j: (jnp.minimum(j, last_src), 0)),
                  blk(lambda j: (jnp.clip(j + 1, first_shifted, last_src), 0)),
                  blk(lambda j: (first_shifted, 0))],
        out_specs=pl.BlockSpec((D_MODEL, PREP_BLOCK), lambda j: (0, j)),
        out_shape=jax.ShapeDtypeStruct((D_MODEL, PROJ_COLS), BF16),
        compiler_params=pltpu.CompilerParams(dimension_semantics=("arbitrary",),
                                             vmem_limit_bytes=VMEM_LIMIT),
        name="permute_w_in",
    )(w_t, w_t, w_t)


def _row(v, width=None):
    v = v.astype(F32).reshape(1, -1)
    if width is not None and v.shape[1] < width:
        v = jnp.pad(v, ((0, 0), (0, width - v.shape[1])))
    return v


def kernel(x, rel_bias, w_in, b_gate, ssm_conv_w, ssm_conv_b, ssm_dt_bias, ssm_a_log, ssm_d, ssm_norm_w,
           attn_sinks, w_branch_ssm, w_branch_attn, w_mix_out, ln1_g, ln1_b, w_up, ffn_conv_w, ffn_conv_b,
           w_down, ln2_g, ln2_b):
    b, s, d = x.shape
    assert (b, d) == (1, D_MODEL) and s % 512 == 0 and w_in.shape[0] == DEPTH
    t = _tiles()
    h = x.reshape(s, d)
    bias_rows = _bias_rows(rel_bias)
    for l in range(DEPTH):
        zs, xc, qkv, gates, dt = _inproj(h, _permute_w_in(w_in[l].T), ssm_conv_w[l].astype(F32),
                                         _row(ssm_conv_b[l]), _row(ssm_dt_bias[l], DT_PAD),
                                         _row(b_gate[l]), t["inproj"])
        y_ssm, y_attn = _mixer(xc, zs, dt, qkv, _row(ssm_a_log[l], DT_PAD),
                               _row(jnp.repeat(ssm_d[l], SSM_HEADDIM)), _row(ssm_norm_w[l]),
                               bias_rows, attn_sinks[l].astype(F32), t["mixer"])
        h1 = _merge(y_ssm, y_attn, gates, h, w_branch_ssm[l].astype(BF16), w_branch_attn[l].astype(BF16),
                    w_mix_out[l].astype(BF16), _row(ln1_g[l]), _row(ln1_b[l]), t["merge"])
        h = _ffn(h1, w_up[l].astype(BF16), ffn_conv_w[l].astype(F32), _row(ffn_conv_b[l]),
                 w_down[l].astype(BF16), _row(ln2_g[l]), _row(ln2_b[l]), t["ffn"])
    return h.reshape(b, s, d)
```

```python
import math

import numpy as np
import jax
import jax.numpy as jnp
from jax import lax
from jax.experimental import pallas as pl
from jax.experimental.pallas import tpu as pltpu

F32 = jnp.float32
BF16 = jnp.bfloat16

D_MODEL = 1024
SSM_D_INNER = 2048
SSM_HEADDIM = 64
SSM_HEADS = 32
SSM_GROUPS = 4
SSM_HEADS_PER_GROUP = 8
SSM_STATE = 128
SSM_CONV = 4
SSM_CHUNK = 128
SSM_CONV_DIM = SSM_D_INNER + 2 * SSM_GROUPS * SSM_STATE
GROUP_WIDTH = SSM_HEADS_PER_GROUP * SSM_HEADDIM
ATTN_HEADS = 16
ATTN_KV_HEADS = 2
ATTN_HEADDIM = 64
WINDOW = 128
REL_BUCKETS = 32
REL_MAX_DIST = 128
Q_COLS = ATTN_HEADS * ATTN_HEADDIM
KV_COLS = ATTN_KV_HEADS * ATTN_HEADDIM
QKV_COLS = Q_COLS + 2 * KV_COLS
GATE_COLS = 2 * D_MODEL
D_FF = 2816
FFN_CONV = 3
DEPTH = 1
DEEPNORM_ALPHA = (2.0 * DEPTH) ** 0.25
LN_EPS = 1e-5
RMS_EPS = 1e-5

LANES = 128
SUBLANES = 8
DT_PAD = LANES
HALO_ROWS = SUBLANES
SSD_LOOKAHEAD = 1
ATTN_LOOKAHEAD = 1
MXU_WIDTH = 256
ROW_BLOCK = MXU_WIDTH // 2
VMEM_LIMIT = 56 * 1024 * 1024

PROJ_SEGMENTS = (SSM_D_INNER, SSM_CONV_DIM, QKV_COLS, GATE_COLS, DT_PAD)
PROJ_COLS = sum(PROJ_SEGMENTS)

NEG_BIG = -1e30
LOG2E = 1.4426950408889634


def _tiles():
    assert SSM_CHUNK == WINDOW
    return dict(inproj=512, mixer=2 * SSM_CHUNK, merge=512, ffn=1024)


def _dot(a, b):
    return jnp.dot(a, b, preferred_element_type=F32)


def _sigmoid(x):
    return 1.0 / (1.0 + jnp.exp2(x * (-LOG2E)))


def _silu(x):
    h = 0.5 * x
    return h + h * jnp.tanh(h)


def _layer_norm(r, g, b):
    mu = jnp.mean(r, axis=-1, keepdims=True)
    rc = r - mu
    var = jnp.mean(rc * rc, axis=-1, keepdims=True)
    return rc * lax.rsqrt(var + LN_EPS) * g + b


def _shift_rows(x, prev, j):
    row = lax.broadcasted_iota(jnp.int32, prev.shape, 0)
    sh = pltpu.roll(x, j, 0)
    top = jnp.where(row < j, pltpu.roll(prev, j, 0), sh[:HALO_ROWS])
    return jnp.concatenate([top, sh[HALO_ROWS:]], axis=0)


def _causal_conv(x, prev, w_ref, b_ref, cols, taps):
    acc = b_ref[:, cols] + w_ref[taps - 1:taps, cols] * x
    for j in range(1, taps):
        acc = acc + w_ref[taps - 1 - j:taps - j, cols] * _shift_rows(x, prev, j)
    return acc


def _inproj_kernel(x_ref, w_ref, cw_ref, cb_ref, dtb_ref, bg_ref,
                   zs_ref, xc_ref, qkv_ref, g_ref, dt_ref, halo_ref):
    tm = x_ref.shape[0]

    @pl.when(pl.program_id(0) == 0)
    def _():
        halo_ref[...] = jnp.zeros_like(halo_ref)

    chunk = MXU_WIDTH
    row_blocks = [slice(r, r + ROW_BLOCK) for r in range(0, tm, ROW_BLOCK)]
    xbs = [x_ref[rows, :].astype(BF16) for rows in row_blocks]
    offs = [sum(PROJ_SEGMENTS[:i]) for i in range(len(PROJ_SEGMENTS))]

    def proj(i, seg, c, w):
        return _dot(xbs[i], w_ref[:, offs[seg] + c:offs[seg] + c + w])

    wide_blocks = [slice(r, r + 2 * ROW_BLOCK) for r in range(0, tm, 2 * ROW_BLOCK)]
    xbw = [jnp.concatenate(xbs[2 * i:2 * i + 2], axis=0) for i in range(len(wide_blocks))]

    def proj_wide(i, seg, c, w):
        return _dot(xbw[i], w_ref[:, offs[seg] + c:offs[seg] + c + w])

    for c in range(0, SSM_D_INNER, chunk):
        for i, rows in enumerate(wide_blocks):
            zs_ref[rows, c:c + chunk] = _silu(proj_wide(i, 0, c, chunk)).astype(BF16)
    for c in range(0, SSM_CONV_DIM, chunk):
        cols = slice(c, c + chunk)
        prev = halo_ref[:, cols]
        for i, rows in enumerate(row_blocks):
            u = proj(i, 1, c, chunk)
            xc_ref[rows, cols] = _silu(_causal_conv(u, prev, cw_ref, cb_ref, cols, SSM_CONV)).astype(BF16)
            prev = u[ROW_BLOCK - HALO_ROWS:, :]
        halo_ref[:, cols] = prev
    for c in range(0, QKV_COLS, chunk):
        w = min(chunk, QKV_COLS - c)
        for i, rows in enumerate(wide_blocks):
            qkv_ref[rows, c:c + w] = proj_wide(i, 2, c, w).astype(BF16)
    for c in range(0, GATE_COLS, chunk):
        for i, rows in enumerate(wide_blocks):
            g_ref[rows, c:c + chunk] = _sigmoid(proj_wide(i, 3, c, chunk) + bg_ref[:, c:c + chunk]).astype(BF16)
    lane_ok = lax.broadcasted_iota(jnp.int32, (ROW_BLOCK, DT_PAD), 1) < SSM_HEADS
    for i, rows in enumerate(row_blocks):
        dtr = proj(i, 4, 0, DT_PAD) + dtb_ref[...]
        softplus = jnp.maximum(dtr, 0.0) + jnp.log(1.0 + jnp.exp(-jnp.abs(dtr)))
        dt_ref[rows, :] = jnp.where(lane_ok, softplus, 0.0)


def _inproj(x2d, w_perm, conv_w, conv_b, dt_bias, b_gate, tm):
    s = x2d.shape[0]
    row = lambda i: (i, 0)
    const = lambda i: (0, 0)
    full = lambda a: pl.BlockSpec(a.shape, const)
    outs = [jax.ShapeDtypeStruct((s, n), dt) for n, dt in
            zip(PROJ_SEGMENTS, (BF16, BF16, BF16, BF16, F32))]
    return pl.pallas_call(
        _inproj_kernel,
        grid=(s // tm,),
        in_specs=[pl.BlockSpec((tm, D_MODEL), row),
                  pl.BlockSpec((D_MODEL, PROJ_COLS), const, pipeline_mode=pl.Buffered(1)),
                  full(conv_w), full(conv_b), full(dt_bias), full(b_gate)],
        out_specs=[pl.BlockSpec((tm, n), row) for n in PROJ_SEGMENTS],
        out_shape=outs,
        scratch_shapes=[pltpu.VMEM((HALO_ROWS, SSM_CONV_DIM), F32)],
        compiler_params=pltpu.CompilerParams(dimension_semantics=("arbitrary",),
                                             vmem_limit_bytes=VMEM_LIMIT),
        name="inproj",
    )(x2d, w_perm, conv_w, conv_b, dt_bias, b_gate)


def _split3(x):
    hi = x.astype(BF16).astype(F32)
    r = x - hi
    mid = r.astype(BF16).astype(F32)
    lo = (r - mid).astype(BF16).astype(F32)
    return hi, mid, lo


def _pack3(x):
    hi, mid, lo = _split3(x)
    return (hi + pltpu.roll(mid, SSM_HEADS, 1) + pltpu.roll(lo, 2 * SSM_HEADS, 1)).astype(BF16)


def _ssd_body(conv_ref, zs_ref, dt_ref, alog_ref, dexp_ref, nw_ref, e3_ref, y_ref, state_ref):
    L = SSM_CHUNK
    lane_ok = lax.broadcasted_iota(jnp.int32, (L, DT_PAD), 1) < SSM_HEADS
    dt = dt_ref[...]
    a_dt = dt * (-LOG2E * jnp.exp(alog_ref[...]))

    ri = lax.broadcasted_iota(jnp.int32, (L, L), 0)
    ci = lax.broadcasted_iota(jnp.int32, (L, L), 1)
    causal = ri >= ci
    tril = jnp.where(causal, 1.0, 0.0).astype(BF16)
    hi, mid, lo = _split3(a_dt)
    a_cs = _dot(tril, hi.astype(BF16)) + _dot(tril, mid.astype(BF16)) + _dot(tril, lo.astype(BF16))

    a_last = a_cs[L - 1:L, :]
    decay_out_b = jnp.where(lane_ok, jnp.exp2(a_cs), 0.0).astype(BF16)
    w_state_b = (dt * jnp.exp2(a_last - a_cs)).astype(BF16)
    cd_e = jnp.exp2(_dot(_pack3(a_cs[L - SUBLANES:, :]), e3_ref[...])[SUBLANES - 1:, :])

    a_sub_t = (a_cs - jnp.log2(dt)).T
    lane_lo = lax.broadcasted_iota(jnp.int32, (L, LANES), 1) < SSM_HEADDIM

    def group_start(g):
        gsl = slice(g * GROUP_WIDTH, (g + 1) * GROUP_WIDTH)
        xs_g = conv_ref[:, gsl].astype(F32)
        b_off = SSM_D_INNER + g * SSM_STATE
        c_off = SSM_D_INNER + SSM_GROUPS * SSM_STATE + g * SSM_STATE
        b_t = conv_ref[:, b_off:b_off + SSM_STATE].astype(F32).T.astype(BF16)
        c_g = conv_ref[:, c_off:c_off + SSM_STATE]
        cb = _dot(c_g, b_t)
        xdtd_b = (xs_g * _dot(w_state_b, e3_ref[:, gsl])).astype(BF16)
        st = state_ref[g]
        y_off = _dot(c_g, st.astype(BF16)) * _dot(decay_out_b, e3_ref[:, gsl])
        state_ref[g] = st * cd_e[:, gsl] + _dot(b_t, xdtd_b)
        return cb, y_off + xs_g * dexp_ref[:, gsl]

    def acol_dot(p):
        return jnp.concatenate([jnp.broadcast_to(a_cs[:, 2 * p + k:2 * p + k + 1], (L, L)) for k in range(2)],
                               axis=1)

    pairs_per_group = SSM_HEADS_PER_GROUP // 2
    n_pairs = SSM_HEADS // 2
    started = {0: group_start(0)}
    acols = {p: acol_dot(p) for p in range(SSD_LOOKAHEAD)}
    pieces = []
    for p in range(n_pairs):
        g, pr = divmod(p, pairs_per_group)
        if p + SSD_LOOKAHEAD < n_pairs:
            acols[p + SSD_LOOKAHEAD] = acol_dot(p + SSD_LOOKAHEAD)
        if pr == pairs_per_group - 2 and g + 1 < SSM_GROUPS:
            started[g + 1] = group_start(g + 1)
        cb, y_rest = started[g]
        acol = acols.pop(p)
        ms = []
        for k in range(2):
            seg = acol[:, k * L:(k + 1) * L] - a_sub_t[2 * p + k:2 * p + k + 1, :]
            ms.append((cb * jnp.exp2(jnp.where(causal, seg, NEG_BIG))).astype(BF16))
            if k == 0:
                yield
        xp =conv_ref[:, p * LANES:(p + 1) * LANES]
        zero = jnp.zeros_like(xp)
        rhs = jnp.concatenate([jnp.where(lane_lo, xp, zero), jnp.where(lane_lo, zero, xp)], axis=0)
        pieces.append(_dot(jnp.concatenate(ms, axis=1), rhs))
        if pr == pairs_per_group - 1:
            gsl = slice(g * GROUP_WIDTH, (g + 1) * GROUP_WIDTH)
            yz = (jnp.concatenate(pieces, axis=1) + y_rest) * zs_ref[:, gsl].astype(F32)
            ms_ = jnp.mean(yz * yz, axis=-1, keepdims=True)
            y_ref[:, gsl] = (yz * lax.rsqrt(ms_ + RMS_EPS) * nw_ref[:, gsl]).astype(BF16)
            pieces = []
            del started[g]
        yield


def _expansion_matrix():
    k = np.arange(LANES)[:, None]
    valid = k < 3 * SSM_HEADS
    head = k % SSM_HEADS
    e3 = (valid & (head == (np.arange(SSM_D_INNER)[None, :] // SSM_HEADDIM))).astype(np.float32)
    return jnp.asarray(e3, BF16)


def _build_bias_mask(ur_ref, bm_ref):
    W = WINDOW
    first = lax.broadcasted_iota(jnp.int32, (W, 2 * W), 1) >= W
    for h in range(ATTN_HEADS):
        band_h = LOG2E * pltpu.roll(jnp.broadcast_to(ur_ref[h:h + 1, :], (W, 2 * W)), W, 1,
                                    stride=1, stride_axis=0)
        bm_ref[1, h] = band_h
        bm_ref[0, h] = jnp.where(first, band_h, -jnp.inf)


def _attn_body(sink_ref, q_ref, kvc_ref, kvp_ref, o_ref, bm_ref, slab):
    W = WINDOW
    lane_lo2 = lax.broadcasted_iota(jnp.int32, (2 * W, LANES), 1) < ATTN_HEADDIM
    lane_lo = lax.broadcasted_iota(jnp.int32, (W, LANES), 1) < ATTN_HEADDIM

    def band(lo_col):
        return jnp.concatenate([kvp_ref[:, lo_col:lo_col + KV_COLS].astype(F32),
                                kvc_ref[:, lo_col:lo_col + KV_COLS].astype(F32)], axis=0)

    kband = band(0) * (ATTN_HEADDIM ** -0.5 * LOG2E)
    vband = band(KV_COLS)
    vsw = pltpu.roll(vband, ATTN_HEADDIM, 1)
    k_t = kband.T
    k_t_sw = pltpu.roll(k_t, ATTN_HEADDIM, 0)
    row_lo = lax.broadcasted_iota(jnp.int32, (LANES, 2 * W), 0) < ATTN_HEADDIM
    k_var = ((jnp.where(row_lo, k_t, 0.0).astype(BF16), jnp.where(row_lo, 0.0, k_t_sw).astype(BF16)),
             (jnp.where(row_lo, k_t_sw, 0.0).astype(BF16), jnp.where(row_lo, 0.0, k_t).astype(BF16)))
    ones_bd = jnp.concatenate([jnp.where(lane_lo2, 1.0, 0.0), jnp.where(lane_lo2, 0.0, 1.0)], axis=0)
    v_bd = tuple(jnp.concatenate([jnp.concatenate([jnp.where(lane_lo2, top, 0.0), jnp.where(lane_lo2, 0.0, bot)],
                                                  axis=0), ones_bd], axis=1).astype(BF16)
                 for top, bot in ((vband, vsw), (vsw, vband)))

    def kv_head(i):
        return (2 * i) // (ATTN_HEADS // ATTN_KV_HEADS)

    def logits(i):
        qp = q_ref[:, i * LANES:(i + 1) * LANES]
        return [_dot(qp, k_var[kv_head(i)][j]) for j in range(2)]

    n_pairs = ATTN_HEADS // 2
    pending = {i: logits(i) for i in range(ATTN_LOOKAHEAD)}
    for i in range(n_pairs):
        c = kv_head(i)
        if i + ATTN_LOOKAHEAD < n_pairs:
            pending[i + ATTN_LOOKAHEAD] = logits(i + ATTN_LOOKAHEAD)
        s_cur = pending.pop(i)
        ps, sink_terms = [], []
        for j in range(2):
            h = 2 * i + j
            sink = sink_ref[h] * LOG2E
            s = s_cur[j] + bm_ref[slab, h]
            m = jnp.maximum(jnp.max(s, axis=-1, keepdims=True), sink)
            ps.append(jnp.exp2(s - m).astype(BF16))
            sink_terms.append(jnp.exp2(sink - m))
            if j == 0:
                yield
        pv =_dot(jnp.concatenate(ps, axis=1), v_bd[c])
        den = pv[:, LANES:] + jnp.where(lane_lo, sink_terms[0], sink_terms[1])
        o_ref[:, i * LANES:(i + 1) * LANES] = (pv[:, :LANES] / den).astype(BF16)
        yield


def _mixer_kernel(sink_ref, conv_ref, zs_ref, dt_ref, alog_ref, dexp_ref, nw_ref, e3_ref,
                  q_ref, kvc_ref, kvp_ref, ur_ref, ys_ref, ya_ref, state_ref, bm_ref):
    @pl.when(pl.program_id(0) == 0)
    def _():
        state_ref[...] = jnp.zeros_like(state_ref)
        _build_bias_mask(ur_ref, bm_ref)

    L = SSM_CHUNK
    live = []
    for blk in range(conv_ref.shape[0] // L):
        rows = pl.ds(blk * L, L)
        live.append(_ssd_body(conv_ref.at[rows], zs_ref.at[rows], dt_ref.at[rows], alog_ref, dexp_ref, nw_ref,
                              e3_ref, ys_ref.at[rows], state_ref))
        prev = kvp_ref if blk == 0 else kvc_ref.at[pl.ds((blk - 1) * L, L)]
        slab = jnp.minimum(pl.program_id(0), 1) if blk == 0 else 1
        live.append(_attn_body(sink_ref, q_ref.at[rows], kvc_ref.at[rows], prev, ya_ref.at[rows], bm_ref, slab))
    while live:
        for body in list(live):
            if next(body, "done") == "done":
                live.remove(body)


def _mixer(xc, zs, dt, qkv, a_log, d_exp, norm_w, bias_rows, sinks, tm):
    s = xc.shape[0]
    e3 = _expansion_matrix()
    row = lambda i: (i, 0)
    const = lambda i: (0, 0)
    full = lambda a: pl.BlockSpec(a.shape, const)
    kv_blk = Q_COLS // (2 * KV_COLS)
    return pl.pallas_call(
        _mixer_kernel,
        grid=(s // tm,),
        in_specs=[pl.BlockSpec(memory_space=pltpu.SMEM),
                  pl.BlockSpec((tm, SSM_CONV_DIM), row),
                  pl.BlockSpec((tm, SSM_D_INNER), row),
                  pl.BlockSpec((tm, DT_PAD), row),
                  full(a_log), full(d_exp), full(norm_w), full(e3),
                  pl.BlockSpec((tm, Q_COLS), row),
                  pl.BlockSpec((tm, 2 * KV_COLS), lambda i: (i, kv_blk)),
                  pl.BlockSpec((WINDOW, 2 * KV_COLS),
                               lambda i: (jnp.maximum(i * (tm // WINDOW) - 1, 0), kv_blk)),
                  full(bias_rows)],
        out_specs=[pl.BlockSpec((tm, SSM_D_INNER), row), pl.BlockSpec((tm, Q_COLS), row)],
        out_shape=[jax.ShapeDtypeStruct((s, SSM_D_INNER), BF16), jax.ShapeDtypeStruct((s, Q_COLS), BF16)],
        scratch_shapes=[pltpu.VMEM((SSM_GROUPS, SSM_STATE, GROUP_WIDTH), F32),
                        pltpu.VMEM((2, ATTN_HEADS, WINDOW, 2 * WINDOW), F32)],
        compiler_params=pltpu.CompilerParams(dimension_semantics=("arbitrary",),
                                             vmem_limit_bytes=VMEM_LIMIT),
        name="mixer",
    )(sinks, xc, zs, dt, a_log, d_exp, norm_w, e3, qkv, qkv, qkv, bias_rows)


def _rel_bucket_static(n):
    max_exact = REL_BUCKETS // 2
    nf = np.maximum(n, 1).astype(np.float32)
    large = max_exact + (np.log(nf / max_exact) / math.log(REL_MAX_DIST / max_exact)
                         * (REL_BUCKETS - max_exact)).astype(np.int32)
    return np.where(n < max_exact, n, np.minimum(large, REL_BUCKETS - 1))


def _bias_rows(rel_bias):
    rel = (-np.arange(2 * WINDOW)) % (2 * WINDOW)
    idx = np.where(rel < WINDOW, _rel_bucket_static(rel), REL_BUCKETS)
    table = jnp.concatenate([rel_bias.astype(F32), jnp.full((1, ATTN_HEADS), -jnp.inf, F32)], axis=0)
    return table[idx].T


def _merge_kernel(ys_ref, ya_ref, g_ref, x_ref, wbs_ref, wba_ref, wmix_ref, lg_ref, lb_ref, h_ref):
    chunk = MXU_WIDTH
    rb = 2 * ROW_BLOCK
    for r in range(0, ys_ref.shape[0], rb):
        rows = slice(r, r + rb)
        ys, ya = ys_ref[rows, :], ya_ref[rows, :]
        merged = []
        for c in range(0, D_MODEL, chunk):
            a = _dot(ys, wbs_ref[:, c:c + chunk])
            b = _dot(ya, wba_ref[:, c:c + chunk])
            merged.append((g_ref[rows, c:c + chunk].astype(F32) * a
                           + g_ref[rows, D_MODEL + c:D_MODEL + c + chunk].astype(F32) * b).astype(BF16))
        mix = _dot(jnp.concatenate(merged, axis=1), wmix_ref[...])
        h_ref[rows, :] = _layer_norm(DEEPNORM_ALPHA * x_ref[rows, :] + mix, lg_ref[...], lb_ref[...])


def _merge(y_ssm, y_attn, gates, x2d, w_bs, w_ba, w_mix, ln_g, ln_b, tm):
    s = x2d.shape[0]
    row = lambda i: (i, 0)
    const = lambda i: (0, 0)
    full = lambda a: pl.BlockSpec(a.shape, const)
    return pl.pallas_call(
        _merge_kernel,
        grid=(s // tm,),
        in_specs=[pl.BlockSpec((tm, SSM_D_INNER), row), pl.BlockSpec((tm, Q_COLS), row),
                  pl.BlockSpec((tm, GATE_COLS), row), pl.BlockSpec((tm, D_MODEL), row),
                  full(w_bs), full(w_ba), full(w_mix), full(ln_g), full(ln_b)],
        out_specs=pl.BlockSpec((tm, D_MODEL), row),
        out_shape=jax.ShapeDtypeStruct((s, D_MODEL), F32),
        compiler_params=pltpu.CompilerParams(dimension_semantics=("arbitrary",),
                                             vmem_limit_bytes=VMEM_LIMIT),
        name="merge",
    )(y_ssm, y_attn, gates, x2d, w_bs, w_ba, w_mix, ln_g, ln_b)


def _ffn_kernel(h_ref, wup_ref, cw_ref, cb_ref, wdn_ref, lg_ref, lb_ref, o_ref, halo_ref, act_ref):
    tm = h_ref.shape[0]

    @pl.when(pl.program_id(0) == 0)
    def _():
        halo_ref[...] = jnp.zeros_like(halo_ref)

    chunk = MXU_WIDTH
    rb = 2 * ROW_BLOCK
    row_blocks = [slice(r, r + rb) for r in range(0, tm, rb)]
    hbs = [h_ref[rows, :].astype(BF16) for rows in row_blocks]

    def conv_cols(hb, cols, prev):
        u = _dot(hb, wup_ref[:, cols])
        return _causal_conv(u, prev, cw_ref, cb_ref, cols, FFN_CONV), u[rb - HALO_ROWS:, :]

    for c in range(0, D_FF, chunk):
        gcols, vcols = slice(c, c + chunk), slice(D_FF + c, D_FF + c + chunk)
        gprev, vprev = halo_ref[:, gcols], halo_ref[:, vcols]
        for hb, rows in zip(hbs, row_blocks):
            gate, gprev = conv_cols(hb, gcols, gprev)
            val, vprev = conv_cols(hb, vcols, vprev)
            act_ref[rows, c:c + chunk] = (_silu(gate) * val).astype(BF16)
        halo_ref[:, gcols] = gprev
        halo_ref[:, vcols] = vprev

    for rows in row_blocks:
        out = _dot(act_ref[rows, :], wdn_ref[...])
        o_ref[rows, :] = _layer_norm(DEEPNORM_ALPHA * h_ref[rows, :] + out, lg_ref[...], lb_ref[...])


def _ffn(h1, w_up, conv_w, conv_b, w_down, ln_g, ln_b, tm):
    s = h1.shape[0]
    row = lambda i: (i, 0)
    const = lambda i: (0, 0)
    full = lambda a: pl.BlockSpec(a.shape, const)
    return pl.pallas_call(
        _ffn_kernel,
        grid=(s // tm,),
        in_specs=[pl.BlockSpec((tm, D_MODEL), row),
                  pl.BlockSpec(w_up.shape, const, pipeline_mode=pl.Buffered(1)),
                  full(conv_w), full(conv_b),
                  pl.BlockSpec(w_down.shape, const, pipeline_mode=pl.Buffered(1)),
                  full(ln_g), full(ln_b)],
        out_specs=pl.BlockSpec((tm, D_MODEL), row),
        out_shape=jax.ShapeDtypeStruct((s, D_MODEL), F32),
        scratch_shapes=[pltpu.VMEM((HALO_ROWS, 2 * D_FF), F32),
                        pltpu.VMEM((tm, D_FF), BF16)],
        compiler_params=pltpu.CompilerParams(dimension_semantics=("arbitrary",),
                                             vmem_limit_bytes=VMEM_LIMIT),
        name="ffn",
    )(h1, w_up, conv_w, conv_b, w_down, ln_g, ln_b)


PREP_BLOCK = 512
DT_LO = SSM_D_INNER + SSM_CONV_DIM
DT_DST = PROJ_COLS - DT_PAD


def _permute_kernel(a_ref, b_ref, d_ref, o_ref):
    j = pl.program_id(0)
    first_shifted = DT_LO // PREP_BLOCK
    last = (PROJ_COLS - 1) // PREP_BLOCK

    @pl.when(j < first_shifted)
    def _():
        o_ref[...] = a_ref[...].T.astype(BF16)

    @pl.when((j >= first_shifted) & (j < last))
    def _():
        rows = jnp.concatenate([a_ref[SSM_HEADS:, :], b_ref[:SSM_HEADS, :]], axis=0)
        o_ref[...] = rows.T.astype(BF16)

    @pl.when(j == last)
    def _():
        n_tail = DT_DST - last * PREP_BLOCK
        rows = jnp.concatenate([a_ref[SSM_HEADS:SSM_HEADS + n_tail, :], d_ref[:SSM_HEADS, :],
                                jnp.zeros((DT_PAD - SSM_HEADS, D_MODEL), F32)], axis=0)
        o_ref[:, :n_tail + DT_PAD] = rows.T.astype(BF16)


def _permute_w_in(w_t):
    n_src = DT_LO + SSM_HEADS + QKV_COLS + GATE_COLS
    assert DT_LO % PREP_BLOCK == 0 and w_t.shape == (n_src, D_MODEL)
    first_shifted = DT_LO // PREP_BLOCK
    last_src = (n_src - 1) // PREP_BLOCK
    blk = lambda f: pl.BlockSpec((PREP_BLOCK, D_MODEL), f)
    return pl.pallas_call(
        _permute_kernel,
        grid=(pl.cdiv(PROJ_COLS, PREP_BLOCK),),
        in_specs=[blk(lambda j: (jnp.minimum(j, last_src), 0)),
                  blk(lambda j: (jnp.clip(j + 1, first_shifted, last_src), 0)),
                  blk(lambda j: (first_shifted, 0))],
        out_specs=pl.BlockSpec((D_MODEL, PREP_BLOCK), lambda j: (0, j)),
        out_shape=jax.ShapeDtypeStruct((D_MODEL, PROJ_COLS), BF16),
        compiler_params=pltpu.CompilerParams(dimension_semantics=("arbitrary",),
                                             vmem_limit_bytes=VMEM_LIMIT),
        name="permute_w_in",
    )(w_t, w_t, w_t)


def _row(v, width=None):
    v = v.astype(F32).reshape(1, -1)
    if width is not None and v.shape[1] < width:
        v = jnp.pad(v, ((0, 0), (0, width - v.shape[1])))
    return v


def kernel(x, rel_bias, w_in, b_gate, ssm_conv_w, ssm_conv_b, ssm_dt_bias, ssm_a_log, ssm_d, ssm_norm_w,
           attn_sinks, w_branch_ssm, w_branch_attn, w_mix_out, ln1_g, ln1_b, w_up, ffn_conv_w, ffn_conv_b,
           w_down, ln2_g, ln2_b):
    b, s, d = x.shape
    assert (b, d) == (1, D_MODEL) and s % 512 == 0 and w_in.shape[0] == DEPTH
    t = _tiles()
    h = x.reshape(s, d)
    bias_rows = _bias_rows(rel_bias)
    for l in range(DEPTH):
        zs, xc, qkv, gates, dt = _inproj(h, _permute_w_in(w_in[l].T), ssm_conv_w[l].astype(F32),
                                         _row(ssm_conv_b[l]), _row(ssm_dt_bias[l], DT_PAD),
                                         _row(b_gate[l]), t["inproj"])
        y_ssm, y_attn = _mixer(xc, zs, dt, qkv, _row(ssm_a_log[l], DT_PAD),
                               _row(jnp.repeat(ssm_d[l], SSM_HEADDIM)), _row(ssm_norm_w[l]),
                               bias_rows, attn_sinks[l].astype(F32), t["mixer"])
        h1 = _merge(y_ssm, y_attn, gates, h, w_branch_ssm[l].astype(BF16), w_branch_attn[l].astype(BF16),
                    w_mix_out[l].astype(BF16), _row(ln1_g[l]), _row(ln1_b[l]), t["merge"])
        h = _ffn(h1, w_up[l].astype(BF16), ffn_conv_w[l].astype(F32), _row(ffn_conv_b[l]),
                 w_down[l].astype(BF16), _row(ln2_g[l]), _row(ln2_b[l]), t["ffn"])
    return h.reshape(b, s, d)
```

```python
import math

import numpy as np
import jax
import jax.numpy as jnp
from jax import lax
from jax.experimental import pallas as pl
from jax.experimental.pallas import tpu as pltpu

F32 = jnp.float32
BF16 = jnp.bfloat16

D_MODEL = 1024
SSM_D_INNER = 2048
SSM_HEADDIM = 64
SSM_HEADS = 32
SSM_GROUPS = 4
SSM_HEADS_PER_GROUP = 8
SSM_STATE = 128
SSM_CONV = 4
SSM_CHUNK = 128
SSM_CONV_DIM = SSM_D_INNER + 2 * SSM_GROUPS * SSM_STATE
GROUP_WIDTH = SSM_HEADS_PER_GROUP * SSM_HEADDIM
ATTN_HEADS = 16
ATTN_KV_HEADS = 2
ATTN_HEADDIM = 64
WINDOW = 128
REL_BUCKETS = 32
REL_MAX_DIST = 128
Q_COLS = ATTN_HEADS * ATTN_HEADDIM
KV_COLS = ATTN_KV_HEADS * ATTN_HEADDIM
QKV_COLS = Q_COLS + 2 * KV_COLS
GATE_COLS = 2 * D_MODEL
D_FF = 2816
FFN_CONV = 3
DEPTH = 1
DEEPNORM_ALPHA = (2.0 * DEPTH) ** 0.25
LN_EPS = 1e-5
RMS_EPS = 1e-5

LANES = 128
SUBLANES = 8
DT_PAD = LANES
HALO_ROWS = SUBLANES
SSD_LOOKAHEAD = 1
ATTN_LOOKAHEAD = 1
MXU_WIDTH = 256
ROW_BLOCK = MXU_WIDTH // 2
VMEM_LIMIT = 56 * 1024 * 1024

PROJ_SEGMENTS = (SSM_D_INNER, SSM_CONV_DIM, QKV_COLS, GATE_COLS, DT_PAD)
PROJ_COLS = sum(PROJ_SEGMENTS)

NEG_BIG = -1e30
LOG2E = 1.4426950408889634


def _tiles():
    assert SSM_CHUNK == WINDOW
    return dict(inproj=512, mixer=4 * SSM_CHUNK, merge=512, ffn=1024)


def _dot(a, b):
    return jnp.dot(a, b, preferred_element_type=F32)


def _sigmoid(x):
    return 1.0 / (1.0 + jnp.exp2(x * (-LOG2E)))


def _silu(x):
    h = 0.5 * x
    return h + h * jnp.tanh(h)


def _layer_norm(r, g, b):
    mu = jnp.mean(r, axis=-1, keepdims=True)
    rc = r - mu
    var = jnp.mean(rc * rc, axis=-1, keepdims=True)
    return rc * lax.rsqrt(var + LN_EPS) * g + b


def _shift_rows(x, prev, j):
    row = lax.broadcasted_iota(jnp.int32, prev.shape, 0)
    sh = pltpu.roll(x, j, 0)
    top = jnp.where(row < j, pltpu.roll(prev, j, 0), sh[:HALO_ROWS])
    return jnp.concatenate([top, sh[HALO_ROWS:]], axis=0)


def _causal_conv(x, prev, w_ref, b_ref, cols, taps):
    acc = b_ref[:, cols] + w_ref[taps - 1:taps, cols] * x
    for j in range(1, taps):
        acc = acc + w_ref[taps - 1 - j:taps - j, cols] * _shift_rows(x, prev, j)
    return acc


def _inproj_kernel(x_ref, w_ref, cw_ref, cb_ref, dtb_ref, bg_ref,
                   zs_ref, xc_ref, qkv_ref, g_ref, dt_ref, halo_ref):
    tm = x_ref.shape[0]

    @pl.when(pl.program_id(0) == 0)
    def _():
        halo_ref[...] = jnp.zeros_like(halo_ref)

    chunk = MXU_WIDTH
    row_blocks = [slice(r, r + ROW_BLOCK) for r in range(0, tm, ROW_BLOCK)]
    xbs = [x_ref[rows, :].astype(BF16) for rows in row_blocks]
    offs = [sum(PROJ_SEGMENTS[:i]) for i in range(len(PROJ_SEGMENTS))]

    def proj(i, seg, c, w):
        return _dot(xbs[i], w_ref[:, offs[seg] + c:offs[seg] + c + w])

    wide_blocks = [slice(r, r + 2 * ROW_BLOCK) for r in range(0, tm, 2 * ROW_BLOCK)]
    xbw = [jnp.concatenate(xbs[2 * i:2 * i + 2], axis=0) for i in range(len(wide_blocks))]

    def proj_wide(i, seg, c, w):
        return _dot(xbw[i], w_ref[:, offs[seg] + c:offs[seg] + c + w])

    for c in range(0, SSM_D_INNER, chunk):
        for i, rows in enumerate(wide_blocks):
            zs_ref[rows, c:c + chunk] = _silu(proj_wide(i, 0, c, chunk)).astype(BF16)
    for c in range(0, SSM_CONV_DIM, chunk):
        cols = slice(c, c + chunk)
        prev = halo_ref[:, cols]
        for i, rows in enumerate(row_blocks):
            u = proj(i, 1, c, chunk)
            xc_ref[rows, cols] = _silu(_causal_conv(u, prev, cw_ref, cb_ref, cols, SSM_CONV)).astype(BF16)
            prev = u[ROW_BLOCK - HALO_ROWS:, :]
        halo_ref[:, cols] = prev
    for c in range(0, QKV_COLS, chunk):
        w = min(chunk, QKV_COLS - c)
        for i, rows in enumerate(wide_blocks):
            qkv_ref[rows, c:c + w] = proj_wide(i, 2, c, w).astype(BF16)
    for c in range(0, GATE_COLS, chunk):
        for i, rows in enumerate(wide_blocks):
            g_ref[rows, c:c + chunk] = _sigmoid(proj_wide(i, 3, c, chunk) + bg_ref[:, c:c + chunk]).astype(BF16)
    lane_ok = lax.broadcasted_iota(jnp.int32, (ROW_BLOCK, DT_PAD), 1) < SSM_HEADS
    for i, rows in enumerate(row_blocks):
        dtr = proj(i, 4, 0, DT_PAD) + dtb_ref[...]
        softplus = jnp.maximum(dtr, 0.0) + jnp.log(1.0 + jnp.exp(-jnp.abs(dtr)))
        dt_ref[rows, :] = jnp.where(lane_ok, softplus, 0.0)


def _inproj(x2d, w_perm, conv_w, conv_b, dt_bias, b_gate, tm):
    s = x2d.shape[0]
    row = lambda i: (i, 0)
    const = lambda i: (0, 0)
    full = lambda a: pl.BlockSpec(a.shape, const)
    outs = [jax.ShapeDtypeStruct((s, n), dt) for n, dt in
            zip(PROJ_SEGMENTS, (BF16, BF16, BF16, BF16, F32))]
    return pl.pallas_call(
        _inproj_kernel,
        grid=(s // tm,),
        in_specs=[pl.BlockSpec((tm, D_MODEL), row),
                  pl.BlockSpec((D_MODEL, PROJ_COLS), const, pipeline_mode=pl.Buffered(1)),
                  full(conv_w), full(conv_b), full(dt_bias), full(b_gate)],
        out_specs=[pl.BlockSpec((tm, n), row) for n in PROJ_SEGMENTS],
        out_shape=outs,
        scratch_shapes=[pltpu.VMEM((HALO_ROWS, SSM_CONV_DIM), F32)],
        compiler_params=pltpu.CompilerParams(dimension_semantics=("arbitrary",),
                                             vmem_limit_bytes=VMEM_LIMIT),
        name="inproj",
    )(x2d, w_perm, conv_w, conv_b, dt_bias, b_gate)


def _split3(x):
    hi = x.astype(BF16).astype(F32)
    r = x - hi
    mid = r.astype(BF16).astype(F32)
    lo = (r - mid).astype(BF16).astype(F32)
    return hi, mid, lo


def _pack3(x):
    hi, mid, lo = _split3(x)
    return (hi + pltpu.roll(mid, SSM_HEADS, 1) + pltpu.roll(lo, 2 * SSM_HEADS, 1)).astype(BF16)


def _ssd_body(conv_ref, zs_ref, dt_ref, alog_ref, dexp_ref, nw_ref, e3_ref, y_ref, state_ref):
    L = SSM_CHUNK
    lane_ok = lax.broadcasted_iota(jnp.int32, (L, DT_PAD), 1) < SSM_HEADS
    dt = dt_ref[...]
    a_dt = dt * (-LOG2E * jnp.exp(alog_ref[...]))

    ri = lax.broadcasted_iota(jnp.int32, (L, L), 0)
    ci = lax.broadcasted_iota(jnp.int32, (L, L), 1)
    causal = ri >= ci
    tril = jnp.where(causal, 1.0, 0.0).astype(BF16)
    hi, mid, lo = _split3(a_dt)
    a_cs = _dot(tril, hi.astype(BF16)) + _dot(tril, mid.astype(BF16)) + _dot(tril, lo.astype(BF16))

    a_last = a_cs[L - 1:L, :]
    decay_out_b = jnp.where(lane_ok, jnp.exp2(a_cs), 0.0).astype(BF16)
    w_state_b = (dt * jnp.exp2(a_last - a_cs)).astype(BF16)
    cd_e = jnp.exp2(_dot(_pack3(a_cs[L - SUBLANES:, :]), e3_ref[...])[SUBLANES - 1:, :])

    a_sub_t = (a_cs - jnp.log2(dt)).T
    lane_lo = lax.broadcasted_iota(jnp.int32, (L, LANES), 1) < SSM_HEADDIM

    def group_start(g):
        gsl = slice(g * GROUP_WIDTH, (g + 1) * GROUP_WIDTH)
        xs_g = conv_ref[:, gsl].astype(F32)
        b_off = SSM_D_INNER + g * SSM_STATE
        c_off = SSM_D_INNER + SSM_GROUPS * SSM_STATE + g * SSM_STATE
        b_t = conv_ref[:, b_off:b_off + SSM_STATE].astype(F32).T.astype(BF16)
        c_g = conv_ref[:, c_off:c_off + SSM_STATE]
        cb = _dot(c_g, b_t)
        xdtd_b = (xs_g * _dot(w_state_b, e3_ref[:, gsl])).astype(BF16)
        st = state_ref[g]
        y_off = _dot(c_g, st.astype(BF16)) * _dot(decay_out_b, e3_ref[:, gsl])
        state_ref[g] = st * cd_e[:, gsl] + _dot(b_t, xdtd_b)
        return cb, y_off + xs_g * dexp_ref[:, gsl]

    def acol_dot(p):
        return jnp.concatenate([jnp.broadcast_to(a_cs[:, 2 * p + k:2 * p + k + 1], (L, L)) for k in range(2)],
                               axis=1)

    pairs_per_group = SSM_HEADS_PER_GROUP // 2
    n_pairs = SSM_HEADS // 2
    started = {0: group_start(0)}
    acols = {p: acol_dot(p) for p in range(SSD_LOOKAHEAD)}
    pieces = []
    for p in range(n_pairs):
        g, pr = divmod(p, pairs_per_group)
        if p + SSD_LOOKAHEAD < n_pairs:
            acols[p + SSD_LOOKAHEAD] = acol_dot(p + SSD_LOOKAHEAD)
        if pr == pairs_per_group - 2 and g + 1 < SSM_GROUPS:
            started[g + 1] = group_start(g + 1)
        cb, y_rest = started[g]
        acol = acols.pop(p)
        ms = []
        for k in range(2):
            seg = acol[:, k * L:(k + 1) * L] - a_sub_t[2 * p + k:2 * p + k + 1, :]
            ms.append((cb * jnp.exp2(jnp.where(causal, seg, NEG_BIG))).astype(BF16))
        xp = conv_ref[:, p * LANES:(p + 1) * LANES]
        zero = jnp.zeros_like(xp)
        rhs = jnp.concatenate([jnp.where(lane_lo, xp, zero), jnp.where(lane_lo, zero, xp)], axis=0)
        pieces.append(_dot(jnp.concatenate(ms, axis=1), rhs))
        if pr == pairs_per_group - 1:
            gsl = slice(g * GROUP_WIDTH, (g + 1) * GROUP_WIDTH)
            yz = (jnp.concatenate(pieces, axis=1) + y_rest) * zs_ref[:, gsl].astype(F32)
            ms_ = jnp.mean(yz * yz, axis=-1, keepdims=True)
            y_ref[:, gsl] = (yz * lax.rsqrt(ms_ + RMS_EPS) * nw_ref[:, gsl]).astype(BF16)
            pieces = []
            del started[g]
        yield


def _expansion_matrix():
    k = np.arange(LANES)[:, None]
    valid = k < 3 * SSM_HEADS
    head = k % SSM_HEADS
    e3 = (valid & (head == (np.arange(SSM_D_INNER)[None, :] // SSM_HEADDIM))).astype(np.float32)
    return jnp.asarray(e3, BF16)


def _build_bias_mask(ur_ref, bm_ref):
    W = WINDOW
    first = lax.broadcasted_iota(jnp.int32, (W, 2 * W), 1) >= W
    for h in range(ATTN_HEADS):
        band_h = LOG2E * pltpu.roll(jnp.broadcast_to(ur_ref[h:h + 1, :], (W, 2 * W)), W, 1,
                                    stride=1, stride_axis=0)
        bm_ref[1, h] = band_h
        bm_ref[0, h] = jnp.where(first, band_h, -jnp.inf)


def _attn_body(sink_ref, q_ref, kvc_ref, kvp_ref, o_ref, bm_ref, slab):
    W = WINDOW
    lane_lo2 = lax.broadcasted_iota(jnp.int32, (2 * W, LANES), 1) < ATTN_HEADDIM
    lane_lo = lax.broadcasted_iota(jnp.int32, (W, LANES), 1) < ATTN_HEADDIM

    def band(lo_col):
        return jnp.concatenate([kvp_ref[:, lo_col:lo_col + KV_COLS].astype(F32),
                                kvc_ref[:, lo_col:lo_col + KV_COLS].astype(F32)], axis=0)

    kband = band(0) * (ATTN_HEADDIM ** -0.5 * LOG2E)
    vband = band(KV_COLS)
    vsw = pltpu.roll(vband, ATTN_HEADDIM, 1)
    k_t = kband.T
    k_t_sw = pltpu.roll(k_t, ATTN_HEADDIM, 0)
    row_lo = lax.broadcasted_iota(jnp.int32, (LANES, 2 * W), 0) < ATTN_HEADDIM
    k_var = ((jnp.where(row_lo, k_t, 0.0).astype(BF16), jnp.where(row_lo, 0.0, k_t_sw).astype(BF16)),
             (jnp.where(row_lo, k_t_sw, 0.0).astype(BF16), jnp.where(row_lo, 0.0, k_t).astype(BF16)))
    ones_bd = jnp.concatenate([jnp.where(lane_lo2, 1.0, 0.0), jnp.where(lane_lo2, 0.0, 1.0)], axis=0)
    v_bd = tuple(jnp.concatenate([jnp.concatenate([jnp.where(lane_lo2, top, 0.0), jnp.where(lane_lo2, 0.0, bot)],
                                                  axis=0), ones_bd], axis=1).astype(BF16)
                 for top, bot in ((vband, vsw), (vsw, vband)))

    def kv_head(i):
        return (2 * i) // (ATTN_HEADS // ATTN_KV_HEADS)

    def logits(i):
        qp = q_ref[:, i * LANES:(i + 1) * LANES]
        return [_dot(qp, k_var[kv_head(i)][j]) for j in range(2)]

    n_pairs = ATTN_HEADS // 2
    pending = {i: logits(i) for i in range(ATTN_LOOKAHEAD)}
    for i in range(n_pairs):
        c = kv_head(i)
        if i + ATTN_LOOKAHEAD < n_pairs:
            pending[i + ATTN_LOOKAHEAD] = logits(i + ATTN_LOOKAHEAD)
        s_cur = pending.pop(i)
        ps, sink_terms = [], []
        for j in range(2):
            h = 2 * i + j
            sink = sink_ref[h] * LOG2E
            s = s_cur[j] + bm_ref[slab, h]
            m = jnp.maximum(jnp.max(s, axis=-1, keepdims=True), sink)
            ps.append(jnp.exp2(s - m).astype(BF16))
            sink_terms.append(jnp.exp2(sink - m))
            if j == 0:
                yield
        pv =_dot(jnp.concatenate(ps, axis=1), v_bd[c])
        den = pv[:, LANES:] + jnp.where(lane_lo, sink_terms[0], sink_terms[1])
        o_ref[:, i * LANES:(i + 1) * LANES] = (pv[:, :LANES] / den).astype(BF16)
        yield


def _mixer_kernel(sink_ref, conv_ref, zs_ref, dt_ref, alog_ref, dexp_ref, nw_ref, e3_ref,
                  q_ref, kvc_ref, kvp_ref, ur_ref, ys_ref, ya_ref, state_ref, bm_ref):
    @pl.when(pl.program_id(0) == 0)
    def _():
        state_ref[...] = jnp.zeros_like(state_ref)
        _build_bias_mask(ur_ref, bm_ref)

    L = SSM_CHUNK
    live = []
    for blk in range(conv_ref.shape[0] // L):
        rows = pl.ds(blk * L, L)
        live.append(_ssd_body(conv_ref.at[rows], zs_ref.at[rows], dt_ref.at[rows], alog_ref, dexp_ref, nw_ref,
                              e3_ref, ys_ref.at[rows], state_ref))
        prev = kvp_ref if blk == 0 else kvc_ref.at[pl.ds((blk - 1) * L, L)]
        slab = jnp.minimum(pl.program_id(0), 1) if blk == 0 else 1
        live.append(_attn_body(sink_ref, q_ref.at[rows], kvc_ref.at[rows], prev, ya_ref.at[rows], bm_ref, slab))
    while live:
        for body in list(live):
            if next(body, "done") == "done":
                live.remove(body)


def _mixer(xc, zs, dt, qkv, a_log, d_exp, norm_w, bias_rows, sinks, tm):
    s = xc.shape[0]
    e3 = _expansion_matrix()
    row = lambda i: (i, 0)
    const = lambda i: (0, 0)
    full = lambda a: pl.BlockSpec(a.shape, const)
    kv_blk = Q_COLS // (2 * KV_COLS)
    return pl.pallas_call(
        _mixer_kernel,
        grid=(s // tm,),
        in_specs=[pl.BlockSpec(memory_space=pltpu.SMEM),
                  pl.BlockSpec((tm, SSM_CONV_DIM), row),
                  pl.BlockSpec((tm, SSM_D_INNER), row),
                  pl.BlockSpec((tm, DT_PAD), row),
                  full(a_log), full(d_exp), full(norm_w), full(e3),
                  pl.BlockSpec((tm, Q_COLS), row),
                  pl.BlockSpec((tm, 2 * KV_COLS), lambda i: (i, kv_blk)),
                  pl.BlockSpec((WINDOW, 2 * KV_COLS),
                               lambda i: (jnp.maximum(i * (tm // WINDOW) - 1, 0), kv_blk)),
                  full(bias_rows)],
        out_specs=[pl.BlockSpec((tm, SSM_D_INNER), row), pl.BlockSpec((tm, Q_COLS), row)],
        out_shape=[jax.ShapeDtypeStruct((s, SSM_D_INNER), BF16), jax.ShapeDtypeStruct((s, Q_COLS), BF16)],
        scratch_shapes=[pltpu.VMEM((SSM_GROUPS, SSM_STATE, GROUP_WIDTH), F32),
                        pltpu.VMEM((2, ATTN_HEADS, WINDOW, 2 * WINDOW), F32)],
        compiler_params=pltpu.CompilerParams(dimension_semantics=("arbitrary",),
                                             vmem_limit_bytes=VMEM_LIMIT),
        name="mixer",
    )(sinks, xc, zs, dt, a_log, d_exp, norm_w, e3, qkv, qkv, qkv, bias_rows)


def _rel_bucket_static(n):
    max_exact = REL_BUCKETS // 2
    nf = np.maximum(n, 1).astype(np.float32)
    large = max_exact + (np.log(nf / max_exact) / math.log(REL_MAX_DIST / max_exact)
                         * (REL_BUCKETS - max_exact)).astype(np.int32)
    return np.where(n < max_exact, n, np.minimum(large, REL_BUCKETS - 1))


def _bias_rows(rel_bias):
    rel = (-np.arange(2 * WINDOW)) % (2 * WINDOW)
    idx = np.where(rel < WINDOW, _rel_bucket_static(rel), REL_BUCKETS)
    table = jnp.concatenate([rel_bias.astype(F32), jnp.full((1, ATTN_HEADS), -jnp.inf, F32)], axis=0)
    return table[idx].T


def _merge_kernel(ys_ref, ya_ref, g_ref, x_ref, wbs_ref, wba_ref, wmix_ref, lg_ref, lb_ref, h_ref):
    chunk = MXU_WIDTH
    rb = 2 * ROW_BLOCK
    for r in range(0, ys_ref.shape[0], rb):
        rows = slice(r, r + rb)
        ys, ya = ys_ref[rows, :], ya_ref[rows, :]
        merged = []
        for c in range(0, D_MODEL, chunk):
            a = _dot(ys, wbs_ref[:, c:c + chunk])
            b = _dot(ya, wba_ref[:, c:c + chunk])
            merged.append((g_ref[rows, c:c + chunk].astype(F32) * a
                           + g_ref[rows, D_MODEL + c:D_MODEL + c + chunk].astype(F32) * b).astype(BF16))
        mix = _dot(jnp.concatenate(merged, axis=1), wmix_ref[...])
        h_ref[rows, :] = _layer_norm(DEEPNORM_ALPHA * x_ref[rows, :] + mix, lg_ref[...], lb_ref[...])


def _merge(y_ssm, y_attn, gates, x2d, w_bs, w_ba, w_mix, ln_g, ln_b, tm):
    s = x2d.shape[0]
    row = lambda i: (i, 0)
    const = lambda i: (0, 0)
    full = lambda a: pl.BlockSpec(a.shape, const)
    return pl.pallas_call(
        _merge_kernel,
        grid=(s // tm,),
        in_specs=[pl.BlockSpec((tm, SSM_D_INNER), row), pl.BlockSpec((tm, Q_COLS), row),
                  pl.BlockSpec((tm, GATE_COLS), row), pl.BlockSpec((tm, D_MODEL), row),
                  full(w_bs), full(w_ba), full(w_mix), full(ln_g), full(ln_b)],
        out_specs=pl.BlockSpec((tm, D_MODEL), row),
        out_shape=jax.ShapeDtypeStruct((s, D_MODEL), F32),
        compiler_params=pltpu.CompilerParams(dimension_semantics=("arbitrary",),
                                             vmem_limit_bytes=VMEM_LIMIT),
        name="merge",
    )(y_ssm, y_attn, gates, x2d, w_bs, w_ba, w_mix, ln_g, ln_b)


def _ffn_kernel(h_ref, wup_ref, cw_ref, cb_ref, wdn_ref, lg_ref, lb_ref, o_ref, halo_ref, act_ref):
    tm = h_ref.shape[0]

    @pl.when(pl.program_id(0) == 0)
    def _():
        halo_ref[...] = jnp.zeros_like(halo_ref)

    chunk = MXU_WIDTH
    rb = 2 * ROW_BLOCK
    row_blocks = [slice(r, r + rb) for r in range(0, tm, rb)]
    hbs = [h_ref[rows, :].astype(BF16) for rows in row_blocks]

    def conv_cols(hb, cols, prev):
        u = _dot(hb, wup_ref[:, cols])
        return _causal_conv(u, prev, cw_ref, cb_ref, cols, FFN_CONV), u[rb - HALO_ROWS:, :]

    for c in range(0, D_FF, chunk):
        gcols, vcols = slice(c, c + chunk), slice(D_FF + c, D_FF + c + chunk)
        gprev, vprev = halo_ref[:, gcols], halo_ref[:, vcols]
        for hb, rows in zip(hbs, row_blocks):
            gate, gprev = conv_cols(hb, gcols, gprev)
            val, vprev = conv_cols(hb, vcols, vprev)
            act_ref[rows, c:c + chunk] = (_silu(gate) * val).astype(BF16)
        halo_ref[:, gcols] = gprev
        halo_ref[:, vcols] = vprev

    for rows in row_blocks:
        out = _dot(act_ref[rows, :], wdn_ref[...])
        o_ref[rows, :] = _layer_norm(DEEPNORM_ALPHA * h_ref[rows, :] + out, lg_ref[...], lb_ref[...])


def _ffn(h1, w_up, conv_w, conv_b, w_down, ln_g, ln_b, tm):
    s = h1.shape[0]
    row = lambda i: (i, 0)
    const = lambda i: (0, 0)
    full = lambda a: pl.BlockSpec(a.shape, const)
    return pl.pallas_call(
        _ffn_kernel,
        grid=(s // tm,),
        in_specs=[pl.BlockSpec((tm, D_MODEL), row),
                  pl.BlockSpec(w_up.shape, const, pipeline_mode=pl.Buffered(1)),
                  full(conv_w), full(conv_b),
                  pl.BlockSpec(w_down.shape, const, pipeline_mode=pl.Buffered(1)),
                  full(ln_g), full(ln_b)],
        out_specs=pl.BlockSpec((tm, D_MODEL), row),
        out_shape=jax.ShapeDtypeStruct((s, D_MODEL), F32),
        scratch_shapes=[pltpu.VMEM((HALO_ROWS, 2 * D_FF), F32),
                        pltpu.VMEM((tm, D_FF), BF16)],
        compiler_params=pltpu.CompilerParams(dimension_semantics=("arbitrary",),
                                             vmem_limit_bytes=VMEM_LIMIT),
        name="ffn",
    )(h1, w_up, conv_w, conv_b, w_down, ln_g, ln_b)


PREP_BLOCK = 512
DT_LO = SSM_D_INNER + SSM_CONV_DIM
DT_DST = PROJ_COLS - DT_PAD


def _permute_kernel(a_ref, b_ref, d_ref, o_ref):
    j = pl.program_id(0)
    first_shifted = DT_LO // PREP_BLOCK
    last = (PROJ_COLS - 1) // PREP_BLOCK

    @pl.when(j < first_shifted)
    def _():
        o_ref[...] = a_ref[...].T.astype(BF16)

    @pl.when((j >= first_shifted) & (j < last))
    def _():
        rows = jnp.concatenate([a_ref[SSM_HEADS:, :], b_ref[:SSM_HEADS, :]], axis=0)
        o_ref[...] = rows.T.astype(BF16)

    @pl.when(j == last)
    def _():
        n_tail = DT_DST - last * PREP_BLOCK
        rows = jnp.concatenate([a_ref[SSM_HEADS:SSM_HEADS + n_tail, :], d_ref[:SSM_HEADS, :],
                                jnp.zeros((DT_PAD - SSM_HEADS, D_MODEL), F32)], axis=0)
        o_ref[:, :n_tail + DT_PAD] = rows.T.astype(BF16)


def _permute_w_in(w_t):
    n_src = DT_LO + SSM_HEADS + QKV_COLS + GATE_COLS
    assert DT_LO % PREP_BLOCK == 0 and w_t.shape == (n_src, D_MODEL)
    first_shifted = DT_LO // PREP_BLOCK
    last_src = (n_src - 1) // PREP_BLOCK
    blk = lambda f: pl.BlockSpec((PREP_BLOCK, D_MODEL), f)
    return pl.pallas_call(
        _permute_kernel,
        grid=(pl.cdiv(PROJ_COLS, PREP_BLOCK),),
        in_specs=[blk(lambda j: (jnp.minimum(j, last_src), 0)),
                  blk(lambda j: (jnp.clip(j + 1, first_shifted, last_src), 0)),
                  blk(lambda j: (first_shifted, 0))],
        out_specs=pl.BlockSpec((D_MODEL, PREP_BLOCK), lambda j: (0, j)),
        out_shape=jax.ShapeDtypeStruct((D_MODEL, PROJ_COLS), BF16),
        compiler_params=pltpu.CompilerParams(dimension_semantics=("arbitrary",),
                                             vmem_limit_bytes=VMEM_LIMIT),
        name="permute_w_in",
    )(w_t, w_t, w_t)


def _row(v, width=None):
    v = v.astype(F32).reshape(1, -1)
    if width is not None and v.shape[1] < width:
        v = jnp.pad(v, ((0, 0), (0, width - v.shape[1])))
    return v


def kernel(x, rel_bias, w_in, b_gate, ssm_conv_w, ssm_conv_b, ssm_dt_bias, ssm_a_log, ssm_d, ssm_norm_w,
           attn_sinks, w_branch_ssm, w_branch_attn, w_mix_out, ln1_g, ln1_b, w_up, ffn_conv_w, ffn_conv_b,
           w_down, ln2_g, ln2_b):
    b, s, d = x.shape
    assert (b, d) == (1, D_MODEL) and s % 512 == 0 and w_in.shape[0] == DEPTH
    t = _tiles()
    h = x.reshape(s, d)
    bias_rows = _bias_rows(rel_bias)
    for l in range(DEPTH):
        zs, xc, qkv, gates, dt = _inproj(h, _permute_w_in(w_in[l].T), ssm_conv_w[l].astype(F32),
                                         _row(ssm_conv_b[l]), _row(ssm_dt_bias[l], DT_PAD),
                                         _row(b_gate[l]), t["inproj"])
        y_ssm, y_attn = _mixer(xc, zs, dt, qkv, _row(ssm_a_log[l], DT_PAD),
                               _row(jnp.repeat(ssm_d[l], SSM_HEADDIM)), _row(ssm_norm_w[l]),
                               bias_rows, attn_sinks[l].astype(F32), t["mixer"])
        h1 = _merge(y_ssm, y_attn, gates, h, w_branch_ssm[l].astype(BF16), w_branch_attn[l].astype(BF16),
                    w_mix_out[l].astype(BF16), _row(ln1_g[l]), _row(ln1_b[l]), t["merge"])
        h = _ffn(h1, w_up[l].astype(BF16), ffn_conv_w[l].astype(F32), _row(ffn_conv_b[l]),
                 w_down[l].astype(BF16), _row(ln2_g[l]), _row(ln2_b[l]), t["ffn"])
    return h.reshape(b, s, d)
```
